```python
import math
import jax
import jax.numpy as jnp
from jax import lax
import numpy as np

D_MODEL = 2048
BATCH = 1
SEQ = 8192
DEPTH = 4

GRID_W = 64
CTX_LEN = 256
N_MIXERS = 3
EPS = 1e-6
HALF_STEP = 0.5
N_MOD = 9
D_FF = 5632

NA_HEADS = 16
NA_HEAD_DIM = 128
NA_WIN_ROWS = 8
NA_WIN_COLS = 16

SSM_D_INNER = 2 * D_MODEL
SSM_HEAD_DIM = 64
SSM_HEADS = SSM_D_INNER // SSM_HEAD_DIM
SSM_GROUPS = 8
SSM_STATE = 128
SSM_CONV_W = 5
SSM_CHUNK = 128
SSM_GN = SSM_GROUPS * SSM_STATE
SSM_CONV_DIM = SSM_D_INNER + 2 * SSM_GN
SSM_IN_DIM = SSM_D_INNER + SSM_CONV_DIM + 2 * SSM_HEADS

GQA_HEADS = 16
GQA_KV_HEADS = 4
GQA_HEAD_DIM = 128
ROPE_THETA = 10000.0
Q_BLOCK = 128

N_A = (DEPTH + 2) // 3
N_B = (DEPTH + 1) // 3
N_C = DEPTH // 3

kernel_name = "hybrid_interleaved_na_ssd_gqa_macaron_prefix"


def rmsnorm(x, g):
    xf = x.astype(jnp.float32)
    var = jnp.mean(xf * xf, axis=-1, keepdims=True)
    return (xf * lax.rsqrt(var + EPS)).astype(x.dtype) * g


def modulate(x, shift, scale):
    return x * (1 + scale) + shift


def swiglu(x, w_in, w_out):
    a, b = jnp.split(x @ w_in, 2, axis=-1)
    return (jax.nn.silu(a) * b) @ w_out


def axial_rope(n_tok, head_dim):
    t = jnp.arange(n_tok, dtype=jnp.int32)
    row = (t // GRID_W).astype(jnp.float32)
    col = (t % GRID_W).astype(jnp.float32)
    n_freq = head_dim // 4
    inv_freq = ROPE_THETA ** (-jnp.arange(n_freq, dtype=jnp.float32) / n_freq)
    ang = jnp.concatenate([row[:, None] * inv_freq, col[:, None] * inv_freq], axis=-1)
    return jnp.cos(ang), jnp.sin(ang)


def apply_rope(x, cos, sin):
    half = x.shape[-1] // 2
    shape = (1, x.shape[1]) + (1,) * (x.ndim - 3) + (half,)
    cos = cos.reshape(shape).astype(x.dtype)
    sin = sin.reshape(shape).astype(x.dtype)
    x1, x2 = x[..., :half], x[..., half:]
    return jnp.concatenate([x1 * cos - x2 * sin, x1 * sin + x2 * cos], axis=-1)


def attend(q, k, v):
    scale = q.shape[-1] ** -0.5
    s = jnp.einsum('bqkgd,bskd->bkgqs', q, k, preferred_element_type=jnp.float32) * scale
    p = jax.nn.softmax(s, axis=-1).astype(v.dtype)
    return jnp.einsum('bkgqs,bskd->bqkgd', p, v)


def neighbourhood_attention(u, uc, w_qkv, rpb, w_o, with_ctx_out):
    bsz, n_tok, _ = u.shape
    n_ctx = uc.shape[1]
    rows = n_tok // GRID_W
    wr = min(NA_WIN_ROWS, rows)
    wc = NA_WIN_COLS
    n_win = wr * wc
    scale = NA_HEAD_DIM ** -0.5
    qkv = (u @ w_qkv).reshape(bsz, n_tok, 3, NA_HEADS, NA_HEAD_DIM)
    qkv_c = (uc @ w_qkv).reshape(bsz, n_ctx, 3, NA_HEADS, NA_HEAD_DIM)
    qc, kc, vc = qkv_c[:, :, 0], qkv_c[:, :, 1], qkv_c[:, :, 2]
    grid = (bsz, rows, GRID_W, NA_HEADS, NA_HEAD_DIM)
    q_grid = qkv[:, :, 0].reshape(grid)
    k_grid = qkv[:, :, 1].reshape(grid)
    v_grid = qkv[:, :, 2].reshape(grid)
    row_start = jnp.clip(jnp.arange(rows) - wr // 2, 0, rows - wr)
    col_start = jnp.clip(jnp.arange(GRID_W) - wc // 2, 0, GRID_W - wc)
    col_idx = col_start[:, None] + jnp.arange(wc)[None, :]
    dcol = col_idx - jnp.arange(GRID_W)[:, None]
    bias_cols = rpb[:, :, dcol + wc - 1]

    def row_block(r):
        rs = row_start[r]
        q_r = q_grid[:, r]
        k_band = lax.dynamic_slice_in_dim(k_grid, rs, wr, axis=1)
        v_band = lax.dynamic_slice_in_dim(v_grid, rs, wr, axis=1)
        win_shape = (bsz, GRID_W, n_win, NA_HEADS, NA_HEAD_DIM)
        k_win = k_band[:, :, col_idx].transpose(0, 2, 1, 3, 4, 5).reshape(win_shape)
        v_win = v_band[:, :, col_idx].transpose(0, 2, 1, 3, 4, 5).reshape(win_shape)
        drow = rs + jnp.arange(wr) - r
        bias = bias_cols[:, drow + NA_WIN_ROWS - 1].transpose(0, 2, 1, 3).reshape(NA_HEADS, GRID_W, n_win)
        s_win = jnp.einsum('bqhd,bqkhd->bhqk', q_r, k_win, preferred_element_type=jnp.float32) * scale + bias
        s_ctx = jnp.einsum('bqhd,bkhd->bhqk', q_r, kc, preferred_element_type=jnp.float32) * scale
        p = jax.nn.softmax(jnp.concatenate([s_win, s_ctx], axis=-1), axis=-1).astype(v_win.dtype)
        return (jnp.einsum('bhqk,bqkhd->bqhd', p[..., :n_win], v_win)
                + jnp.einsum('bhqk,bkhd->bqhd', p[..., n_win:], vc))

    o = lax.map(row_block, jnp.arange(rows))
    y = jnp.moveaxis(o, 0, 1).reshape(bsz, n_tok, NA_HEADS * NA_HEAD_DIM) @ w_o
    yc = None
    if with_ctx_out:
        oc = attend(qc[:, :, :, None], kc, vc)
        yc = oc.reshape(bsz, n_ctx, NA_HEADS * NA_HEAD_DIM) @ w_o
    return y, yc


def centred_dwconv(x, w, b):
    pad = (SSM_CONV_W - 1) // 2
    y = lax.conv_general_dilated(x, w[:, None, :], window_strides=(1,), padding=[(pad, pad)],
                                 dimension_numbers=('NWC', 'WIO', 'NWC'), feature_group_count=x.shape[-1])
    return y + b


def segsum(x):
    t = x.shape[-1]
    cs = jnp.cumsum(x, axis=-1)
    seg = cs[..., :, None] - cs[..., None, :]
    return jnp.where(jnp.tril(jnp.ones((t, t), dtype=bool)), seg, -jnp.inf)


def ssd_chunked(x, dt, a, bm, cm, init_state):
    bsz, length, n_heads, p = x.shape
    g, n = bm.shape[2], bm.shape[3]
    r = n_heads // g
    q = SSM_CHUNK
    nc = length // q
    f32 = jnp.float32
    dt = dt.astype(f32)
    xdt = (x.astype(f32) * dt[..., None]).reshape(bsz, nc, q, g, r, p)
    da = (dt * a.astype(f32)).reshape(bsz, nc, q, g, r).transpose(0, 3, 4, 1, 2)
    bc = bm.astype(f32).reshape(bsz, nc, q, g, n)
    cc = cm.astype(f32).reshape(bsz, nc, q, g, n)
    a_cs = jnp.cumsum(da, axis=-1)
    decay_in = jnp.exp(segsum(da))
    cb = jnp.einsum('bclgn,bcsgn->bgcls', cc, bc)
    y_diag = jnp.einsum('bgcls,bgrcls,bcsgrp->bclgrp', cb, decay_in, xdt)
    decay_to_end = jnp.exp(a_cs[..., -1:] - a_cs)
    chunk_states = jnp.einsum('bcsgn,bgrcs,bcsgrp->bcgrpn', bc, decay_to_end, xdt)
    chunk_states = jnp.concatenate([init_state.astype(f32).reshape(bsz, 1, g, r, p, n), chunk_states], axis=1)
    chunk_decay = jnp.exp(segsum(jnp.pad(a_cs[..., -1], ((0, 0), (0, 0), (0, 0), (1, 0)))))
    states = jnp.einsum('bgrzc,bcgrpn->bzgrpn', chunk_decay, chunk_states)
    y_off = jnp.einsum('bclgn,bcgrpn,bgrcl->bclgrp', cc, states[:, :-1], jnp.exp(a_cs))
    y = (y_diag + y_off).reshape(bsz, length, n_heads, p)
    return y, states[:, -1].reshape(bsz, n_heads, p, n)


def mamba2_bidirectional(u, uc, w_in, conv_w, conv_b, a_log, dt_bias, d_skip, norm_g, w_out):
    bsz, n_tok, _ = u.shape

    def project(h):
        length = h.shape[1]
        zxbcdt = h @ w_in
        z = zxbcdt[..., :SSM_D_INNER]
        xbc = jax.nn.silu(centred_dwconv(zxbcdt[..., SSM_D_INNER:SSM_D_INNER + SSM_CONV_DIM], conv_w, conv_b))
        dt = zxbcdt[..., SSM_D_INNER + SSM_CONV_DIM:].reshape(bsz, length, 2, SSM_HEADS)
        xs = xbc[..., :SSM_D_INNER].reshape(bsz, length, SSM_HEADS, SSM_HEAD_DIM)
        bm = xbc[..., SSM_D_INNER:SSM_D_INNER + SSM_GN].reshape(bsz, length, SSM_GROUPS, SSM_STATE)
        cm = xbc[..., SSM_D_INNER + SSM_GN:].reshape(bsz, length, SSM_GROUPS, SSM_STATE)
        return z, xs, bm, cm, dt

    z, xs, bm, cm, dt = project(u)
    zc, xsc, bmc, cmc, dtc = project(uc)
    a = -jnp.exp(a_log.astype(jnp.float32))
    init = jnp.zeros((bsz, SSM_HEADS, SSM_HEAD_DIM, SSM_STATE), jnp.float32)
    y_parts, yc_parts = [], []
    for direction in range(2):
        if direction == 0:
            flip = lambda t: t
        else:
            flip = lambda t: jnp.flip(t, axis=1)
        delta = jax.nn.softplus(dt[..., direction, :].astype(jnp.float32) + dt_bias[direction])
        delta_c = jax.nn.softplus(dtc[..., direction, :].astype(jnp.float32) + dt_bias[direction])
        y_c, s_c = ssd_chunked(flip(xsc), flip(delta_c), a[direction], flip(bmc), flip(cmc), init)
        y_l, _ = ssd_chunked(flip(xs), flip(delta), a[direction], flip(bm), flip(cm), s_c)
        y_parts.append(flip(y_l) + d_skip[direction][:, None] * xs)
        yc_parts.append(flip(y_c) + d_skip[direction][:, None] * xsc)
    y = (y_parts[0] + y_parts[1]).astype(u.dtype).reshape(bsz, n_tok, SSM_D_INNER)
    yc = (yc_parts[0] + yc_parts[1]).astype(u.dtype).reshape(bsz, uc.shape[1], SSM_D_INNER)
    y = rmsnorm(y * jax.nn.silu(z), norm_g) @ w_out
    yc = rmsnorm(yc * jax.nn.silu(zc), norm_g) @ w_out
    return y, yc


def gqa_attention(u, uc, w_qkv, q_norm, k_norm, w_o, cos, sin, with_ctx_out):
    bsz, n_tok, _ = u.shape
    n_ctx = uc.shape[1]
    grp = GQA_HEADS // GQA_KV_HEADS
    nq = GQA_HEADS * GQA_HEAD_DIM
    nkv = GQA_KV_HEADS * GQA_HEAD_DIM

    def project(h):
        length = h.shape[1]
        p = h @ w_qkv
        q = rmsnorm(p[..., :nq].reshape(bsz, length, GQA_KV_HEADS, grp, GQA_HEAD_DIM), q_norm)
        k = rmsnorm(p[..., nq:nq + nkv].reshape(bsz, length, GQA_KV_HEADS, GQA_HEAD_DIM), k_norm)
        v = p[..., nq + nkv:].reshape(bsz, length, GQA_KV_HEADS, GQA_HEAD_DIM)
        return q, k, v

    q, k, v = project(u)
    qc, kc, vc = project(uc)
    q = apply_rope(q, cos, sin)
    k = apply_rope(k, cos, sin)
    k_all = jnp.concatenate([k, kc], axis=1)
    v_all = jnp.concatenate([v, vc], axis=1)
    nb = n_tok // Q_BLOCK
    q_blocks = jnp.moveaxis(q.reshape(bsz, nb, Q_BLOCK, GQA_KV_HEADS, grp, GQA_HEAD_DIM), 1, 0)
    o = lax.map(lambda qb: attend(qb, k_all, v_all), q_blocks)
    y = jnp.moveaxis(o, 0, 1).reshape(bsz, n_tok, nq) @ w_o
    yc = None
    if with_ctx_out:
        yc = attend(qc, kc, vc).reshape(bsz, n_ctx, nq) @ w_o
    return y, yc


def setup_inputs(seed: int = 0) -> dict:
    key = jax.random.key(seed)
    ks = jax.random.split(key, 25)
    f32 = jnp.float32
    D = D_MODEL

    def nrm(k, shape, scale):
        return jax.random.normal(k, shape, f32) * scale

    x = nrm(ks[0], (BATCH, SEQ, D), 1.0)
    c = nrm(ks[1], (BATCH, D), 1.0)
    ctx = nrm(ks[2], (BATCH, CTX_LEN, D), 1.0)
    c_ctx = nrm(ks[3], (D,), 1.0)
    ada_w = nrm(ks[4], (DEPTH, D, N_MOD * D), 0.5 * D ** -0.5)
    ada_b = nrm(ks[5], (DEPTH, N_MOD * D), 0.02)
    norm_g = 1.0 + nrm(ks[6], (DEPTH, 3, D), 0.02)
    ffn_w_in = nrm(ks[7], (DEPTH, 2, D, 2 * D_FF), D ** -0.5)
    ffn_w_out = nrm(ks[8], (DEPTH, 2, D_FF, D), D_FF ** -0.5)
    na_w_qkv = nrm(ks[9], (N_A, D, 3 * NA_HEADS * NA_HEAD_DIM), D ** -0.5)
    na_rpb = nrm(ks[10], (N_A, NA_HEADS, 2 * NA_WIN_ROWS - 1, 2 * NA_WIN_COLS - 1), 0.1)
    na_w_o = nrm(ks[11], (N_A, NA_HEADS * NA_HEAD_DIM, D), (NA_HEADS * NA_HEAD_DIM) ** -0.5)
    ssm_w_in = nrm(ks[12], (N_B, D, SSM_IN_DIM), D ** -0.5)
    ssm_conv_w = nrm(ks[13], (N_B, SSM_CONV_W, SSM_CONV_DIM), SSM_CONV_W ** -0.5)
    ssm_conv_b = nrm(ks[14], (N_B, SSM_CONV_DIM), 0.02)
    ssm_a_log = jnp.log(jax.random.uniform(ks[15], (N_B, 2, SSM_HEADS), f32, 1.0, 16.0))
    dt0 = jnp.exp(jax.random.uniform(ks[16], (N_B, 2, SSM_HEADS), f32, math.log(1e-3), math.log(1e-1)))
    ssm_dt_bias = dt0 + jnp.log(-jnp.expm1(-dt0))
    ssm_d = 1.0 + nrm(ks[17], (N_B, 2, SSM_HEADS), 0.1)
    ssm_norm_g = 1.0 + nrm(ks[18], (N_B, SSM_D_INNER), 0.02)
    ssm_w_out = nrm(ks[19], (N_B, SSM_D_INNER, D), SSM_D_INNER ** -0.5)
    gqa_w_qkv = nrm(ks[20], (N_C, D, (GQA_HEADS + 2 * GQA_KV_HEADS) * GQA_HEAD_DIM), D ** -0.5)
    gqa_q_norm = 1.0 + nrm(ks[21], (N_C, GQA_HEAD_DIM), 0.02)
    gqa_k_norm = 1.0 + nrm(ks[22], (N_C, GQA_HEAD_DIM), 0.02)
    gqa_w_o = nrm(ks[23], (N_C, GQA_HEADS * GQA_HEAD_DIM, D), (GQA_HEADS * GQA_HEAD_DIM) ** -0.5)
    final_norm_g = 1.0 + nrm(ks[24], (D,), 0.02)
    return {"x": x, "c": c, "ctx": ctx, "c_ctx": c_ctx, "ada_w": ada_w, "ada_b": ada_b,
            "norm_g": norm_g, "ffn_w_in": ffn_w_in, "ffn_w_out": ffn_w_out,
            "na_w_qkv": na_w_qkv, "na_rpb": na_rpb, "na_w_o": na_w_o,
            "ssm_w_in": ssm_w_in, "ssm_conv_w": ssm_conv_w, "ssm_conv_b": ssm_conv_b,
            "ssm_a_log": ssm_a_log, "ssm_dt_bias": ssm_dt_bias, "ssm_d": ssm_d,
            "ssm_norm_g": ssm_norm_g, "ssm_w_out": ssm_w_out,
            "gqa_w_qkv": gqa_w_qkv, "gqa_q_norm": gqa_q_norm, "gqa_k_norm": gqa_k_norm, "gqa_w_o": gqa_w_o,
            "final_norm_g": final_norm_g}


def reference(x, c, ctx, c_ctx, ada_w, ada_b, norm_g, ffn_w_in, ffn_w_out, na_w_qkv, na_rpb, na_w_o,
              ssm_w_in, ssm_conv_w, ssm_conv_b, ssm_a_log, ssm_dt_bias, ssm_d, ssm_norm_g, ssm_w_out,
              gqa_w_qkv, gqa_q_norm, gqa_k_norm, gqa_w_o, final_norm_g):
    bsz, n_tok, d = x.shape
    cos, sin = axial_rope(n_tok, GQA_HEAD_DIM)
    s_lat = jax.nn.silu(c)
    s_ctx = jax.nn.silu(c_ctx)[None]
    h, hc = x, ctx
    for i in range(DEPTH):
        last = i == DEPTH - 1
        mod = (s_lat @ ada_w[i] + ada_b[i]).reshape(bsz, N_MOD, 1, d)
        mod_c = (s_ctx @ ada_w[i] + ada_b[i]).reshape(1, N_MOD, 1, d)
        h = h + HALF_STEP * mod[:, 2] * swiglu(modulate(rmsnorm(h, norm_g[i, 0]), mod[:, 0], mod[:, 1]),
                                               ffn_w_in[i, 0], ffn_w_out[i, 0])
        hc = hc + HALF_STEP * mod_c[:, 2] * swiglu(modulate(rmsnorm(hc, norm_g[i, 0]), mod_c[:, 0], mod_c[:, 1]),
                                                   ffn_w_in[i, 0], ffn_w_out[i, 0])
        u = modulate(rmsnorm(h, norm_g[i, 1]), mod[:, 3], mod[:, 4])
        uc = modulate(rmsnorm(hc, norm_g[i, 1]), mod_c[:, 3], mod_c[:, 4])
        kind, j = i % N_MIXERS, i // N_MIXERS
        if kind == 0:
            y, yc = neighbourhood_attention(u, uc, na_w_qkv[j], na_rpb[j], na_w_o[j], not last)
        elif kind == 1:
            y, yc = mamba2_bidirectional(u, uc, ssm_w_in[j], ssm_conv_w[j], ssm_conv_b[j], ssm_a_log[j],
                                         ssm_dt_bias[j], ssm_d[j], ssm_norm_g[j], ssm_w_out[j])
        else:
            y, yc = gqa_attention(u, uc, gqa_w_qkv[j], gqa_q_norm[j], gqa_k_norm[j], gqa_w_o[j], cos, sin, not last)
        h = h + mod[:, 5] * y
        h = h + HALF_STEP * mod[:, 8] * swiglu(modulate(rmsnorm(h, norm_g[i, 2]), mod[:, 6], mod[:, 7]),
                                               ffn_w_in[i, 1], ffn_w_out[i, 1])
        if not last:
            hc = hc + mod_c[:, 5] * yc
            hc = hc + HALF_STEP * mod_c[:, 8] * swiglu(modulate(rmsnorm(hc, norm_g[i, 2]), mod_c[:, 6], mod_c[:, 7]),
                                                       ffn_w_in[i, 1], ffn_w_out[i, 1])
    return rmsnorm(h, final_norm_g)
```

```python
import functools

import jax
import jax.numpy as jnp
import numpy as np
from jax import lax
from jax.experimental import pallas as pl
from jax.experimental.pallas import tpu as pltpu

F32 = jnp.float32
BF16 = jnp.bfloat16

EPS = 1e-6
HALF_STEP = 0.5
N_MOD = 9
GRID_W = 64
LANES = 128

NA_HEADS = 16
NA_WIN_ROWS = 8
NA_WIN_COLS = 16
NA_Q_ROWS = 4
NA_BAND_ROWS = 12

SSM_HEADS = 64
SSM_HEAD_DIM = 64
SSM_GROUPS = 8
SSM_STATE = 128
SSM_CHUNK = 128
SSM_D_INNER = SSM_HEADS * SSM_HEAD_DIM
SSM_GN = SSM_GROUPS * SSM_STATE
SSM_CONV_W = 5
SSM_CONV_DIM = SSM_D_INNER + 2 * SSM_GN
SSM_HEADS_PER_GROUP = SSM_HEADS // SSM_GROUPS
CONV_HALO = 16

GQA_HEADS = 16
GQA_KV_HEADS = 4
GQA_GROUP = GQA_HEADS // GQA_KV_HEADS
ROPE_THETA = 10000.0

MASK_VALUE = -1e30
VMEM_LIMIT = 56 * 1024 * 1024


def _params(*sem):
    return pltpu.CompilerParams(dimension_semantics=sem, vmem_limit_bytes=VMEM_LIMIT)


def _silu(x):
    return x / (1.0 + jnp.exp(-x))


def _dot(a, b):
    return jnp.dot(a, b, preferred_element_type=F32)


def _dot_nt(a, b):
    return lax.dot_general(a, b, (((1,), (1,)), ((), ())), preferred_element_type=F32)


def _dot_tn(a, b):
    return lax.dot_general(a, b, (((0,), (0,)), ((), ())), preferred_element_type=F32)


def _norm_modulate(x, g, shift, scale):
    var = jnp.mean(x * x, axis=-1, keepdims=True)
    return (x * lax.rsqrt(var + EPS)) * g * (1.0 + scale) + shift


def _adaln_kernel(ct_ref, w_ref, b_ref, o_ref, s_sc, *, rows):
    c = ct_ref[...]
    s_sc[...] = _silu(c)
    d, tn = w_ref.shape[1], w_ref.shape[2]

    def body(i, carry):
        a0, a1 = carry
        r = pl.multiple_of(i * rows, rows)
        w = w_ref[0, pl.ds(r, rows), :]
        s = s_sc[pl.ds(r, rows), :]
        return a0 + w * s[:, 0:1], a1 + w * s[:, 1:2]

    zero = jnp.zeros((rows, tn), F32)
    a0, a1 = lax.fori_loop(0, d // rows, body, (zero, zero))
    bias = b_ref[0]
    o_ref[0, 0:1, :] = jnp.sum(a0, axis=0, keepdims=True) + bias
    o_ref[0, 1:2, :] = jnp.sum(a1, axis=0, keepdims=True) + bias


def adaln_mod(c, c_ctx, ada_w, ada_b):
    depth, d, n = ada_w.shape
    tn = 512
    ct = jnp.stack([c[0], c_ctx], axis=1)
    out = pl.pallas_call(
        functools.partial(_adaln_kernel, rows=32),
        grid=(depth, n // tn),
        in_specs=[
            pl.BlockSpec((d, 2), lambda l, j: (0, 0)),
            pl.BlockSpec((1, d, tn), lambda l, j: (l, 0, j)),
            pl.BlockSpec((1, 1, tn), lambda l, j: (l, 0, j)),
        ],
        out_specs=pl.BlockSpec((1, 2, tn), lambda l, j: (l, 0, j)),
        out_shape=jax.ShapeDtypeStruct((depth, 2, n), F32),
        scratch_shapes=[pltpu.VMEM((d, 2), F32)],
        compiler_params=_params("parallel", "parallel"),
        name="adaln_mod",
    )(ct, ada_w, ada_b.reshape(depth, 1, n))
    return out.reshape(depth, 2, N_MOD, d)


def _ffn_kernel(h_ref, g_ref, sh_ref, sc_ref, gate_ref, wa_ref, wb_ref, wo_ref, fg_ref, o_ref,
                xn_sc, acc_sc, *, final_norm):
    j = pl.program_id(1)

    @pl.when(j == 0)
    def _():
        xn = _norm_modulate(h_ref[...], g_ref[...], sh_ref[...], sc_ref[...])
        xn_sc[...] = xn.astype(BF16)
        acc_sc[...] = jnp.zeros_like(acc_sc)

    xn = xn_sc[...]
    a = _dot(xn, wa_ref[...])
    b = _dot(xn, wb_ref[...])
    hm = (_silu(a) * b).astype(BF16)
    acc_sc[...] += _dot(hm, wo_ref[...])

    @pl.when(j == pl.num_programs(1) - 1)
    def _():
        out = h_ref[...] + (HALF_STEP * gate_ref[...]) * acc_sc[...]
        if final_norm:
            var = jnp.mean(out * out, axis=-1, keepdims=True)
            out = (out * lax.rsqrt(var + EPS)) * fg_ref[...]
        o_ref[...] = out


def ffn_half_step(h, g, shift, scale, gate, w_in, w_out, final_g=None):
    m, d = h.shape
    f = w_out.shape[0]
    tm = min(m, 512)
    tf = 512
    nf = f // tf
    row = lambda i, j: (0, 0)
    fg = final_g if final_g is not None else g
    return pl.pallas_call(
        functools.partial(_ffn_kernel, final_norm=final_g is not None),
        grid=(m // tm, nf),
        in_specs=[
            pl.BlockSpec((tm, d), lambda i, j: (i, 0)),
            pl.BlockSpec((1, d), row), pl.BlockSpec((1, d), row),
            pl.BlockSpec((1, d), row), pl.BlockSpec((1, d), row),
            pl.BlockSpec((d, tf), lambda i, j: (0, j)),
            pl.BlockSpec((d, tf), lambda i, j: (0, j + nf)),
            pl.BlockSpec((tf, d), lambda i, j: (j, 0)),
            pl.BlockSpec((1, d), row),
        ],
        out_specs=pl.BlockSpec((tm, d), lambda i, j: (i, 0)),
        out_shape=jax.ShapeDtypeStruct((m, d), F32),
        scratch_shapes=[pltpu.VMEM((tm, d), BF16), pltpu.VMEM((tm, d), F32)],
        compiler_params=_params("parallel", "arbitrary"),
        name="ffn_half_step",
    )(h, g, shift, scale, gate, w_in, w_in, w_out, fg)


def _nmm_kernel(h_ref, g_ref, sh_ref, sc_ref, w_ref, o_ref, xn_sc, *, precise):
    @pl.when(pl.program_id(1) == 0)
    def _():
        xn = _norm_modulate(h_ref[...], g_ref[...], sh_ref[...], sc_ref[...])
        xn_sc[...] = xn.astype(xn_sc.dtype)

    if precise:
        y = jnp.dot(xn_sc[...], w_ref[...], preferred_element_type=F32, precision=lax.Precision.HIGHEST)
    else:
        y = _dot(xn_sc[...], w_ref[...])
    o_ref[...] = y.astype(o_ref.dtype)


def norm_mod_matmul(h, g, shift, scale, w, out_dtype, precise=False):
    m, d = h.shape
    n = w.shape[1]
    tm = min(m, 512)
    tn = n if n <= 1024 else (1024 if n % 1024 == 0 else 512)
    row = lambda i, j: (0, 0)
    return pl.pallas_call(
        functools.partial(_nmm_kernel, precise=precise),
        grid=(m // tm, n // tn),
        in_specs=[
            pl.BlockSpec((tm, d), lambda i, j: (i, 0)),
            pl.BlockSpec((1, d), row), pl.BlockSpec((1, d), row), pl.BlockSpec((1, d), row),
            pl.BlockSpec((d, tn), lambda i, j: (0, j)),
        ],
        out_specs=pl.BlockSpec((tm, tn), lambda i, j: (i, j)),
        out_shape=jax.ShapeDtypeStruct((m, n), out_dtype),
        scratch_shapes=[pltpu.VMEM((tm, d), F32 if precise else BF16)],
        compiler_params=_params("parallel", "arbitrary"),
        name="norm_mod_matmul",
    )(h, g, shift, scale, w)


def _mgr_kernel(a_ref, w_ref, h_ref, gate_ref, o_ref):
    o_ref[...] = h_ref[...] + gate_ref[...] * _dot(a_ref[...], w_ref[...])


def matmul_gated_residual(a, w, h, gate):
    m, k = a.shape
    n = w.shape[1]
    tm = min(m, 512)
    tn = min(n, 1024)
    return pl.pallas_call(
        _mgr_kernel,
        grid=(m // tm, n // tn),
        in_specs=[
            pl.BlockSpec((tm, k), lambda i, j: (i, 0)),
            pl.BlockSpec((k, tn), lambda i, j: (0, j)),
            pl.BlockSpec((tm, tn), lambda i, j: (i, j)),
            pl.BlockSpec((1, tn), lambda i, j: (0, j)),
        ],
        out_specs=pl.BlockSpec((tm, tn), lambda i, j: (i, j)),
        out_shape=jax.ShapeDtypeStruct((m, n), F32),
        compiler_params=_params("parallel", "parallel"),
        name="matmul_gated_residual",
    )(a, w, h, gate)


def _ctx_attn_kernel(q_ref, k_ref, v_ref, o_ref):
    q = q_ref[...]
    s = _dot_nt(q, k_ref[...]) * (q.shape[-1] ** -0.5)
    m = jnp.max(s, axis=-1, keepdims=True)
    p = jnp.exp(s - m)
    l = jnp.sum(p, axis=-1, keepdims=True)
    o_ref[...] = (_dot(p.astype(BF16), v_ref[...]) / l).astype(o_ref.dtype)


def ctx_attention(qkv, n_heads, q_blk, k_blk, v_blk):
    n_ctx = qkv.shape[0]
    spec = lambda f: pl.BlockSpec((n_ctx, LANES), lambda h: (0, f(h)))
    return pl.pallas_call(
        _ctx_attn_kernel,
        grid=(n_heads,),
        in_specs=[spec(q_blk), spec(k_blk), spec(v_blk)],
        out_specs=pl.BlockSpec((n_ctx, LANES), lambda h: (0, h)),
        out_shape=jax.ShapeDtypeStruct((n_ctx, n_heads * LANES), BF16),
        compiler_params=_params("parallel"),
        name="ctx_attention",
    )(qkv, qkv, qkv)


def _na_kernel(q_ref, k_ref, v_ref, kc_ref, vc_ref, bias_ref, o_ref, *, n_rows):
    r0 = pl.program_id(1) * NA_Q_ROWS
    b0 = jnp.clip(r0 - NA_WIN_ROWS // 2, 0, n_rows - NA_BAND_ROWS)
    start = pl.multiple_of(b0 * GRID_W, GRID_W)
    n_band = NA_BAND_ROWS * GRID_W
    q = q_ref[...]
    scale = q.shape[-1] ** -0.5
    s_w = _dot_nt(q, k_ref[pl.ds(start, n_band), :]) * scale + bias_ref[0, 0]
    s_c = _dot_nt(q, kc_ref[...]) * scale
    m = jnp.maximum(jnp.max(s_w, axis=-1, keepdims=True), jnp.max(s_c, axis=-1, keepdims=True))
    p_w = jnp.exp(s_w - m)
    p_c = jnp.exp(s_c - m)
    l = jnp.sum(p_w, axis=-1, keepdims=True) + jnp.sum(p_c, axis=-1, keepdims=True)
    o = _dot(p_w.astype(BF16), v_ref[pl.ds(start, n_band), :]) + _dot(p_c.astype(BF16), vc_ref[...])
    o_ref[...] = (o / l).astype(o_ref.dtype)


def _na_block_geometry(n_rows):
    nq, nk = NA_Q_ROWS * GRID_W, NA_BAND_ROWS * GRID_W
    valid, drow = [], []
    for r0 in (0, NA_Q_ROWS, n_rows - NA_Q_ROWS):
        b0 = int(np.clip(r0 - NA_WIN_ROWS // 2, 0, n_rows - NA_BAND_ROWS))
        qr = r0 + np.arange(nq) // GRID_W
        qc = np.arange(nq) % GRID_W
        kr = b0 + np.arange(nk) // GRID_W
        kc = np.arange(nk) % GRID_W
        rs = np.clip(qr - NA_WIN_ROWS // 2, 0, n_rows - NA_WIN_ROWS)
        cs = np.clip(qc - NA_WIN_COLS // 2, 0, GRID_W - NA_WIN_COLS)
        ok = ((kr[None] >= rs[:, None]) & (kr[None] < rs[:, None] + NA_WIN_ROWS)
              & (kc[None] >= cs[:, None]) & (kc[None] < cs[:, None] + NA_WIN_COLS))
        valid.append(ok)
        drow.append((b0 + np.arange(NA_BAND_ROWS))[None, :] - (r0 + np.arange(NA_Q_ROWS))[:, None])
    return np.stack(valid), np.stack(drow)


def _na_bias_table(rpb, n_rows):
    n_heads = rpb.shape[0]
    valid, drow = _na_block_geometry(n_rows)
    dcol = np.arange(GRID_W)[None, :] - np.arange(GRID_W)[:, None] + NA_WIN_COLS - 1
    in_table = (dcol >= 0) & (dcol < 2 * NA_WIN_COLS - 1)
    onehot = (np.clip(dcol, 0, 2 * NA_WIN_COLS - 2)[None] == np.arange(2 * NA_WIN_COLS - 1)[:, None, None]) & in_table
    toeplitz = jnp.einsum("hrd,dqk->hrqk", rpb, jnp.asarray(onehot, F32), precision=lax.Precision.HIGHEST)
    zero_tile = jnp.zeros((n_heads, GRID_W, GRID_W), F32)
    variants = []
    for v in range(3):
        rows = []
        for qi in range(NA_Q_ROWS):
            tiles = []
            for j in range(NA_BAND_ROWS):
                dr = int(drow[v, qi, j]) + NA_WIN_ROWS - 1
                tiles.append(toeplitz[:, dr] if 0 <= dr < 2 * NA_WIN_ROWS - 1 else zero_tile)
            rows.append(jnp.concatenate(tiles, axis=-1))
        variants.append(jnp.concatenate(rows, axis=-2))
    table = jnp.stack(variants)
    return jnp.where(jnp.asarray(valid)[:, None], table, MASK_VALUE)


def neighbourhood_attention(qkv, qkv_c, rpb):
    n_tok = qkv.shape[0]
    n_ctx = qkv_c.shape[0]
    n_rows = n_tok // GRID_W
    n_blocks = n_rows // NA_Q_ROWS
    nq, nk = NA_Q_ROWS * GRID_W, NA_BAND_ROWS * GRID_W
    bias = _na_bias_table(rpb, n_rows)
    h_ = NA_HEADS

    def variant(rb):
        return jnp.where(rb == 0, 0, jnp.where(rb == n_blocks - 1, 2, 1))

    return pl.pallas_call(
        functools.partial(_na_kernel, n_rows=n_rows),
        grid=(h_, n_blocks),
        in_specs=[
            pl.BlockSpec((nq, LANES), lambda h, rb: (rb, h)),
            pl.BlockSpec((n_tok, LANES), lambda h, rb: (0, h_ + h)),
            pl.BlockSpec((n_tok, LANES), lambda h, rb: (0, 2 * h_ + h)),
            pl.BlockSpec((n_ctx, LANES), lambda h, rb: (0, h_ + h)),
            pl.BlockSpec((n_ctx, LANES), lambda h, rb: (0, 2 * h_ + h)),
            pl.BlockSpec((1, 1, nq, nk), lambda h, rb: (variant(rb), h, 0, 0)),
        ],
        out_specs=pl.BlockSpec((nq, LANES), lambda h, rb: (rb, h)),
        out_shape=jax.ShapeDtypeStruct((n_tok, h_ * LANES), BF16),
        compiler_params=_params("parallel", "arbitrary"),
        name="neighbourhood_attention",
    )(qkv, qkv, qkv, qkv_c, qkv_c, bias)


def _qk_norm_rope_kernel(p_ref, w_ref, cos_ref, sin_ref, o_ref, *, rope, n_qk):
    x = p_ref[...].astype(F32)
    cb = pl.program_id(1)

    @pl.when(cb < n_qk)
    def _():
        var = jnp.mean(x * x, axis=-1, keepdims=True)
        y = (x * lax.rsqrt(var + EPS)) * w_ref[0]
        if rope:
            y = y * cos_ref[...] + pltpu.roll(y, LANES // 2, 1) * sin_ref[...]
        o_ref[...] = y.astype(o_ref.dtype)

    @pl.when(cb >= n_qk)
    def _():
        o_ref[...] = x.astype(o_ref.dtype)


def qk_norm_rope(p, q_norm, k_norm, cos_full, sin_signed, rope):
    n, c = p.shape
    tl = min(n, 512)
    n_qk = GQA_HEADS + GQA_KV_HEADS
    n_blk = c // LANES
    w = jnp.concatenate([jnp.tile(q_norm[None], (GQA_HEADS, 1)), jnp.tile(k_norm[None], (GQA_KV_HEADS, 1)),
                         jnp.ones((n_blk - n_qk, LANES), F32)]).reshape(n_blk, 1, LANES)
    return pl.pallas_call(
        functools.partial(_qk_norm_rope_kernel, rope=rope, n_qk=n_qk),
        grid=(n // tl, n_blk),
        in_specs=[
            pl.BlockSpec((tl, LANES), lambda i, j: (i, j)),
            pl.BlockSpec((1, 1, LANES), lambda i, j: (j, 0, 0)),
            pl.BlockSpec((tl, LANES), lambda i, j: (i, 0)),
            pl.BlockSpec((tl, LANES), lambda i, j: (i, 0)),
        ],
        out_specs=pl.BlockSpec((tl, LANES), lambda i, j: (i, j)),
        out_shape=jax.ShapeDtypeStruct((n, c), BF16),
        compiler_params=_params("parallel", "parallel"),
        name="qk_norm_rope",
    )(p, w, cos_full, sin_signed)


def _gqa_kernel(q_ref, k_ref, v_ref, kc_ref, vc_ref, o_ref, m_sc, l_sc, acc_sc, *, tk):
    tq = q_ref.shape[0]
    n_tok = k_ref.shape[0]
    q = jnp.concatenate([q_ref[:, g * LANES:(g + 1) * LANES] for g in range(GQA_GROUP)], axis=0)
    scale = LANES ** -0.5
    m_sc[...] = jnp.full_like(m_sc, -jnp.inf)
    l_sc[...] = jnp.zeros_like(l_sc)
    acc_sc[...] = jnp.zeros_like(acc_sc)

    def update(kb, vb):
        s = _dot_nt(q, kb) * scale
        m_old = m_sc[...]
        m_new = jnp.maximum(m_old, jnp.max(s, axis=-1, keepdims=True))
        alpha = jnp.exp(m_old - m_new)
        p = jnp.exp(s - m_new)
        l_sc[...] = alpha * l_sc[...] + jnp.sum(p, axis=-1, keepdims=True)
        acc_sc[...] = alpha * acc_sc[...] + _dot(p.astype(BF16), vb)
        m_sc[...] = m_new

    def body(j, carry):
        r = pl.multiple_of(j * tk, tk)
        update(k_ref[pl.ds(r, tk), :], v_ref[pl.ds(r, tk), :])
        return carry

    lax.fori_loop(0, n_tok // tk, body, 0)
    update(kc_ref[...], vc_ref[...])
    o = acc_sc[...] / l_sc[...]
    for g in range(GQA_GROUP):
        o_ref[:, g * LANES:(g + 1) * LANES] = o[g * tq:(g + 1) * tq].astype(o_ref.dtype)


def gqa_attention(qkv, qkv_c):
    n_tok = qkv.shape[0]
    n_ctx = qkv_c.shape[0]
    tq = min(n_tok, 256)
    tk = min(n_tok, 512)
    kv0 = GQA_HEADS
    v0 = GQA_HEADS + GQA_KV_HEADS
    rows = GQA_GROUP * tq
    return pl.pallas_call(
        functools.partial(_gqa_kernel, tk=tk),
        grid=(GQA_KV_HEADS, n_tok // tq),
        in_specs=[
            pl.BlockSpec((tq, GQA_GROUP * LANES), lambda kv, i: (i, kv)),
            pl.BlockSpec((n_tok, LANES), lambda kv, i: (0, kv0 + kv)),
            pl.BlockSpec((n_tok, LANES), lambda kv, i: (0, v0 + kv)),
            pl.BlockSpec((n_ctx, LANES), lambda kv, i: (0, kv0 + kv)),
            pl.BlockSpec((n_ctx, LANES), lambda kv, i: (0, v0 + kv)),
        ],
        out_specs=pl.BlockSpec((tq, GQA_GROUP * LANES), lambda kv, i: (i, kv)),
        out_shape=jax.ShapeDtypeStruct((n_tok, GQA_HEADS * LANES), BF16),
        scratch_shapes=[pltpu.VMEM((rows, 1), F32), pltpu.VMEM((rows, 1), F32), pltpu.VMEM((rows, LANES), F32)],
        compiler_params=_params("parallel", "arbitrary"),
        name="gqa_attention",
    )(qkv, qkv, qkv, qkv_c, qkv_c)


def axial_rope_tables(n_tok):
    t = jnp.arange(n_tok, dtype=jnp.int32)
    row = (t // GRID_W).astype(F32)
    col = (t % GRID_W).astype(F32)
    n_freq = LANES // 4
    inv_freq = ROPE_THETA ** (-jnp.arange(n_freq, dtype=F32) / n_freq)
    ang = jnp.concatenate([row[:, None] * inv_freq, col[:, None] * inv_freq], axis=-1)
    cos, sin = jnp.cos(ang), jnp.sin(ang)
    return jnp.concatenate([cos, cos], axis=-1), jnp.concatenate([-sin, sin], axis=-1)


def _conv_silu_kernel(prev_ref, cur_ref, next_ref, w_ref, b_ref, o_ref):
    i = pl.program_id(0)
    tl = cur_ref.shape[0]
    prev = jnp.where(i == 0, 0.0, prev_ref[...].astype(F32))
    nxt = jnp.where(i == pl.num_programs(0) - 1, 0.0, next_ref[...].astype(F32))
    ext = jnp.concatenate([prev, cur_ref[...].astype(F32), nxt], axis=0)
    n_ext = ext.shape[0]
    pad = (SSM_CONV_W - 1) // 2
    acc = jnp.zeros((tl, ext.shape[1]), F32) + b_ref[...]
    for k in range(SSM_CONV_W):
        shifted = ext if k == pad else pltpu.roll(ext, (pad - k) % n_ext, 0)
        acc = acc + shifted[CONV_HALO:CONV_HALO + tl] * w_ref[k:k + 1, :]
    o_ref[...] = _silu(acc).astype(o_ref.dtype)


def conv_silu(zxbc, conv_w, conv_b):
    n = zxbc.shape[0]
    tl = min(n, 512)
    tc = 512
    c0 = SSM_D_INNER // tc
    hb = tl // CONV_HALO
    n_halo = n // CONV_HALO
    return pl.pallas_call(
        _conv_silu_kernel,
        grid=(n // tl, SSM_CONV_DIM // tc),
        in_specs=[
            pl.BlockSpec((CONV_HALO, tc), lambda i, j: (jnp.maximum(i * hb - 1, 0), c0 + j)),
            pl.BlockSpec((tl, tc), lambda i, j: (i, c0 + j)),
            pl.BlockSpec((CONV_HALO, tc), lambda i, j: (jnp.minimum((i + 1) * hb, n_halo - 1), c0 + j)),
            pl.BlockSpec((SSM_CONV_W, tc), lambda i, j: (0, j)),
            pl.BlockSpec((1, tc), lambda i, j: (0, j)),
        ],
        out_specs=pl.BlockSpec((tl, tc), lambda i, j: (i, j)),
        out_shape=jax.ShapeDtypeStruct((n, SSM_CONV_DIM), BF16),
        compiler_params=_params("parallel", "parallel"),
        name="conv_silu",
    )(zxbc, zxbc, zxbc, conv_w, conv_b)


def _ssd_prep_kernel(dt_ref, bias_ref, alog_ref, delta_ref, acs_ref):
    x = dt_ref[...] + bias_ref[...]
    delta = jnp.maximum(x, 0.0) + jnp.log1p(jnp.exp(-jnp.abs(x)))
    da = delta * (-jnp.exp(alog_ref[...]))
    q = x.shape[0]
    i = lax.broadcasted_iota(jnp.int32, (q, q), 0)
    j = lax.broadcasted_iota(jnp.int32, (q, q), 1)
    hi = lax.Precision.HIGHEST
    fwd = jnp.dot((i >= j).astype(F32), da, preferred_element_type=F32, precision=hi)
    bwd = jnp.dot((i <= j).astype(F32), da, preferred_element_type=F32, precision=hi)
    lane = lax.broadcasted_iota(jnp.int32, da.shape, 1)
    delta_ref[...] = delta
    acs_ref[...] = jnp.where(lane < SSM_HEADS, fwd, bwd)


def ssd_prep(dt, dt_bias, a_log):
    n = dt.shape[0]
    q = SSM_CHUNK
    blk = pl.BlockSpec((q, 2 * SSM_HEADS), lambda c: (c, 0))
    row = pl.BlockSpec((1, 2 * SSM_HEADS), lambda c: (0, 0))
    return pl.pallas_call(
        _ssd_prep_kernel,
        grid=(n // q,),
        in_specs=[blk, row, row],
        out_specs=[blk, blk],
        out_shape=[jax.ShapeDtypeStruct(dt.shape, F32)] * 2,
        compiler_params=_params("parallel"),
        name="ssd_prep",
    )(dt, dt_bias.reshape(1, -1), a_log.reshape(1, -1))


def _ssd_kernel(x_ref, b_ref, c_ref, dl_ref, ac_ref, ar_ref, init_ref, y_ref, fin_ref, state_sc):
    d = pl.program_id(0)
    c = pl.program_id(2)
    q = SSM_CHUNK

    @pl.when(c == 0)
    def _():
        state_sc[...] = init_ref[0]

    bm = b_ref[...]
    cm = c_ref[...]
    dl = dl_ref[0, 0]
    ac = ac_ref[0, 0]
    ar = ar_ref[0, 0]
    tot = jnp.where(d == 0, ac[q - 1:q, :], ac[0:1, :])
    e_in = jnp.exp(ac)
    e_out = jnp.exp(tot - ac)
    e_tot = jnp.exp(tot)
    cb = _dot_nt(cm, bm)
    i = lax.broadcasted_iota(jnp.int32, (q, q), 0)
    j = lax.broadcasted_iota(jnp.int32, (q, q), 1)
    causal = jnp.where(d == 0, i - j, j - i) >= 0
    lo = lax.broadcasted_iota(jnp.int32, (q, LANES), 1) < SSM_HEAD_DIM
    state = state_sc[...]
    y_off = _dot(cm, state.astype(BF16))

    def pair(v, p):
        return jnp.where(lo[:v.shape[0]], v[:, 2 * p:2 * p + 1], v[:, 2 * p + 1:2 * p + 2])

    for p in range(SSM_HEADS_PER_GROUP // 2):
        cols = slice(p * LANES, (p + 1) * LANES)
        xdt = x_ref[:, cols].astype(F32) * pair(dl, p)
        decay = []
        for hd in (2 * p, 2 * p + 1):
            seg = jnp.where(causal, ac[:, hd:hd + 1] - ar[hd:hd + 1, :], -jnp.inf)
            decay.append(cb * jnp.exp(seg))
        res = _dot(jnp.concatenate(decay, axis=0).astype(BF16), xdt.astype(BF16))
        y_diag = jnp.where(lo, res[:q], res[q:])
        y_ref[0, :, cols] = (y_diag + y_off[:, cols] * pair(e_in, p)).astype(y_ref.dtype)
        s_new = _dot_tn(bm, (xdt * pair(e_out, p)).astype(BF16))
        state_sc[:, cols] = state[:, cols] * pair(e_tot, p) + s_new

    @pl.when(c == pl.num_programs(2) - 1)
    def _():
        fin_ref[0] = state_sc[...]


def ssd_scan(xbc, delta, acs, init_state):
    n = xbc.shape[0]
    q = SSM_CHUNK
    nc = n // q
    hg = SSM_HEADS_PER_GROUP
    gw = hg * SSM_HEAD_DIM
    col = lambda t: t.reshape(n, 2, SSM_GROUPS, hg).transpose(1, 2, 0, 3)
    dl_col, ac_col = col(delta), col(acs)
    ac_row = ac_col.transpose(0, 1, 3, 2)
    b0 = SSM_D_INNER // SSM_STATE
    c0 = b0 + SSM_GROUPS
    chunk = lambda d, c: c + d * (nc - 1 - 2 * c)
    col_spec = pl.BlockSpec((1, 1, q, hg), lambda d, g, c: (d, g, chunk(d, c), 0))
    state_spec = pl.BlockSpec((1, SSM_STATE, gw), lambda d, g, c: (d, 0, g))
    return pl.pallas_call(
        _ssd_kernel,
        grid=(2, SSM_GROUPS, nc),
        in_specs=[
            pl.BlockSpec((q, gw), lambda d, g, c: (chunk(d, c), g)),
            pl.BlockSpec((q, SSM_STATE), lambda d, g, c: (chunk(d, c), b0 + g)),
            pl.BlockSpec((q, SSM_STATE), lambda d, g, c: (chunk(d, c), c0 + g)),
            col_spec, col_spec,
            pl.BlockSpec((1, 1, hg, q), lambda d, g, c: (d, g, 0, chunk(d, c))),
            state_spec,
        ],
        out_specs=[pl.BlockSpec((1, q, gw), lambda d, g, c: (d, chunk(d, c), g)), state_spec],
        out_shape=[jax.ShapeDtypeStruct((2, n, SSM_D_INNER), F32),
                   jax.ShapeDtypeStruct((2, SSM_STATE, SSM_D_INNER), F32)],
        scratch_shapes=[pltpu.VMEM((SSM_STATE, gw), F32)],
        compiler_params=_params("parallel", "parallel", "arbitrary"),
        name="ssd_scan",
    )(xbc, xbc, xbc, dl_col, ac_col, ac_row, init_state)


def _ssm_gate_norm_kernel(yf_ref, yb_ref, x_ref, z_ref, dsum_ref, g_ref, o_ref):
    y = yf_ref[0] + yb_ref[0] + dsum_ref[...] * x_ref[...].astype(F32)
    v = y * _silu(z_ref[...].astype(F32))
    var = jnp.mean(v * v, axis=-1, keepdims=True)
    o_ref[...] = ((v * lax.rsqrt(var + EPS)) * g_ref[...]).astype(o_ref.dtype)


def ssm_gate_norm(y, xbc, zxbc, dsum, norm_g):
    n = xbc.shape[0]
    tl = min(n, 256)
    di = SSM_D_INNER
    row = pl.BlockSpec((1, di), lambda i: (0, 0))
    return pl.pallas_call(
        _ssm_gate_norm_kernel,
        grid=(n // tl,),
        in_specs=[
            pl.BlockSpec((1, tl, di), lambda i: (0, i, 0)),
            pl.BlockSpec((1, tl, di), lambda i: (1, i, 0)),
            pl.BlockSpec((tl, di), lambda i: (i, 0)),
            pl.BlockSpec((tl, di), lambda i: (i, 0)),
            row, row,
        ],
        out_specs=pl.BlockSpec((tl, di), lambda i: (i, 0)),
        out_shape=jax.ShapeDtypeStruct((n, di), BF16),
        compiler_params=_params("parallel"),
        name="ssm_gate_norm",
    )(y, y, xbc, zxbc, dsum, norm_g)


def mamba2_bidirectional(h, hc, g, mod_l, mod_c, w_in, conv_w, conv_b, a_log, dt_bias, d_skip, norm_g, w_out):
    n_main = SSM_D_INNER + SSM_CONV_DIM
    w_main = w_in[:, :n_main].astype(BF16)
    w_dt = w_in[:, n_main:]
    w_out = w_out.astype(BF16)
    dsum = jnp.repeat(d_skip[0] + d_skip[1], SSM_HEAD_DIM)[None]
    state = jnp.zeros((2, SSM_STATE, SSM_D_INNER), F32)
    outs = []
    for t, mod in ((hc, mod_c), (h, mod_l)):
        shift, scale, gate = mod[3:4], mod[4:5], mod[5:6]
        zxbc = norm_mod_matmul(t, g, shift, scale, w_main, BF16)
        dt = norm_mod_matmul(t, g, shift, scale, w_dt, F32, precise=True)
        xbc = conv_silu(zxbc, conv_w, conv_b[None])
        delta, acs = ssd_prep(dt, dt_bias, a_log)
        y, state = ssd_scan(xbc, delta, acs, state)
        yn = ssm_gate_norm(y, xbc, zxbc, dsum, norm_g[None])
        outs.append(matmul_gated_residual(yn, w_out, t, gate))
    return outs[1], outs[0]


def kernel(x, c, ctx, c_ctx, ada_w, ada_b, norm_g, ffn_w_in, ffn_w_out, na_w_qkv, na_rpb, na_w_o, ssm_w_in, ssm_conv_w, ssm_conv_b, ssm_a_log, ssm_dt_bias, ssm_d, ssm_norm_g, ssm_w_out, gqa_w_qkv, gqa_q_norm, gqa_k_norm, gqa_w_o, final_norm_g):
    bsz, n_tok, d = x.shape
    assert bsz == 1, "the kernels take one sequence"
    depth = ada_w.shape[0]
    h, hc = x[0], ctx[0]
    mod = adaln_mod(c, c_ctx, ada_w, ada_b)
    cos_full, sin_signed = axial_rope_tables(n_tok)
    rope_off = jnp.zeros((ctx.shape[1], LANES), F32)

    def ffn(t, m, i, k, final_g=None):
        return ffn_half_step(t, norm_g[i, 2 * k][None], m[6 * k:6 * k + 1], m[6 * k + 1:6 * k + 2],
                             m[6 * k + 2:6 * k + 3], ffn_w_in[i, k].astype(BF16), ffn_w_out[i, k].astype(BF16),
                             final_g)

    for i in range(depth):
        last = i == depth - 1
        ml, mc = mod[i, 0], mod[i, 1]
        g_mix = norm_g[i, 1][None]
        h = ffn(h, ml, i, 0)
        hc = ffn(hc, mc, i, 0)
        kind, j = i % 3, i // 3
        if kind == 0:
            w_qkv = na_w_qkv[j].astype(BF16)
            w_o = na_w_o[j].astype(BF16)
            qkv = norm_mod_matmul(h, g_mix, ml[3:4], ml[4:5], w_qkv, BF16)
            qkv_c = norm_mod_matmul(hc, g_mix, mc[3:4], mc[4:5], w_qkv, BF16)
            o = neighbourhood_attention(qkv, qkv_c, na_rpb[j])
            h = matmul_gated_residual(o, w_o, h, ml[5:6])
            if not last:
                oc = ctx_attention(qkv_c, NA_HEADS, lambda hd: hd, lambda hd: NA_HEADS + hd,
                                   lambda hd: 2 * NA_HEADS + hd)
                hc = matmul_gated_residual(oc, w_o, hc, mc[5:6])
        elif kind == 1:
            h, hc = mamba2_bidirectional(h, hc, g_mix, ml, mc, ssm_w_in[j], ssm_conv_w[j], ssm_conv_b[j],
                                         ssm_a_log[j], ssm_dt_bias[j], ssm_d[j], ssm_norm_g[j], ssm_w_out[j])
        else:
            w_qkv = gqa_w_qkv[j].astype(BF16)
            w_o = gqa_w_o[j].astype(BF16)
            p = norm_mod_matmul(h, g_mix, ml[3:4], ml[4:5], w_qkv, F32)
            p_c = norm_mod_matmul(hc, g_mix, mc[3:4], mc[4:5], w_qkv, F32)
            qkv = qk_norm_rope(p, gqa_q_norm[j], gqa_k_norm[j], cos_full, sin_signed, True)
            qkv_c = qk_norm_rope(p_c, gqa_q_norm[j], gqa_k_norm[j], rope_off, rope_off, False)
            o = gqa_attention(qkv, qkv_c)
            h = matmul_gated_residual(o, w_o, h, ml[5:6])
            if not last:
                oc = ctx_attention(qkv_c, GQA_HEADS, lambda hd: hd, lambda hd: GQA_HEADS + hd // GQA_GROUP,
                                   lambda hd: GQA_HEADS + GQA_KV_HEADS + hd // GQA_GROUP)
                hc = matmul_gated_residual(oc, w_o, hc, mc[5:6])
        h = ffn(h, ml, i, 1, final_norm_g[None] if last else None)
        if not last:
            hc = ffn(hc, mc, i, 1)
    return h[None]
```

```python
import functools

import jax
import jax.numpy as jnp
import numpy as np
from jax import lax
from jax.experimental import pallas as pl
from jax.experimental.pallas import tpu as pltpu

F32 = jnp.float32
BF16 = jnp.bfloat16

EPS = 1e-6
HALF_STEP = 0.5
N_MOD = 9
GRID_W = 64
LANES = 128

NA_HEADS = 16
NA_WIN_ROWS = 8
NA_WIN_COLS = 16
NA_Q_ROWS = 4
NA_BAND_ROWS = 12

SSM_HEADS = 64
SSM_HEAD_DIM = 64
SSM_GROUPS = 8
SSM_STATE = 128
SSM_CHUNK = 128
SSM_D_INNER = SSM_HEADS * SSM_HEAD_DIM
SSM_GN = SSM_GROUPS * SSM_STATE
SSM_CONV_W = 5
SSM_CONV_DIM = SSM_D_INNER + 2 * SSM_GN
SSM_HEADS_PER_GROUP = SSM_HEADS // SSM_GROUPS
CONV_HALO = 16

GQA_HEADS = 16
GQA_KV_HEADS = 4
GQA_GROUP = GQA_HEADS // GQA_KV_HEADS
ROPE_THETA = 10000.0
LOG2_E = 1.4426950408889634

MASK_VALUE = -1e30
VMEM_LIMIT = 56 * 1024 * 1024


def _params(*sem):
    return pltpu.CompilerParams(dimension_semantics=sem, vmem_limit_bytes=VMEM_LIMIT)


def _silu(x):
    return x / (1.0 + jnp.exp(-x))


def _dot(a, b):
    return jnp.dot(a, b, preferred_element_type=F32)


def _dot_nt(a, b):
    return lax.dot_general(a, b, (((1,), (1,)), ((), ())), preferred_element_type=F32)


def _dot_tn(a, b):
    return lax.dot_general(a, b, (((0,), (0,)), ((), ())), preferred_element_type=F32)


def _norm_modulate(x, g, shift, scale):
    var = jnp.mean(x * x, axis=-1, keepdims=True)
    return (x * lax.rsqrt(var + EPS)) * g * (1.0 + scale) + shift


def _adaln_kernel(c_ref, w_ref, b_ref, o_ref, s_sc, *, rows, group):
    tk, tn = w_ref.shape
    s_sc[...] = _silu(c_ref[...])

    @pl.when(pl.program_id(2) == 0)
    def _():
        o_ref[...] = jnp.broadcast_to(b_ref[...], o_ref.shape)

    for lg in range(tn // group):
        cols = slice(lg * group, (lg + 1) * group)

        def body(i, carry):
            a0, a1 = carry
            r = pl.multiple_of(i * rows, rows)
            w = w_ref[pl.ds(r, rows), cols]
            s0 = jnp.concatenate([s_sc[0, pl.ds(r, rows), :]] * (group // LANES), axis=1)
            s1 = jnp.concatenate([s_sc[1, pl.ds(r, rows), :]] * (group // LANES), axis=1)
            return a0 + w * s0, a1 + w * s1

        zero = jnp.zeros((rows, group), F32)
        a0, a1 = lax.fori_loop(0, tk // rows, body, (zero, zero), unroll=2)
        o_ref[0:1, cols] += jnp.sum(a0, axis=0, keepdims=True)
        o_ref[1:2, cols] += jnp.sum(a1, axis=0, keepdims=True)


def adaln_mod(c, c_ctx, ada_w, ada_b):
    depth, d, n = ada_w.shape
    tk = 256
    tn = n // 4 if n % 2048 == 0 else 512
    cb = jnp.broadcast_to(jnp.stack([c[0], c_ctx])[:, :, None], (2, d, LANES))
    out = pl.pallas_call(
        functools.partial(_adaln_kernel, rows=16, group=512),
        grid=(depth, n // tn, d // tk),
        in_specs=[
            pl.BlockSpec((2, tk, LANES), lambda l, j, k: (0, k, 0)),
            pl.BlockSpec((None, tk, tn), lambda l, j, k: (l, k, j)),
            pl.BlockSpec((None, 1, tn), lambda l, j, k: (l, 0, j)),
        ],
        out_specs=pl.BlockSpec((None, 2, tn), lambda l, j, k: (l, 0, j)),
        out_shape=jax.ShapeDtypeStruct((depth, 2, n), F32),
        scratch_shapes=[pltpu.VMEM((2, tk, LANES), F32)],
        compiler_params=_params("parallel", "parallel", "arbitrary"),
        name="adaln_mod",
    )(cb, ada_w, ada_b.reshape(depth, 1, n))
    return out.reshape(depth, 2, N_MOD, d)


def _ffn_kernel(h_ref, g_ref, sh_ref, sc_ref, gate_ref, wa_ref, wb_ref, wo_ref, fg_ref, o_ref,
                xn_sc, acc_sc, *, final_norm):
    j = pl.program_id(1)

    @pl.when(j == 0)
    def _():
        xn = _norm_modulate(h_ref[...], g_ref[...], sh_ref[...], sc_ref[...])
        xn_sc[...] = xn.astype(BF16)
        acc_sc[...] = jnp.zeros_like(acc_sc)

    xn = xn_sc[...]
    a = _dot(xn, wa_ref[...])
    b = _dot(xn, wb_ref[...])
    hm = (_silu(a) * b).astype(BF16)
    acc_sc[...] += _dot(hm, wo_ref[...])

    @pl.when(j == pl.num_programs(1) - 1)
    def _():
        out = h_ref[...] + (HALF_STEP * gate_ref[...]) * acc_sc[...]
        if final_norm:
            var = jnp.mean(out * out, axis=-1, keepdims=True)
            out = (out * lax.rsqrt(var + EPS)) * fg_ref[...]
        o_ref[...] = out


def ffn_half_step(h, g, shift, scale, gate, w_in, w_out, layer, k, final_g=None):
    m, d = h.shape
    f = w_out.shape[2]
    tm = min(m, 512)
    tf = 512
    nf = f // tf
    row = lambda i, j: (0, 0)
    fg = final_g if final_g is not None else g
    return pl.pallas_call(
        functools.partial(_ffn_kernel, final_norm=final_g is not None),
        grid=(m // tm, nf),
        in_specs=[
            pl.BlockSpec((tm, d), lambda i, j: (i, 0)),
            pl.BlockSpec((1, d), row), pl.BlockSpec((1, d), row),
            pl.BlockSpec((1, d), row), pl.BlockSpec((1, d), row),
            pl.BlockSpec((None, None, d, tf), lambda i, j: (layer, k, 0, j)),
            pl.BlockSpec((None, None, d, tf), lambda i, j: (layer, k, 0, j + nf)),
            pl.BlockSpec((None, None, tf, d), lambda i, j: (layer, k, j, 0)),
            pl.BlockSpec((1, d), row),
        ],
        out_specs=pl.BlockSpec((tm, d), lambda i, j: (i, 0)),
        out_shape=jax.ShapeDtypeStruct((m, d), F32),
        scratch_shapes=[pltpu.VMEM((tm, d), BF16), pltpu.VMEM((tm, d), F32)],
        compiler_params=_params("parallel", "arbitrary"),
        name="ffn_half_step",
    )(h, g, shift, scale, gate, w_in, w_in, w_out, fg)


def _nmm_kernel(h_ref, g_ref, sh_ref, sc_ref, w_ref, o_ref, xn_sc, *, precise):
    @pl.when(pl.program_id(1) == 0)
    def _():
        xn = _norm_modulate(h_ref[...], g_ref[...], sh_ref[...], sc_ref[...])
        xn_sc[...] = xn.astype(xn_sc.dtype)

    if precise:
        y = jnp.dot(xn_sc[...], w_ref[...], preferred_element_type=F32, precision=lax.Precision.HIGHEST)
    else:
        y = _dot(xn_sc[...], w_ref[...])
    o_ref[...] = y.astype(o_ref.dtype)


def norm_mod_matmul(h, g, shift, scale, w, layer, out_dtype, n=None, precise=False):
    m, d = h.shape
    n = w.shape[2] if n is None else n
    tm = min(m, 512)
    tn = n if n <= 1024 else (1024 if n % 1024 == 0 else 512)
    row = lambda i, j: (0, 0)
    return pl.pallas_call(
        functools.partial(_nmm_kernel, precise=precise),
        grid=(m // tm, n // tn),
        in_specs=[
            pl.BlockSpec((tm, d), lambda i, j: (i, 0)),
            pl.BlockSpec((1, d), row), pl.BlockSpec((1, d), row), pl.BlockSpec((1, d), row),
            pl.BlockSpec((None, d, tn), lambda i, j: (layer, 0, j)),
        ],
        out_specs=pl.BlockSpec((tm, tn), lambda i, j: (i, j)),
        out_shape=jax.ShapeDtypeStruct((m, n), out_dtype),
        scratch_shapes=[pltpu.VMEM((tm, d), F32 if precise else BF16)],
        compiler_params=_params("parallel", "arbitrary"),
        name="norm_mod_matmul",
    )(h, g, shift, scale, w)


def _mgr_kernel(a_ref, w_ref, h_ref, gate_ref, o_ref):
    o_ref[...] = h_ref[...] + gate_ref[...] * _dot(a_ref[...], w_ref[...])


def matmul_gated_residual(a, w, layer, h, gate):
    m, k = a.shape
    n = w.shape[2]
    tm = min(m, 512)
    tn = min(n, 1024)
    return pl.pallas_call(
        _mgr_kernel,
        grid=(m // tm, n // tn),
        in_specs=[
            pl.BlockSpec((tm, k), lambda i, j: (i, 0)),
            pl.BlockSpec((None, k, tn), lambda i, j: (layer, 0, j)),
            pl.BlockSpec((tm, tn), lambda i, j: (i, j)),
            pl.BlockSpec((1, tn), lambda i, j: (0, j)),
        ],
        out_specs=pl.BlockSpec((tm, tn), lambda i, j: (i, j)),
        out_shape=jax.ShapeDtypeStruct((m, n), F32),
        compiler_params=_params("parallel", "parallel"),
        name="matmul_gated_residual",
    )(a, w, h, gate)


def _ctx_attn_kernel(q_ref, k_ref, v_ref, o_ref):
    q = q_ref[...]
    s = _dot_nt(q, k_ref[...]) * (q.shape[-1] ** -0.5)
    m = jnp.max(s, axis=-1, keepdims=True)
    p = jnp.exp(s - m)
    l = jnp.sum(p, axis=-1, keepdims=True)
    o_ref[...] = (_dot(p.astype(BF16), v_ref[...]) / l).astype(o_ref.dtype)


def ctx_attention(qkv, n_heads, q_blk, k_blk, v_blk):
    n_ctx = qkv.shape[0]
    spec = lambda f: pl.BlockSpec((n_ctx, LANES), lambda h: (0, f(h)))
    return pl.pallas_call(
        _ctx_attn_kernel,
        grid=(n_heads,),
        in_specs=[spec(q_blk), spec(k_blk), spec(v_blk)],
        out_specs=pl.BlockSpec((n_ctx, LANES), lambda h: (0, h)),
        out_shape=jax.ShapeDtypeStruct((n_ctx, n_heads * LANES), BF16),
        compiler_params=_params("parallel"),
        name="ctx_attention",
    )(qkv, qkv, qkv)


def _na_kernel(q_ref, k_ref, v_ref, kc_ref, vc_ref, bias_ref, o_ref, *, n_rows):
    r0 = pl.program_id(1) * NA_Q_ROWS
    b0 = jnp.clip(r0 - NA_WIN_ROWS // 2, 0, n_rows - NA_BAND_ROWS)
    start = pl.multiple_of(b0 * GRID_W, GRID_W)
    n_band = NA_BAND_ROWS * GRID_W
    q = q_ref[...]
    scale = q.shape[-1] ** -0.5
    s_w = _dot_nt(q, k_ref[pl.ds(start, n_band), :]) * scale + bias_ref[0, 0]
    s_c = _dot_nt(q, kc_ref[...]) * scale
    m = jnp.maximum(jnp.max(s_w, axis=-1, keepdims=True), jnp.max(s_c, axis=-1, keepdims=True))
    p_w = jnp.exp(s_w - m)
    p_c = jnp.exp(s_c - m)
    l = jnp.sum(p_w, axis=-1, keepdims=True) + jnp.sum(p_c, axis=-1, keepdims=True)
    o = _dot(p_w.astype(BF16), v_ref[pl.ds(start, n_band), :]) + _dot(p_c.astype(BF16), vc_ref[...])
    o_ref[...] = (o / l).astype(o_ref.dtype)


def _na_block_geometry(n_rows):
    nq, nk = NA_Q_ROWS * GRID_W, NA_BAND_ROWS * GRID_W
    valid, drow = [], []
    for r0 in (0, NA_Q_ROWS, n_rows - NA_Q_ROWS):
        b0 = int(np.clip(r0 - NA_WIN_ROWS // 2, 0, n_rows - NA_BAND_ROWS))
        qr = r0 + np.arange(nq) // GRID_W
        qc = np.arange(nq) % GRID_W
        kr = b0 + np.arange(nk) // GRID_W
        kc = np.arange(nk) % GRID_W
        rs = np.clip(qr - NA_WIN_ROWS // 2, 0, n_rows - NA_WIN_ROWS)
        cs = np.clip(qc - NA_WIN_COLS // 2, 0, GRID_W - NA_WIN_COLS)
        ok = ((kr[None] >= rs[:, None]) & (kr[None] < rs[:, None] + NA_WIN_ROWS)
              & (kc[None] >= cs[:, None]) & (kc[None] < cs[:, None] + NA_WIN_COLS))
        valid.append(ok)
        drow.append((b0 + np.arange(NA_BAND_ROWS))[None, :] - (r0 + np.arange(NA_Q_ROWS))[:, None])
    return np.stack(valid), np.stack(drow)


def _na_bias_table(rpb, n_rows):
    n_heads = rpb.shape[0]
    valid, drow = _na_block_geometry(n_rows)
    dcol = np.arange(GRID_W)[None, :] - np.arange(GRID_W)[:, None] + NA_WIN_COLS - 1
    in_table = (dcol >= 0) & (dcol < 2 * NA_WIN_COLS - 1)
    onehot = (np.clip(dcol, 0, 2 * NA_WIN_COLS - 2)[None] == np.arange(2 * NA_WIN_COLS - 1)[:, None, None]) & in_table
    toeplitz = jnp.einsum("hrd,dqk->hrqk", rpb, jnp.asarray(onehot, F32), precision=lax.Precision.HIGHEST)
    zero_tile = jnp.zeros((n_heads, GRID_W, GRID_W), F32)
    variants = []
    for v in range(3):
        rows = []
        for qi in range(NA_Q_ROWS):
            tiles = []
            for j in range(NA_BAND_ROWS):
                dr = int(drow[v, qi, j]) + NA_WIN_ROWS - 1
                tiles.append(toeplitz[:, dr] if 0 <= dr < 2 * NA_WIN_ROWS - 1 else zero_tile)
            rows.append(jnp.concatenate(tiles, axis=-1))
        variants.append(jnp.concatenate(rows, axis=-2))
    table = jnp.stack(variants)
    return jnp.where(jnp.asarray(valid)[:, None], table, MASK_VALUE)


def neighbourhood_attention(qkv, qkv_c, rpb):
    n_tok = qkv.shape[0]
    n_ctx = qkv_c.shape[0]
    n_rows = n_tok // GRID_W
    n_blocks = n_rows // NA_Q_ROWS
    nq, nk = NA_Q_ROWS * GRID_W, NA_BAND_ROWS * GRID_W
    bias = _na_bias_table(rpb, n_rows)
    h_ = NA_HEADS

    def variant(rb):
        return jnp.where(rb == 0, 0, jnp.where(rb == n_blocks - 1, 2, 1))

    return pl.pallas_call(
        functools.partial(_na_kernel, n_rows=n_rows),
        grid=(h_, n_blocks),
        in_specs=[
            pl.BlockSpec((nq, LANES), lambda h, rb: (rb, h)),
            pl.BlockSpec((n_tok, LANES), lambda h, rb: (0, h_ + h)),
            pl.BlockSpec((n_tok, LANES), lambda h, rb: (0, 2 * h_ + h)),
            pl.BlockSpec((n_ctx, LANES), lambda h, rb: (0, h_ + h)),
            pl.BlockSpec((n_ctx, LANES), lambda h, rb: (0, 2 * h_ + h)),
            pl.BlockSpec((1, 1, nq, nk), lambda h, rb: (variant(rb), h, 0, 0)),
        ],
        out_specs=pl.BlockSpec((nq, LANES), lambda h, rb: (rb, h)),
        out_shape=jax.ShapeDtypeStruct((n_tok, h_ * LANES), BF16),
        compiler_params=_params("parallel", "arbitrary"),
        name="neighbourhood_attention",
    )(qkv, qkv, qkv, qkv_c, qkv_c, bias)


def _gqa_proj_kernel(h_ref, g_ref, sh_ref, sc_ref, w_ref, qn_ref, kn_ref, cos_ref, sin_ref, o_ref, xn_sc,
                     *, rope, q_scale):
    j = pl.program_id(1)
    tn = o_ref.shape[1]
    n_q = GQA_HEADS * LANES // tn

    @pl.when(j == 0)
    def _():
        xn = _norm_modulate(h_ref[...], g_ref[...], sh_ref[...], sc_ref[...])
        xn_sc[...] = xn.astype(BF16)

    y = _dot(xn_sc[...], w_ref[...])

    def normed(w, mult):
        for s in range(tn // LANES):
            cols = slice(s * LANES, (s + 1) * LANES)
            x = y[:, cols]
            var = jnp.mean(x * x, axis=-1, keepdims=True)
            t = (x * lax.rsqrt(var + EPS)) * w
            if rope:
                t = t * cos_ref[...] + pltpu.roll(t, LANES // 2, 1) * sin_ref[...]
            if mult != 1.0:
                t = t * mult
            o_ref[:, cols] = t.astype(o_ref.dtype)

    @pl.when(j < n_q)
    def _():
        normed(qn_ref[...], q_scale)

    @pl.when(j == n_q)
    def _():
        normed(kn_ref[...], 1.0)

    @pl.when(j > n_q)
    def _():
        o_ref[...] = y.astype(o_ref.dtype)


def gqa_projection(h, g, shift, scale, w, layer, q_norm, k_norm, cos_full, sin_signed, rope, q_scale):
    m, d = h.shape
    n = w.shape[2]
    tm = min(m, 512)
    tn = GQA_KV_HEADS * LANES
    row = lambda i, j: (0, 0)
    head = pl.BlockSpec((1, LANES), row)
    pos = pl.BlockSpec((tm, LANES), lambda i, j: (i, 0))
    return pl.pallas_call(
        functools.partial(_gqa_proj_kernel, rope=rope, q_scale=q_scale),
        grid=(m // tm, n // tn),
        in_specs=[
            pl.BlockSpec((tm, d), lambda i, j: (i, 0)),
            pl.BlockSpec((1, d), row), pl.BlockSpec((1, d), row), pl.BlockSpec((1, d), row),
            pl.BlockSpec((None, d, tn), lambda i, j: (layer, 0, j)),
            head, head, pos, pos,
        ],
        out_specs=pl.BlockSpec((tm, tn), lambda i, j: (i, j)),
        out_shape=jax.ShapeDtypeStruct((m, n), BF16),
        scratch_shapes=[pltpu.VMEM((tm, d), BF16)],
        compiler_params=_params("parallel", "arbitrary"),
        name="gqa_projection",
    )(h, g, shift, scale, w, q_norm[None], k_norm[None], cos_full, sin_signed)


def _gqa_kernel(q_ref, k_ref, v_ref, kc_ref, vc_ref, o_ref, m_sc, l_sc, acc_sc, *, tk):
    tq = q_ref.shape[0]
    n_tok = k_ref.shape[0]
    q = jnp.concatenate([q_ref[:, g * LANES:(g + 1) * LANES] for g in range(GQA_GROUP)], axis=0)
    m_sc[...] = jnp.full_like(m_sc, -jnp.inf)
    l_sc[...] = jnp.zeros_like(l_sc)
    acc_sc[...] = jnp.zeros_like(acc_sc)

    def update(kb, vb):
        s = _dot_nt(q, kb)
        blocks = [s[:, c * LANES:(c + 1) * LANES] for c in range(kb.shape[0] // LANES)]
        m_old = m_sc[...]
        block_max = jnp.max(functools.reduce(jnp.maximum, blocks), axis=-1, keepdims=True)
        m_new = jnp.maximum(m_old, block_max)
        alpha = jnp.exp2(m_old - m_new)
        p = [jnp.exp2(b - m_new) for b in blocks]
        l_sc[...] = alpha * l_sc[...] + functools.reduce(jnp.add, p)
        acc_sc[...] = alpha * acc_sc[...] + _dot(jnp.concatenate(p, axis=-1).astype(BF16), vb)
        m_sc[...] = m_new

    def body(j, carry):
        r = pl.multiple_of(j * tk, tk)
        update(k_ref[pl.ds(r, tk), :], v_ref[pl.ds(r, tk), :])
        return carry

    lax.fori_loop(0, n_tok // tk, body, 0)
    update(kc_ref[...], vc_ref[...])
    o = acc_sc[...] / jnp.sum(l_sc[...], axis=-1, keepdims=True)
    for g in range(GQA_GROUP):
        o_ref[:, g * LANES:(g + 1) * LANES] = o[g * tq:(g + 1) * tq].astype(o_ref.dtype)


def gqa_attention(qkv, qkv_c):
    n_tok = qkv.shape[0]
    n_ctx = qkv_c.shape[0]
    tq = min(n_tok, 256)
    tk = min(n_tok, 512)
    kv0 = GQA_HEADS
    v0 = GQA_HEADS + GQA_KV_HEADS
    rows = GQA_GROUP * tq
    return pl.pallas_call(
        functools.partial(_gqa_kernel, tk=tk),
        grid=(GQA_KV_HEADS, n_tok // tq),
        in_specs=[
            pl.BlockSpec((tq, GQA_GROUP * LANES), lambda kv, i: (i, kv)),
            pl.BlockSpec((n_tok, LANES), lambda kv, i: (0, kv0 + kv)),
            pl.BlockSpec((n_tok, LANES), lambda kv, i: (0, v0 + kv)),
            pl.BlockSpec((n_ctx, LANES), lambda kv, i: (0, kv0 + kv)),
            pl.BlockSpec((n_ctx, LANES), lambda kv, i: (0, v0 + kv)),
        ],
        out_specs=pl.BlockSpec((tq, GQA_GROUP * LANES), lambda kv, i: (i, kv)),
        out_shape=jax.ShapeDtypeStruct((n_tok, GQA_HEADS * LANES), BF16),
        scratch_shapes=[pltpu.VMEM((rows, LANES), F32)] * 3,
        compiler_params=_params("parallel", "arbitrary"),
        name="gqa_attention",
    )(qkv, qkv, qkv, qkv_c, qkv_c)


def axial_rope_tables(n_tok):
    t = jnp.arange(n_tok, dtype=jnp.int32)
    row = (t // GRID_W).astype(F32)
    col = (t % GRID_W).astype(F32)
    n_freq = LANES // 4
    inv_freq = ROPE_THETA ** (-jnp.arange(n_freq, dtype=F32) / n_freq)
    ang = jnp.concatenate([row[:, None] * inv_freq, col[:, None] * inv_freq], axis=-1)
    cos, sin = jnp.cos(ang), jnp.sin(ang)
    return jnp.concatenate([cos, cos], axis=-1), jnp.concatenate([-sin, sin], axis=-1)


def _conv_silu_kernel(prev_ref, cur_ref, next_ref, w_ref, b_ref, o_ref):
    i = pl.program_id(0)
    tl = cur_ref.shape[0]
    prev = jnp.where(i == 0, 0.0, prev_ref[...].astype(F32))
    nxt = jnp.where(i == pl.num_programs(0) - 1, 0.0, next_ref[...].astype(F32))
    ext = jnp.concatenate([prev, cur_ref[...].astype(F32), nxt], axis=0)
    n_ext = ext.shape[0]
    pad = (SSM_CONV_W - 1) // 2
    acc = jnp.zeros((tl, ext.shape[1]), F32) + b_ref[...]
    for k in range(SSM_CONV_W):
        shifted = ext if k == pad else pltpu.roll(ext, (pad - k) % n_ext, 0)
        acc = acc + shifted[CONV_HALO:CONV_HALO + tl] * w_ref[k:k + 1, :]
    o_ref[...] = _silu(acc).astype(o_ref.dtype)


def conv_silu(zxbc, conv_w, conv_b):
    n = zxbc.shape[0]
    tl = min(n, 512)
    tc = 512
    c0 = SSM_D_INNER // tc
    hb = tl // CONV_HALO
    n_halo = n // CONV_HALO
    return pl.pallas_call(
        _conv_silu_kernel,
        grid=(n // tl, SSM_CONV_DIM // tc),
        in_specs=[
            pl.BlockSpec((CONV_HALO, tc), lambda i, j: (jnp.maximum(i * hb - 1, 0), c0 + j)),
            pl.BlockSpec((tl, tc), lambda i, j: (i, c0 + j)),
            pl.BlockSpec((CONV_HALO, tc), lambda i, j: (jnp.minimum((i + 1) * hb, n_halo - 1), c0 + j)),
            pl.BlockSpec((SSM_CONV_W, tc), lambda i, j: (0, j)),
            pl.BlockSpec((1, tc), lambda i, j: (0, j)),
        ],
        out_specs=pl.BlockSpec((tl, tc), lambda i, j: (i, j)),
        out_shape=jax.ShapeDtypeStruct((n, SSM_CONV_DIM), BF16),
        compiler_params=_params("parallel", "parallel"),
        name="conv_silu",
    )(zxbc, zxbc, zxbc, conv_w, conv_b)


def _ssd_prep_kernel(dt_ref, bias_ref, alog_ref, delta_ref, acs_ref):
    x = dt_ref[...] + bias_ref[...]
    delta = jnp.maximum(x, 0.0) + jnp.log1p(jnp.exp(-jnp.abs(x)))
    da = delta * (-jnp.exp(alog_ref[...]))
    q = x.shape[0]
    i = lax.broadcasted_iota(jnp.int32, (q, q), 0)
    j = lax.broadcasted_iota(jnp.int32, (q, q), 1)
    hi = lax.Precision.HIGHEST
    fwd = jnp.dot((i >= j).astype(F32), da, preferred_element_type=F32, precision=hi)
    bwd = jnp.dot((i <= j).astype(F32), da, preferred_element_type=F32, precision=hi)
    lane = lax.broadcasted_iota(jnp.int32, da.shape, 1)
    delta_ref[...] = delta
    acs_ref[...] = jnp.where(lane < SSM_HEADS, fwd, bwd)


def ssd_prep(dt, dt_bias, a_log):
    n = dt.shape[0]
    q = SSM_CHUNK
    blk = pl.BlockSpec((q, 2 * SSM_HEADS), lambda c: (c, 0))
    row = pl.BlockSpec((1, 2 * SSM_HEADS), lambda c: (0, 0))
    return pl.pallas_call(
        _ssd_prep_kernel,
        grid=(n // q,),
        in_specs=[blk, row, row],
        out_specs=[blk, blk],
        out_shape=[jax.ShapeDtypeStruct(dt.shape, F32)] * 2,
        compiler_params=_params("parallel"),
        name="ssd_prep",
    )(dt, dt_bias.reshape(1, -1), a_log.reshape(1, -1))


def _ssd_kernel(x_ref, b_ref, c_ref, dl_ref, ac_ref, ar_ref, init_ref, y_ref, fin_ref, state_sc):
    d = pl.program_id(0)
    c = pl.program_id(2)
    q = SSM_CHUNK

    @pl.when(c == 0)
    def _():
        state_sc[...] = init_ref[0]

    bm = b_ref[...]
    cm = c_ref[...]
    dl = dl_ref[0, 0]
    ac = ac_ref[0, 0]
    ar = ar_ref[0, 0]
    tot = jnp.where(d == 0, ac[q - 1:q, :], ac[0:1, :])
    e_in = jnp.exp(ac)
    e_out = jnp.exp(tot - ac)
    e_tot = jnp.exp(tot)
    cb = _dot_nt(cm, bm)
    i = lax.broadcasted_iota(jnp.int32, (q, q), 0)
    j = lax.broadcasted_iota(jnp.int32, (q, q), 1)
    causal = jnp.where(d == 0, i - j, j - i) >= 0
    lo = lax.broadcasted_iota(jnp.int32, (q, LANES), 1) < SSM_HEAD_DIM
    state = state_sc[...]
    y_off = _dot(cm, state.astype(BF16))

    def pair(v, p):
        return jnp.where(lo[:v.shape[0]], v[:, 2 * p:2 * p + 1], v[:, 2 * p + 1:2 * p + 2])

    for p in range(SSM_HEADS_PER_GROUP // 2):
        cols = slice(p * LANES, (p + 1) * LANES)
        xdt = x_ref[:, cols].astype(F32) * pair(dl, p)
        decay = []
        for hd in (2 * p, 2 * p + 1):
            seg = jnp.where(causal, ac[:, hd:hd + 1] - ar[hd:hd + 1, :], -jnp.inf)
            decay.append(cb * jnp.exp(seg))
        res = _dot(jnp.concatenate(decay, axis=0).astype(BF16), xdt.astype(BF16))
        y_diag = jnp.where(lo, res[:q], res[q:])
        y_ref[0, :, cols] = (y_diag + y_off[:, cols] * pair(e_in, p)).astype(y_ref.dtype)
        s_new = _dot_tn(bm, (xdt * pair(e_out, p)).astype(BF16))
        state_sc[:, cols] = state[:, cols] * pair(e_tot, p) + s_new

    @pl.when(c == pl.num_programs(2) - 1)
    def _():
        fin_ref[0] = state_sc[...]


def ssd_scan(xbc, delta, acs, init_state):
    n = xbc.shape[0]
    q = SSM_CHUNK
    nc = n // q
    hg = SSM_HEADS_PER_GROUP
    gw = hg * SSM_HEAD_DIM
    col = lambda t: t.reshape(n, 2, SSM_GROUPS, hg).transpose(1, 2, 0, 3)
    dl_col, ac_col = col(delta), col(acs)
    ac_row = ac_col.transpose(0, 1, 3, 2)
    b0 = SSM_D_INNER // SSM_STATE
    c0 = b0 + SSM_GROUPS
    chunk = lambda d, c: c + d * (nc - 1 - 2 * c)
    col_spec = pl.BlockSpec((1, 1, q, hg), lambda d, g, c: (d, g, chunk(d, c), 0))
    state_spec = pl.BlockSpec((1, SSM_STATE, gw), lambda d, g, c: (d, 0, g))
    return pl.pallas_call(
        _ssd_kernel,
        grid=(2, SSM_GROUPS, nc),
        in_specs=[
            pl.BlockSpec((q, gw), lambda d, g, c: (chunk(d, c), g)),
            pl.BlockSpec((q, SSM_STATE), lambda d, g, c: (chunk(d, c), b0 + g)),
            pl.BlockSpec((q, SSM_STATE), lambda d, g, c: (chunk(d, c), c0 + g)),
            col_spec, col_spec,
            pl.BlockSpec((1, 1, hg, q), lambda d, g, c: (d, g, 0, chunk(d, c))),
            state_spec,
        ],
        out_specs=[pl.BlockSpec((1, q, gw), lambda d, g, c: (d, chunk(d, c), g)), state_spec],
        out_shape=[jax.ShapeDtypeStruct((2, n, SSM_D_INNER), F32),
                   jax.ShapeDtypeStruct((2, SSM_STATE, SSM_D_INNER), F32)],
        scratch_shapes=[pltpu.VMEM((SSM_STATE, gw), F32)],
        compiler_params=_params("parallel", "parallel", "arbitrary"),
        name="ssd_scan",
    )(xbc, xbc, xbc, dl_col, ac_col, ac_row, init_state)


def _ssm_gate_norm_kernel(yf_ref, yb_ref, x_ref, z_ref, dsum_ref, g_ref, o_ref):
    y = yf_ref[0] + yb_ref[0] + dsum_ref[...] * x_ref[...].astype(F32)
    v = y * _silu(z_ref[...].astype(F32))
    var = jnp.mean(v * v, axis=-1, keepdims=True)
    o_ref[...] = ((v * lax.rsqrt(var + EPS)) * g_ref[...]).astype(o_ref.dtype)


def ssm_gate_norm(y, xbc, zxbc, dsum, norm_g):
    n = xbc.shape[0]
    tl = min(n, 256)
    di = SSM_D_INNER
    row = pl.BlockSpec((1, di), lambda i: (0, 0))
    return pl.pallas_call(
        _ssm_gate_norm_kernel,
        grid=(n // tl,),
        in_specs=[
            pl.BlockSpec((1, tl, di), lambda i: (0, i, 0)),
            pl.BlockSpec((1, tl, di), lambda i: (1, i, 0)),
            pl.BlockSpec((tl, di), lambda i: (i, 0)),
            pl.BlockSpec((tl, di), lambda i: (i, 0)),
            row, row,
        ],
        out_specs=pl.BlockSpec((tl, di), lambda i: (i, 0)),
        out_shape=jax.ShapeDtypeStruct((n, di), BF16),
        compiler_params=_params("parallel"),
        name="ssm_gate_norm",
    )(y, y, xbc, zxbc, dsum, norm_g)


def mamba2_bidirectional(h, hc, g, mod_l, mod_c, w_in, w_in_bf16, w_out_bf16, layer, conv_w, conv_b, a_log, dt_bias,
                         d_skip, norm_g):
    n_main = SSM_D_INNER + SSM_CONV_DIM
    w_dt = w_in[layer, :, n_main:][None]
    dsum = jnp.repeat(d_skip[0] + d_skip[1], SSM_HEAD_DIM)[None]
    state = jnp.zeros((2, SSM_STATE, SSM_D_INNER), F32)
    outs = []
    for t, mod in ((hc, mod_c), (h, mod_l)):
        shift, scale, gate = mod[3:4], mod[4:5], mod[5:6]
        zxbc = norm_mod_matmul(t, g, shift, scale, w_in_bf16, layer, BF16, n=n_main)
        dt = norm_mod_matmul(t, g, shift, scale, w_dt, 0, F32, precise=True)
        xbc = conv_silu(zxbc, conv_w, conv_b[None])
        delta, acs = ssd_prep(dt, dt_bias, a_log)
        y, state = ssd_scan(xbc, delta, acs, state)
        yn = ssm_gate_norm(y, xbc, zxbc, dsum, norm_g[None])
        outs.append(matmul_gated_residual(yn, w_out_bf16, layer, t, gate))
    return outs[1], outs[0]


def kernel(x, c, ctx, c_ctx, ada_w, ada_b, norm_g, ffn_w_in, ffn_w_out, na_w_qkv, na_rpb, na_w_o, ssm_w_in, ssm_conv_w, ssm_conv_b, ssm_a_log, ssm_dt_bias, ssm_d, ssm_norm_g, ssm_w_out, gqa_w_qkv, gqa_q_norm, gqa_k_norm, gqa_w_o, final_norm_g):
    bsz, n_tok, d = x.shape
    assert bsz == 1, "the kernels take one sequence"
    depth = ada_w.shape[0]
    h, hc = x[0], ctx[0]
    mod = adaln_mod(c, c_ctx, ada_w, ada_b)
    cos_full, sin_signed = axial_rope_tables(n_tok)
    rope_off = jnp.zeros((ctx.shape[1], LANES), F32)
    ffn_w_in_b, ffn_w_out_b = ffn_w_in.astype(BF16), ffn_w_out.astype(BF16)
    na_w_qkv_b, na_w_o_b = na_w_qkv.astype(BF16), na_w_o.astype(BF16)
    ssm_w_in_b, ssm_w_out_b = ssm_w_in.astype(BF16), ssm_w_out.astype(BF16)
    gqa_w_qkv_b, gqa_w_o_b = gqa_w_qkv.astype(BF16), gqa_w_o.astype(BF16)
    gqa_q_scale = LANES ** -0.5 * LOG2_E

    def ffn(t, m, i, k, final_g=None):
        return ffn_half_step(t, norm_g[i, 2 * k][None], m[6 * k:6 * k + 1], m[6 * k + 1:6 * k + 2],
                             m[6 * k + 2:6 * k + 3], ffn_w_in_b, ffn_w_out_b, i, k, final_g)

    for i in range(depth):
        last = i == depth - 1
        ml, mc = mod[i, 0], mod[i, 1]
        g_mix = norm_g[i, 1][None]
        h = ffn(h, ml, i, 0)
        hc = ffn(hc, mc, i, 0)
        kind, j = i % 3, i // 3
        if kind == 0:
            qkv = norm_mod_matmul(h, g_mix, ml[3:4], ml[4:5], na_w_qkv_b, j, BF16)
            qkv_c = norm_mod_matmul(hc, g_mix, mc[3:4], mc[4:5], na_w_qkv_b, j, BF16)
            o = neighbourhood_attention(qkv, qkv_c, na_rpb[j])
            h = matmul_gated_residual(o, na_w_o_b, j, h, ml[5:6])
            if not last:
                oc = ctx_attention(qkv_c, NA_HEADS, lambda hd: hd, lambda hd: NA_HEADS + hd,
                                   lambda hd: 2 * NA_HEADS + hd)
                hc = matmul_gated_residual(oc, na_w_o_b, j, hc, mc[5:6])
        elif kind == 1:
            h, hc = mamba2_bidirectional(h, hc, g_mix, ml, mc, ssm_w_in, ssm_w_in_b, ssm_w_out_b, j, ssm_conv_w[j],
                                         ssm_conv_b[j], ssm_a_log[j], ssm_dt_bias[j], ssm_d[j], ssm_norm_g[j])
        else:
            qkv = gqa_projection(h, g_mix, ml[3:4], ml[4:5], gqa_w_qkv_b, j, gqa_q_norm[j], gqa_k_norm[j],
                                 cos_full, sin_signed, True, gqa_q_scale)
            qkv_c = gqa_projection(hc, g_mix, mc[3:4], mc[4:5], gqa_w_qkv_b, j, gqa_q_norm[j], gqa_k_norm[j],
                                   rope_off, rope_off, False, 1.0)
            o = gqa_attention(qkv, qkv_c)
            h = matmul_gated_residual(o, gqa_w_o_b, j, h, ml[5:6])
            if not last:
                oc = ctx_attention(qkv_c, GQA_HEADS, lambda hd: hd, lambda hd: GQA_HEADS + hd // GQA_GROUP,
                                   lambda hd: GQA_HEADS + GQA_KV_HEADS + hd // GQA_GROUP)
                hc = matmul_gated_residual(oc, gqa_w_o_b, j, hc, mc[5:6])
        h = ffn(h, ml, i, 1, final_norm_g[None] if last else None)
        if not last:
            hc = ffn(hc, mc, i, 1)
    return h[None]
```

```python
import functools

import jax
import jax.numpy as jnp
import numpy as np
from jax import lax
from jax.experimental import pallas as pl
from jax.experimental.pallas import tpu as pltpu

F32 = jnp.float32
BF16 = jnp.bfloat16

EPS = 1e-6
HALF_STEP = 0.5
N_MOD = 9
GRID_W = 64
LANES = 128

NA_HEADS = 16
NA_WIN_ROWS = 8
NA_WIN_COLS = 16
NA_Q_ROWS = 4
NA_BAND_ROWS = 12
NA_BLOCKS_PER_STEP = 4

SSM_HEADS = 64
SSM_HEAD_DIM = 64
SSM_GROUPS = 8
SSM_STATE = 128
SSM_CHUNK = 128
SSM_D_INNER = SSM_HEADS * SSM_HEAD_DIM
SSM_GN = SSM_GROUPS * SSM_STATE
SSM_CONV_W = 5
SSM_CONV_DIM = SSM_D_INNER + 2 * SSM_GN
SSM_HEADS_PER_GROUP = SSM_HEADS // SSM_GROUPS
SSD_CHUNKS_PER_STEP = 4
CONV_HALO = 16

GQA_HEADS = 16
GQA_KV_HEADS = 4
GQA_GROUP = GQA_HEADS // GQA_KV_HEADS
ROPE_THETA = 10000.0
LOG2_E = 1.4426950408889634

MASK_VALUE = -1e30
VMEM_LIMIT = 56 * 1024 * 1024


def _params(*sem):
    return pltpu.CompilerParams(dimension_semantics=sem, vmem_limit_bytes=VMEM_LIMIT)


def _silu(x):
    return x / (1.0 + jnp.exp(-x))


def _dot(a, b):
    return jnp.dot(a, b, preferred_element_type=F32)


def _dot_nt(a, b):
    return lax.dot_general(a, b, (((1,), (1,)), ((), ())), preferred_element_type=F32)


def _dot_tn(a, b):
    return lax.dot_general(a, b, (((0,), (0,)), ((), ())), preferred_element_type=F32)


def _norm_modulate(x, g, shift, scale):
    var = jnp.mean(x * x, axis=-1, keepdims=True)
    return (x * lax.rsqrt(var + EPS)) * g * (1.0 + scale) + shift


def _adaln_kernel(c_ref, w_ref, b_ref, o_ref, s_sc, *, rows, group):
    tk, tn = w_ref.shape
    s_sc[...] = _silu(c_ref[...])

    @pl.when(pl.program_id(2) == 0)
    def _():
        o_ref[...] = jnp.broadcast_to(b_ref[...], o_ref.shape)

    for lg in range(tn // group):
        cols = slice(lg * group, (lg + 1) * group)

        def body(i, carry):
            a0, a1 = carry
            r = pl.multiple_of(i * rows, rows)
            w = w_ref[pl.ds(r, rows), cols]
            s0 = jnp.concatenate([s_sc[0, pl.ds(r, rows), :]] * (group // LANES), axis=1)
            s1 = jnp.concatenate([s_sc[1, pl.ds(r, rows), :]] * (group // LANES), axis=1)
            return a0 + w * s0, a1 + w * s1

        zero = jnp.zeros((rows, group), F32)
        a0, a1 = lax.fori_loop(0, tk // rows, body, (zero, zero), unroll=2)
        o_ref[0:1, cols] += jnp.sum(a0, axis=0, keepdims=True)
        o_ref[1:2, cols] += jnp.sum(a1, axis=0, keepdims=True)


def adaln_mod(c, c_ctx, ada_w, ada_b):
    depth, d, n = ada_w.shape
    tk = 256
    tn = n // 4 if n % 2048 == 0 else 512
    cb = jnp.broadcast_to(jnp.stack([c[0], c_ctx])[:, :, None], (2, d, LANES))
    out = pl.pallas_call(
        functools.partial(_adaln_kernel, rows=16, group=512),
        grid=(depth, n // tn, d // tk),
        in_specs=[
            pl.BlockSpec((2, tk, LANES), lambda l, j, k: (0, k, 0)),
            pl.BlockSpec((None, tk, tn), lambda l, j, k: (l, k, j)),
            pl.BlockSpec((None, 1, tn), lambda l, j, k: (l, 0, j)),
        ],
        out_specs=pl.BlockSpec((None, 2, tn), lambda l, j, k: (l, 0, j)),
        out_shape=jax.ShapeDtypeStruct((depth, 2, n), F32),
        scratch_shapes=[pltpu.VMEM((2, tk, LANES), F32)],
        compiler_params=_params("parallel", "parallel", "arbitrary"),
        name="adaln_mod",
    )(cb, ada_w, ada_b.reshape(depth, 1, n))
    return out.reshape(depth, 2, N_MOD, d)


def _ffn_kernel(h_ref, g_ref, sh_ref, sc_ref, gate_ref, wa_ref, wb_ref, wo_ref, fg_ref, o_ref,
                xn_sc, acc_sc, *, final_norm):
    j = pl.program_id(1)

    @pl.when(j == 0)
    def _():
        xn = _norm_modulate(h_ref[...], g_ref[...], sh_ref[...], sc_ref[...])
        xn_sc[...] = xn.astype(BF16)
        acc_sc[...] = jnp.zeros_like(acc_sc)

    xn = xn_sc[...]
    a = _dot(xn, wa_ref[...])
    b = _dot(xn, wb_ref[...])
    hm = (_silu(a) * b).astype(BF16)
    acc_sc[...] += _dot(hm, wo_ref[...])

    @pl.when(j == pl.num_programs(1) - 1)
    def _():
        out = h_ref[...] + (HALF_STEP * gate_ref[...]) * acc_sc[...]
        if final_norm:
            var = jnp.mean(out * out, axis=-1, keepdims=True)
            out = (out * lax.rsqrt(var + EPS)) * fg_ref[...]
        o_ref[...] = out


def ffn_half_step(h, g, shift, scale, gate, w_in, w_out, layer, k, final_g=None):
    m, d = h.shape
    f = w_out.shape[2]
    tm = min(m, 512)
    tf = 512
    nf = f // tf
    row = lambda i, j: (0, 0)
    fg = final_g if final_g is not None else g
    return pl.pallas_call(
        functools.partial(_ffn_kernel, final_norm=final_g is not None),
        grid=(m // tm, nf),
        in_specs=[
            pl.BlockSpec((tm, d), lambda i, j: (i, 0)),
            pl.BlockSpec((1, d), row), pl.BlockSpec((1, d), row),
            pl.BlockSpec((1, d), row), pl.BlockSpec((1, d), row),
            pl.BlockSpec((None, None, d, tf), lambda i, j: (layer, k, 0, j)),
            pl.BlockSpec((None, None, d, tf), lambda i, j: (layer, k, 0, j + nf)),
            pl.BlockSpec((None, None, tf, d), lambda i, j: (layer, k, j, 0)),
            pl.BlockSpec((1, d), row),
        ],
        out_specs=pl.BlockSpec((tm, d), lambda i, j: (i, 0)),
        out_shape=jax.ShapeDtypeStruct((m, d), F32),
        scratch_shapes=[pltpu.VMEM((tm, d), BF16), pltpu.VMEM((tm, d), F32)],
        compiler_params=_params("parallel", "arbitrary"),
        name="ffn_half_step",
    )(h, g, shift, scale, gate, w_in, w_in, w_out, fg)


def _nmm_kernel(h_ref, g_ref, sh_ref, sc_ref, w_ref, o_ref, xn_sc, *, precise):
    @pl.when(pl.program_id(1) == 0)
    def _():
        xn = _norm_modulate(h_ref[...], g_ref[...], sh_ref[...], sc_ref[...])
        xn_sc[...] = xn.astype(xn_sc.dtype)

    if precise:
        y = jnp.dot(xn_sc[...], w_ref[...], preferred_element_type=F32, precision=lax.Precision.HIGHEST)
    else:
        y = _dot(xn_sc[...], w_ref[...])
    o_ref[...] = y.astype(o_ref.dtype)


def norm_mod_matmul(h, g, shift, scale, w, layer, out_dtype, n=None, precise=False):
    m, d = h.shape
    n = w.shape[2] if n is None else n
    tm = min(m, 512)
    tn = n if n <= 1024 else (1024 if n % 1024 == 0 else 512)
    row = lambda i, j: (0, 0)
    return pl.pallas_call(
        functools.partial(_nmm_kernel, precise=precise),
        grid=(m // tm, n // tn),
        in_specs=[
            pl.BlockSpec((tm, d), lambda i, j: (i, 0)),
            pl.BlockSpec((1, d), row), pl.BlockSpec((1, d), row), pl.BlockSpec((1, d), row),
            pl.BlockSpec((None, d, tn), lambda i, j: (layer, 0, j)),
        ],
        out_specs=pl.BlockSpec((tm, tn), lambda i, j: (i, j)),
        out_shape=jax.ShapeDtypeStruct((m, n), out_dtype),
        scratch_shapes=[pltpu.VMEM((tm, d), F32 if precise else BF16)],
        compiler_params=_params("parallel", "arbitrary"),
        name="norm_mod_matmul",
    )(h, g, shift, scale, w)


def _mgr_kernel(a_ref, w_ref, h_ref, gate_ref, o_ref):
    o_ref[...] = h_ref[...] + gate_ref[...] * _dot(a_ref[...], w_ref[...])


def matmul_gated_residual(a, w, layer, h, gate):
    m, k = a.shape
    n = w.shape[2]
    tm = min(m, 512)
    tn = n if k * n * 2 <= 8 * 1024 * 1024 else min(n, 1024)
    return pl.pallas_call(
        _mgr_kernel,
        grid=(m // tm, n // tn),
        in_specs=[
            pl.BlockSpec((tm, k), lambda i, j: (i, 0)),
            pl.BlockSpec((None, k, tn), lambda i, j: (layer, 0, j)),
            pl.BlockSpec((tm, tn), lambda i, j: (i, j)),
            pl.BlockSpec((1, tn), lambda i, j: (0, j)),
        ],
        out_specs=pl.BlockSpec((tm, tn), lambda i, j: (i, j)),
        out_shape=jax.ShapeDtypeStruct((m, n), F32),
        compiler_params=_params("parallel", "parallel"),
        name="matmul_gated_residual",
    )(a, w, h, gate)


def _ctx_attn_kernel(q_ref, k_ref, v_ref, o_ref):
    q = q_ref[...]
    s = _dot_nt(q, k_ref[...]) * (q.shape[-1] ** -0.5)
    m = jnp.max(s, axis=-1, keepdims=True)
    p = jnp.exp(s - m)
    l = jnp.sum(p, axis=-1, keepdims=True)
    o_ref[...] = (_dot(p.astype(BF16), v_ref[...]) / l).astype(o_ref.dtype)


def ctx_attention(qkv, n_heads, q_blk, k_blk, v_blk):
    n_ctx = qkv.shape[0]
    spec = lambda f: pl.BlockSpec((n_ctx, LANES), lambda h: (0, f(h)))
    return pl.pallas_call(
        _ctx_attn_kernel,
        grid=(n_heads,),
        in_specs=[spec(q_blk), spec(k_blk), spec(v_blk)],
        out_specs=pl.BlockSpec((n_ctx, LANES), lambda h: (0, h)),
        out_shape=jax.ShapeDtypeStruct((n_ctx, n_heads * LANES), BF16),
        compiler_params=_params("parallel"),
        name="ctx_attention",
    )(qkv, qkv, qkv)


def _na_kernel(q_ref, k_ref, v_ref, kc_ref, vc_ref, *rest, n_rows):
    bias_refs, o_ref = rest[:-1], rest[-1]
    nq = NA_Q_ROWS * GRID_W
    n_band = NA_BAND_ROWS * GRID_W
    scale = q_ref.shape[-1] ** -0.5 * LOG2_E
    for blk, bias_ref in enumerate(bias_refs):
        r0 = (pl.program_id(1) * len(bias_refs) + blk) * NA_Q_ROWS
        b0 = jnp.clip(r0 - NA_WIN_ROWS // 2, 0, n_rows - NA_BAND_ROWS)
        start = pl.multiple_of(b0 * GRID_W, GRID_W)
        q = q_ref[blk * nq:(blk + 1) * nq, :]
        s_w = _dot_nt(q, k_ref[pl.ds(start, n_band), :]) * scale + bias_ref[0, 0]
        s_c = _dot_nt(q, kc_ref[...]) * scale
        m = jnp.maximum(jnp.max(s_w, axis=-1, keepdims=True), jnp.max(s_c, axis=-1, keepdims=True))
        p_w = jnp.exp2(s_w - m)
        p_c = jnp.exp2(s_c - m)
        l = jnp.sum(p_w, axis=-1, keepdims=True) + jnp.sum(p_c, axis=-1, keepdims=True)
        o = _dot(p_w.astype(BF16), v_ref[pl.ds(start, n_band), :]) + _dot(p_c.astype(BF16), vc_ref[...])
        o_ref[blk * nq:(blk + 1) * nq, :] = (o / l).astype(o_ref.dtype)


def _na_block_geometry(n_rows):
    nq, nk = NA_Q_ROWS * GRID_W, NA_BAND_ROWS * GRID_W
    valid, drow = [], []
    for r0 in (0, NA_Q_ROWS, n_rows - NA_Q_ROWS):
        b0 = int(np.clip(r0 - NA_WIN_ROWS // 2, 0, n_rows - NA_BAND_ROWS))
        qr = r0 + np.arange(nq) // GRID_W
        qc = np.arange(nq) % GRID_W
        kr = b0 + np.arange(nk) // GRID_W
        kc = np.arange(nk) % GRID_W
        rs = np.clip(qr - NA_WIN_ROWS // 2, 0, n_rows - NA_WIN_ROWS)
        cs = np.clip(qc - NA_WIN_COLS // 2, 0, GRID_W - NA_WIN_COLS)
        ok = ((kr[None] >= rs[:, None]) & (kr[None] < rs[:, None] + NA_WIN_ROWS)
              & (kc[None] >= cs[:, None]) & (kc[None] < cs[:, None] + NA_WIN_COLS))
        valid.append(ok)
        drow.append((b0 + np.arange(NA_BAND_ROWS))[None, :] - (r0 + np.arange(NA_Q_ROWS))[:, None])
    return np.stack(valid), np.stack(drow)


def _na_bias_table(rpb, n_rows):
    n_heads = rpb.shape[0]
    valid, drow = _na_block_geometry(n_rows)
    dcol = np.arange(GRID_W)[None, :] - np.arange(GRID_W)[:, None] + NA_WIN_COLS - 1
    in_table = (dcol >= 0) & (dcol < 2 * NA_WIN_COLS - 1)
    onehot = (np.clip(dcol, 0, 2 * NA_WIN_COLS - 2)[None] == np.arange(2 * NA_WIN_COLS - 1)[:, None, None]) & in_table
    toeplitz = jnp.einsum("hrd,dqk->hrqk", rpb, jnp.asarray(onehot, F32), precision=lax.Precision.HIGHEST)
    zero_tile = jnp.zeros((n_heads, GRID_W, GRID_W), F32)
    variants = []
    for v in range(3):
        rows = []
        for qi in range(NA_Q_ROWS):
            tiles = []
            for j in range(NA_BAND_ROWS):
                dr = int(drow[v, qi, j]) + NA_WIN_ROWS - 1
                tiles.append(toeplitz[:, dr] if 0 <= dr < 2 * NA_WIN_ROWS - 1 else zero_tile)
            rows.append(jnp.concatenate(tiles, axis=-1))
        variants.append(jnp.concatenate(rows, axis=-2))
    table = jnp.stack(variants)
    return jnp.where(jnp.asarray(valid)[:, None], table, MASK_VALUE)


def neighbourhood_attention(qkv, qkv_c, rpb):
    n_tok = qkv.shape[0]
    n_ctx = qkv_c.shape[0]
    n_rows = n_tok // GRID_W
    n_blocks = n_rows // NA_Q_ROWS
    nb = NA_BLOCKS_PER_STEP
    nq, nk = NA_Q_ROWS * GRID_W, NA_BAND_ROWS * GRID_W
    bias = _na_bias_table(rpb, n_rows) * LOG2_E
    h_ = NA_HEADS

    def bias_spec(blk):
        def index(h, s):
            rb = s * nb + blk
            return (jnp.where(rb == 0, 0, jnp.where(rb == n_blocks - 1, 2, 1)), h, 0, 0)
        return pl.BlockSpec((1, 1, nq, nk), index)

    return pl.pallas_call(
        functools.partial(_na_kernel, n_rows=n_rows),
        grid=(h_, n_blocks // nb),
        in_specs=[
            pl.BlockSpec((nb * nq, LANES), lambda h, s: (s, h)),
            pl.BlockSpec((n_tok, LANES), lambda h, s: (0, h_ + h)),
            pl.BlockSpec((n_tok, LANES), lambda h, s: (0, 2 * h_ + h)),
            pl.BlockSpec((n_ctx, LANES), lambda h, s: (0, h_ + h)),
            pl.BlockSpec((n_ctx, LANES), lambda h, s: (0, 2 * h_ + h)),
        ] + [bias_spec(blk) for blk in range(nb)],
        out_specs=pl.BlockSpec((nb * nq, LANES), lambda h, s: (s, h)),
        out_shape=jax.ShapeDtypeStruct((n_tok, h_ * LANES), BF16),
        compiler_params=_params("parallel", "arbitrary"),
        name="neighbourhood_attention",
    )(qkv, qkv, qkv, qkv_c, qkv_c, *([bias] * nb))


def _gqa_proj_kernel(h_ref, g_ref, sh_ref, sc_ref, w_ref, qn_ref, kn_ref, cos_ref, sin_ref, o_ref, xn_sc,
                     *, rope, q_scale):
    j = pl.program_id(1)
    tn = o_ref.shape[1]
    n_q = GQA_HEADS * LANES // tn

    @pl.when(j == 0)
    def _():
        xn = _norm_modulate(h_ref[...], g_ref[...], sh_ref[...], sc_ref[...])
        xn_sc[...] = xn.astype(BF16)

    y = _dot(xn_sc[...], w_ref[...])

    def normed(w, mult):
        for s in range(tn // LANES):
            cols = slice(s * LANES, (s + 1) * LANES)
            x = y[:, cols]
            var = jnp.mean(x * x, axis=-1, keepdims=True)
            t = (x * lax.rsqrt(var + EPS)) * w
            if rope:
                t = t * cos_ref[...] + pltpu.roll(t, LANES // 2, 1) * sin_ref[...]
            if mult != 1.0:
                t = t * mult
            o_ref[:, cols] = t.astype(o_ref.dtype)

    @pl.when(j < n_q)
    def _():
        normed(qn_ref[...], q_scale)

    @pl.when(j == n_q)
    def _():
        normed(kn_ref[...], 1.0)

    @pl.when(j > n_q)
    def _():
        o_ref[...] = y.astype(o_ref.dtype)


def gqa_projection(h, g, shift, scale, w, layer, q_norm, k_norm, cos_full, sin_signed, rope, q_scale):
    m, d = h.shape
    n = w.shape[2]
    tm = min(m, 512)
    tn = GQA_KV_HEADS * LANES
    row = lambda i, j: (0, 0)
    head = pl.BlockSpec((1, LANES), row)
    pos = pl.BlockSpec((tm, LANES), lambda i, j: (i, 0))
    return pl.pallas_call(
        functools.partial(_gqa_proj_kernel, rope=rope, q_scale=q_scale),
        grid=(m // tm, n // tn),
        in_specs=[
            pl.BlockSpec((tm, d), lambda i, j: (i, 0)),
            pl.BlockSpec((1, d), row), pl.BlockSpec((1, d), row), pl.BlockSpec((1, d), row),
            pl.BlockSpec((None, d, tn), lambda i, j: (layer, 0, j)),
            head, head, pos, pos,
        ],
        out_specs=pl.BlockSpec((tm, tn), lambda i, j: (i, j)),
        out_shape=jax.ShapeDtypeStruct((m, n), BF16),
        scratch_shapes=[pltpu.VMEM((tm, d), BF16)],
        compiler_params=_params("parallel", "arbitrary"),
        name="gqa_projection",
    )(h, g, shift, scale, w, q_norm[None], k_norm[None], cos_full, sin_signed)


def _gqa_kernel(q_ref, k_ref, v_ref, kc_ref, vc_ref, o_ref, m_sc, l_sc, acc_sc, *, tk):
    tq = q_ref.shape[0]
    n_tok = k_ref.shape[0]
    q = jnp.concatenate([q_ref[:, g * LANES:(g + 1) * LANES] for g in range(GQA_GROUP)], axis=0)
    m_sc[...] = jnp.full_like(m_sc, -jnp.inf)
    l_sc[...] = jnp.zeros_like(l_sc)
    acc_sc[...] = jnp.zeros_like(acc_sc)

    def update(kb, vb):
        s = _dot_nt(q, kb)
        blocks = [s[:, c * LANES:(c + 1) * LANES] for c in range(kb.shape[0] // LANES)]
        m_old = m_sc[...]
        block_max = jnp.max(functools.reduce(jnp.maximum, blocks), axis=-1, keepdims=True)
        m_new = jnp.maximum(m_old, block_max)
        alpha = jnp.exp2(m_old - m_new)
        p = [jnp.exp2(b - m_new) for b in blocks]
        l_sc[...] = alpha * l_sc[...] + functools.reduce(jnp.add, p)
        acc_sc[...] = alpha * acc_sc[...] + _dot(jnp.concatenate(p, axis=-1).astype(BF16), vb)
        m_sc[...] = m_new

    def body(j, carry):
        r = pl.multiple_of(j * tk, tk)
        update(k_ref[pl.ds(r, tk), :], v_ref[pl.ds(r, tk), :])
        return carry

    lax.fori_loop(0, n_tok // tk, body, 0)
    update(kc_ref[...], vc_ref[...])
    o = acc_sc[...] / jnp.sum(l_sc[...], axis=-1, keepdims=True)
    for g in range(GQA_GROUP):
        o_ref[:, g * LANES:(g + 1) * LANES] = o[g * tq:(g + 1) * tq].astype(o_ref.dtype)


def gqa_attention(qkv, qkv_c):
    n_tok = qkv.shape[0]
    n_ctx = qkv_c.shape[0]
    tq = min(n_tok, 256)
    tk = min(n_tok, 1024)
    kv0 = GQA_HEADS
    v0 = GQA_HEADS + GQA_KV_HEADS
    rows = GQA_GROUP * tq
    return pl.pallas_call(
        functools.partial(_gqa_kernel, tk=tk),
        grid=(GQA_KV_HEADS, n_tok // tq),
        in_specs=[
            pl.BlockSpec((tq, GQA_GROUP * LANES), lambda kv, i: (i, kv)),
            pl.BlockSpec((n_tok, LANES), lambda kv, i: (0, kv0 + kv)),
            pl.BlockSpec((n_tok, LANES), lambda kv, i: (0, v0 + kv)),
            pl.BlockSpec((n_ctx, LANES), lambda kv, i: (0, kv0 + kv)),
            pl.BlockSpec((n_ctx, LANES), lambda kv, i: (0, v0 + kv)),
        ],
        out_specs=pl.BlockSpec((tq, GQA_GROUP * LANES), lambda kv, i: (i, kv)),
        out_shape=jax.ShapeDtypeStruct((n_tok, GQA_HEADS * LANES), BF16),
        scratch_shapes=[pltpu.VMEM((rows, LANES), F32)] * 3,
        compiler_params=_params("parallel", "arbitrary"),
        name="gqa_attention",
    )(qkv, qkv, qkv, qkv_c, qkv_c)


def axial_rope_tables(n_tok):
    t = jnp.arange(n_tok, dtype=jnp.int32)
    row = (t // GRID_W).astype(F32)
    col = (t % GRID_W).astype(F32)
    n_freq = LANES // 4
    inv_freq = ROPE_THETA ** (-jnp.arange(n_freq, dtype=F32) / n_freq)
    ang = jnp.concatenate([row[:, None] * inv_freq, col[:, None] * inv_freq], axis=-1)
    cos, sin = jnp.cos(ang), jnp.sin(ang)
    return jnp.concatenate([cos, cos], axis=-1), jnp.concatenate([-sin, sin], axis=-1)


def _conv_silu_kernel(prev_ref, cur_ref, next_ref, w_ref, b_ref, o_ref):
    i = pl.program_id(0)
    tl = cur_ref.shape[0]
    prev = jnp.where(i == 0, 0.0, prev_ref[...].astype(F32))
    nxt = jnp.where(i == pl.num_programs(0) - 1, 0.0, next_ref[...].astype(F32))
    ext = jnp.concatenate([prev, cur_ref[...].astype(F32), nxt], axis=0)
    n_ext = ext.shape[0]
    pad = (SSM_CONV_W - 1) // 2
    acc = jnp.zeros((tl, ext.shape[1]), F32) + b_ref[...]
    for k in range(SSM_CONV_W):
        shifted = ext if k == pad else pltpu.roll(ext, (pad - k) % n_ext, 0)
        acc = acc + shifted[CONV_HALO:CONV_HALO + tl] * w_ref[k:k + 1, :]
    o_ref[...] = _silu(acc).astype(o_ref.dtype)


def conv_silu(zxbc, conv_w, conv_b):
    n = zxbc.shape[0]
    tl = min(n, 512)
    tc = 512
    c0 = SSM_D_INNER // tc
    hb = tl // CONV_HALO
    n_halo = n // CONV_HALO
    return pl.pallas_call(
        _conv_silu_kernel,
        grid=(n // tl, SSM_CONV_DIM // tc),
        in_specs=[
            pl.BlockSpec((CONV_HALO, tc), lambda i, j: (jnp.maximum(i * hb - 1, 0), c0 + j)),
            pl.BlockSpec((tl, tc), lambda i, j: (i, c0 + j)),
            pl.BlockSpec((CONV_HALO, tc), lambda i, j: (jnp.minimum((i + 1) * hb, n_halo - 1), c0 + j)),
            pl.BlockSpec((SSM_CONV_W, tc), lambda i, j: (0, j)),
            pl.BlockSpec((1, tc), lambda i, j: (0, j)),
        ],
        out_specs=pl.BlockSpec((tl, tc), lambda i, j: (i, j)),
        out_shape=jax.ShapeDtypeStruct((n, SSM_CONV_DIM), BF16),
        compiler_params=_params("parallel", "parallel"),
        name="conv_silu",
    )(zxbc, zxbc, zxbc, conv_w, conv_b)


def _ssd_prep_kernel(dt_ref, bias_ref, alog_ref, delta_ref, acs_ref):
    x = dt_ref[...] + bias_ref[...]
    delta = jnp.maximum(x, 0.0) + jnp.log1p(jnp.exp(-jnp.abs(x)))
    da = delta * (-jnp.exp(alog_ref[...]))
    q = x.shape[0]
    i = lax.broadcasted_iota(jnp.int32, (q, q), 0)
    j = lax.broadcasted_iota(jnp.int32, (q, q), 1)
    hi = lax.Precision.HIGHEST
    fwd = jnp.dot((i >= j).astype(F32), da, preferred_element_type=F32, precision=hi)
    bwd = jnp.dot((i <= j).astype(F32), da, preferred_element_type=F32, precision=hi)
    lane = lax.broadcasted_iota(jnp.int32, da.shape, 1)
    delta_ref[...] = delta
    acs_ref[...] = jnp.where(lane < SSM_HEADS, fwd, bwd)


def ssd_prep(dt, dt_bias, a_log):
    n = dt.shape[0]
    q = SSM_CHUNK
    blk = pl.BlockSpec((q, 2 * SSM_HEADS), lambda c: (c, 0))
    row = pl.BlockSpec((1, 2 * SSM_HEADS), lambda c: (0, 0))
    return pl.pallas_call(
        _ssd_prep_kernel,
        grid=(n // q,),
        in_specs=[blk, row, row],
        out_specs=[blk, blk],
        out_shape=[jax.ShapeDtypeStruct(dt.shape, F32)] * 2,
        compiler_params=_params("parallel"),
        name="ssd_prep",
    )(dt, dt_bias.reshape(1, -1), a_log.reshape(1, -1))


def _ssd_kernel(x_ref, b_ref, c_ref, ac_ref, ar_ref, dr_ref, init_ref, y_ref, fin_ref, state_sc, *, n_sub):
    d = pl.program_id(0)
    q = SSM_CHUNK

    @pl.when(pl.program_id(2) == 0)
    def _():
        state_sc[...] = init_ref[0]

    i = lax.broadcasted_iota(jnp.int32, (q, q), 0)
    j = lax.broadcasted_iota(jnp.int32, (q, q), 1)
    causal = jnp.where(d == 0, i - j, j - i) >= 0
    lo = lax.broadcasted_iota(jnp.int32, (q, LANES), 1) < SSM_HEAD_DIM

    def chunk(k, carry):
        ci = k + d * (n_sub - 1 - 2 * k)
        r = pl.multiple_of(ci * q, q)
        bm = b_ref[pl.ds(r, q), :]
        cm = c_ref[pl.ds(r, q), :]
        ac = ac_ref[0, 0, pl.ds(r, q), :]
        ar = ar_ref[0, 0, ci]
        dr = dr_ref[0, 0, ci]
        tot = jnp.where(d == 0, ar[:, q - 1:q], ar[:, 0:1])
        w_out = jnp.exp(tot - ar) * dr
        e_tot = jnp.exp(tot)
        cb = _dot_nt(cm, bm)
        bt = bm.astype(F32).T
        cf = cm.astype(F32)
        state = state_sc[...]
        for p in range(SSM_HEADS_PER_GROUP // 2):
            cols = slice(p * LANES, (p + 1) * LANES)
            x = x_ref[pl.ds(r, q), cols]
            lhs_y, lhs_s = [], []
            for hd in (2 * p, 2 * p + 1):
                a_i = jnp.broadcast_to(ac[:, hd:hd + 1], (q, q))
                seg = jnp.where(causal, a_i - ar[hd:hd + 1, :], -jnp.inf)
                intra = cb * jnp.exp(seg) * dr[hd:hd + 1, :]
                lhs_y.append(jnp.concatenate([intra, cf * jnp.exp(a_i)], axis=1).astype(BF16))
                lhs_s.append((bt * w_out[hd:hd + 1, :]).astype(BF16))
            rhs = jnp.concatenate([x, state[:, cols].astype(BF16)], axis=0)
            res = _dot(jnp.concatenate(lhs_y, axis=0), rhs)
            y_ref[0, pl.ds(r, q), cols] = jnp.where(lo, res[:q], res[q:]).astype(y_ref.dtype)
            s_res = _dot(jnp.concatenate(lhs_s, axis=0), x)
            s_new = jnp.where(lo, s_res[:SSM_STATE], s_res[SSM_STATE:])
            keep = jnp.where(lo[0:1], e_tot[2 * p:2 * p + 1, :], e_tot[2 * p + 1:2 * p + 2, :])
            state_sc[:, cols] = state[:, cols] * keep + s_new
        return carry

    lax.fori_loop(0, n_sub, chunk, 0)

    @pl.when(pl.program_id(2) == pl.num_programs(2) - 1)
    def _():
        fin_ref[0] = state_sc[...]


def ssd_scan(xbc, delta, acs, init_state):
    assert SSM_STATE == SSM_CHUNK == LANES
    n = xbc.shape[0]
    q = SSM_CHUNK
    nc = n // q
    n_sub = min(nc, SSD_CHUNKS_PER_STEP)
    ns = nc // n_sub
    rows = n_sub * q
    hg = SSM_HEADS_PER_GROUP
    gw = hg * SSM_HEAD_DIM
    ac_col = acs.reshape(n, 2, SSM_GROUPS, hg).transpose(1, 2, 0, 3)
    row = lambda t: t.reshape(nc, q, 2, SSM_GROUPS, hg).transpose(2, 3, 0, 4, 1)
    ac_row, dl_row = row(acs), row(delta)
    b0 = SSM_D_INNER // SSM_STATE
    c0 = b0 + SSM_GROUPS
    step = lambda d, s: s + d * (ns - 1 - 2 * s)
    row_spec = pl.BlockSpec((1, 1, n_sub, hg, q), lambda d, g, s: (d, g, step(d, s), 0, 0))
    state_spec = pl.BlockSpec((1, SSM_STATE, gw), lambda d, g, s: (d, 0, g))
    return pl.pallas_call(
        functools.partial(_ssd_kernel, n_sub=n_sub),
        grid=(2, SSM_GROUPS, ns),
        in_specs=[
            pl.BlockSpec((rows, gw), lambda d, g, s: (step(d, s), g)),
            pl.BlockSpec((rows, SSM_STATE), lambda d, g, s: (step(d, s), b0 + g)),
            pl.BlockSpec((rows, SSM_STATE), lambda d, g, s: (step(d, s), c0 + g)),
            pl.BlockSpec((1, 1, rows, hg), lambda d, g, s: (d, g, step(d, s), 0)),
            row_spec, row_spec,
            state_spec,
        ],
        out_specs=[pl.BlockSpec((1, rows, gw), lambda d, g, s: (d, step(d, s), g)), state_spec],
        out_shape=[jax.ShapeDtypeStruct((2, n, SSM_D_INNER), BF16),
                   jax.ShapeDtypeStruct((2, SSM_STATE, SSM_D_INNER), F32)],
        scratch_shapes=[pltpu.VMEM((SSM_STATE, gw), F32)],
        compiler_params=_params("parallel", "parallel", "arbitrary"),
        name="ssd_scan",
    )(xbc, xbc, xbc, ac_col, ac_row, dl_row, init_state)


def _ssm_gate_norm_kernel(yf_ref, yb_ref, x_ref, z_ref, dsum_ref, g_ref, o_ref):
    y = yf_ref[0].astype(F32) + yb_ref[0].astype(F32) + dsum_ref[...] * x_ref[...].astype(F32)
    v = y * _silu(z_ref[...].astype(F32))
    var = jnp.mean(v * v, axis=-1, keepdims=True)
    o_ref[...] = ((v * lax.rsqrt(var + EPS)) * g_ref[...]).astype(o_ref.dtype)


def ssm_gate_norm(y, xbc, zxbc, dsum, norm_g):
    n = xbc.shape[0]
    tl = min(n, 256)
    di = SSM_D_INNER
    row = pl.BlockSpec((1, di), lambda i: (0, 0))
    return pl.pallas_call(
        _ssm_gate_norm_kernel,
        grid=(n // tl,),
        in_specs=[
            pl.BlockSpec((1, tl, di), lambda i: (0, i, 0)),
            pl.BlockSpec((1, tl, di), lambda i: (1, i, 0)),
            pl.BlockSpec((tl, di), lambda i: (i, 0)),
            pl.BlockSpec((tl, di), lambda i: (i, 0)),
            row, row,
        ],
        out_specs=pl.BlockSpec((tl, di), lambda i: (i, 0)),
        out_shape=jax.ShapeDtypeStruct((n, di), BF16),
        compiler_params=_params("parallel"),
        name="ssm_gate_norm",
    )(y, y, xbc, zxbc, dsum, norm_g)


def mamba2_bidirectional(h, hc, g, mod_l, mod_c, w_in, w_in_bf16, w_out_bf16, layer, conv_w, conv_b, a_log, dt_bias,
                         d_skip, norm_g):
    n_main = SSM_D_INNER + SSM_CONV_DIM
    w_dt = w_in[layer, :, n_main:][None]
    dsum = jnp.repeat(d_skip[0] + d_skip[1], SSM_HEAD_DIM)[None]
    state = jnp.zeros((2, SSM_STATE, SSM_D_INNER), F32)
    outs = []
    for t, mod in ((hc, mod_c), (h, mod_l)):
        shift, scale, gate = mod[3:4], mod[4:5], mod[5:6]
        zxbc = norm_mod_matmul(t, g, shift, scale, w_in_bf16, layer, BF16, n=n_main)
        dt = norm_mod_matmul(t, g, shift, scale, w_dt, 0, F32, precise=True)
        xbc = conv_silu(zxbc, conv_w, conv_b[None])
        delta, acs = ssd_prep(dt, dt_bias, a_log)
        y, state = ssd_scan(xbc, delta, acs, state)
        yn = ssm_gate_norm(y, xbc, zxbc, dsum, norm_g[None])
        outs.append(matmul_gated_residual(yn, w_out_bf16, layer, t, gate))
    return outs[1], outs[0]


def kernel(x, c, ctx, c_ctx, ada_w, ada_b, norm_g, ffn_w_in, ffn_w_out, na_w_qkv, na_rpb, na_w_o, ssm_w_in, ssm_conv_w, ssm_conv_b, ssm_a_log, ssm_dt_bias, ssm_d, ssm_norm_g, ssm_w_out, gqa_w_qkv, gqa_q_norm, gqa_k_norm, gqa_w_o, final_norm_g):
    bsz, n_tok, d = x.shape
    assert bsz == 1, "the kernels take one sequence"
    depth = ada_w.shape[0]
    h, hc = x[0], ctx[0]
    mod = adaln_mod(c, c_ctx, ada_w, ada_b)
    cos_full, sin_signed = axial_rope_tables(n_tok)
    rope_off = jnp.zeros((ctx.shape[1], LANES), F32)
    ffn_w_in_b, ffn_w_out_b = ffn_w_in.astype(BF16), ffn_w_out.astype(BF16)
    na_w_qkv_b, na_w_o_b = na_w_qkv.astype(BF16), na_w_o.astype(BF16)
    ssm_w_in_b, ssm_w_out_b = ssm_w_in.astype(BF16), ssm_w_out.astype(BF16)
    gqa_w_qkv_b, gqa_w_o_b = gqa_w_qkv.astype(BF16), gqa_w_o.astype(BF16)
    gqa_q_scale = LANES ** -0.5 * LOG2_E

    def ffn(t, m, i, k, final_g=None):
        return ffn_half_step(t, norm_g[i, 2 * k][None], m[6 * k:6 * k + 1], m[6 * k + 1:6 * k + 2],
                             m[6 * k + 2:6 * k + 3], ffn_w_in_b, ffn_w_out_b, i, k, final_g)

    for i in range(depth):
        last = i == depth - 1
        ml, mc = mod[i, 0], mod[i, 1]
        g_mix = norm_g[i, 1][None]
        h = ffn(h, ml, i, 0)
        hc = ffn(hc, mc, i, 0)
        kind, j = i % 3, i // 3
        if kind == 0:
            qkv = norm_mod_matmul(h, g_mix, ml[3:4], ml[4:5], na_w_qkv_b, j, BF16)
            qkv_c = norm_mod_matmul(hc, g_mix, mc[3:4], mc[4:5], na_w_qkv_b, j, BF16)
            o = neighbourhood_attention(qkv, qkv_c, na_rpb[j])
            h = matmul_gated_residual(o, na_w_o_b, j, h, ml[5:6])
            if not last:
                oc = ctx_attention(qkv_c, NA_HEADS, lambda hd: hd, lambda hd: NA_HEADS + hd,
                                   lambda hd: 2 * NA_HEADS + hd)
                hc = matmul_gated_residual(oc, na_w_o_b, j, hc, mc[5:6])
        elif kind == 1:
            h, hc = mamba2_bidirectional(h, hc, g_mix, ml, mc, ssm_w_in, ssm_w_in_b, ssm_w_out_b, j, ssm_conv_w[j],
                                         ssm_conv_b[j], ssm_a_log[j], ssm_dt_bias[j], ssm_d[j], ssm_norm_g[j])
        else:
            qkv = gqa_projection(h, g_mix, ml[3:4], ml[4:5], gqa_w_qkv_b, j, gqa_q_norm[j], gqa_k_norm[j],
                                 cos_full, sin_signed, True, gqa_q_scale)
            qkv_c = gqa_projection(hc, g_mix, mc[3:4], mc[4:5], gqa_w_qkv_b, j, gqa_q_norm[j], gqa_k_norm[j],
                                   rope_off, rope_off, False, 1.0)
            o = gqa_attention(qkv, qkv_c)
            h = matmul_gated_residual(o, gqa_w_o_b, j, h, ml[5:6])
            if not last:
                oc = ctx_attention(qkv_c, GQA_HEADS, lambda hd: hd, lambda hd: GQA_HEADS + hd // GQA_GROUP,
                                   lambda hd: GQA_HEADS + GQA_KV_HEADS + hd // GQA_GROUP)
                hc = matmul_gated_residual(oc, gqa_w_o_b, j, hc, mc[5:6])
        h = ffn(h, ml, i, 1, final_norm_g[None] if last else None)
        if not last:
            hc = ffn(hc, mc, i, 1)
    return h[None]
```

```python
import functools

import jax
import jax.numpy as jnp
import numpy as np
from jax import lax
from jax.experimental import pallas as pl
from jax.experimental.pallas import tpu as pltpu

F32 = jnp.float32
BF16 = jnp.bfloat16

EPS = 1e-6
HALF_STEP = 0.5
N_MOD = 9
GRID_W = 64
LANES = 128

NA_HEADS = 16
NA_WIN_ROWS = 8
NA_WIN_COLS = 16
NA_Q_ROWS = 4
NA_BAND_ROWS = 12
NA_BLOCKS_PER_STEP = 4

SSM_HEADS = 64
SSM_HEAD_DIM = 64
SSM_GROUPS = 8
SSM_STATE = 128
SSM_CHUNK = 128
SSM_D_INNER = SSM_HEADS * SSM_HEAD_DIM
SSM_GN = SSM_GROUPS * SSM_STATE
SSM_CONV_W = 5
SSM_CONV_DIM = SSM_D_INNER + 2 * SSM_GN
SSM_HEADS_PER_GROUP = SSM_HEADS // SSM_GROUPS
SSD_CHUNKS_PER_STEP = 4
CONV_HALO = 16

GQA_HEADS = 16
GQA_KV_HEADS = 4
GQA_GROUP = GQA_HEADS // GQA_KV_HEADS
ROPE_THETA = 10000.0
LOG2_E = 1.4426950408889634

MASK_VALUE = -1e30
VMEM_LIMIT = 56 * 1024 * 1024


def _params(*sem):
    return pltpu.CompilerParams(dimension_semantics=sem, vmem_limit_bytes=VMEM_LIMIT)


def _silu(x):
    return x / (1.0 + jnp.exp(-x))


def _dot(a, b):
    return jnp.dot(a, b, preferred_element_type=F32)


def _dot_nt(a, b):
    return lax.dot_general(a, b, (((1,), (1,)), ((), ())), preferred_element_type=F32)


def _dot_tn(a, b):
    return lax.dot_general(a, b, (((0,), (0,)), ((), ())), preferred_element_type=F32)


def _norm_modulate(x, g, shift, scale):
    var = jnp.mean(x * x, axis=-1, keepdims=True)
    return (x * lax.rsqrt(var + EPS)) * g * (1.0 + scale) + shift


def _adaln_kernel(c_ref, w_ref, b_ref, o_ref, s_sc, *, rows, group):
    tk, tn = w_ref.shape
    s_sc[...] = _silu(c_ref[...])

    @pl.when(pl.program_id(2) == 0)
    def _():
        o_ref[...] = jnp.broadcast_to(b_ref[...], o_ref.shape)

    for lg in range(tn // group):
        cols = slice(lg * group, (lg + 1) * group)

        def body(i, carry):
            a0, a1 = carry
            r = pl.multiple_of(i * rows, rows)
            w = w_ref[pl.ds(r, rows), cols]
            s0 = jnp.concatenate([s_sc[0, pl.ds(r, rows), :]] * (group // LANES), axis=1)
            s1 = jnp.concatenate([s_sc[1, pl.ds(r, rows), :]] * (group // LANES), axis=1)
            return a0 + w * s0, a1 + w * s1

        zero = jnp.zeros((rows, group), F32)
        a0, a1 = lax.fori_loop(0, tk // rows, body, (zero, zero), unroll=2)
        o_ref[0:1, cols] += jnp.sum(a0, axis=0, keepdims=True)
        o_ref[1:2, cols] += jnp.sum(a1, axis=0, keepdims=True)


def adaln_mod(c, c_ctx, ada_w, ada_b):
    depth, d, n = ada_w.shape
    tk = 256
    tn = n // 4 if n % 2048 == 0 else 512
    cb = jnp.broadcast_to(jnp.stack([c[0], c_ctx])[:, :, None], (2, d, LANES))
    out = pl.pallas_call(
        functools.partial(_adaln_kernel, rows=16, group=512),
        grid=(depth, n // tn, d // tk),
        in_specs=[
            pl.BlockSpec((2, tk, LANES), lambda l, j, k: (0, k, 0)),
            pl.BlockSpec((None, tk, tn), lambda l, j, k: (l, k, j)),
            pl.BlockSpec((None, 1, tn), lambda l, j, k: (l, 0, j)),
        ],
        out_specs=pl.BlockSpec((None, 2, tn), lambda l, j, k: (l, 0, j)),
        out_shape=jax.ShapeDtypeStruct((depth, 2, n), F32),
        scratch_shapes=[pltpu.VMEM((2, tk, LANES), F32)],
        compiler_params=_params("parallel", "parallel", "arbitrary"),
        name="adaln_mod",
    )(cb, ada_w, ada_b.reshape(depth, 1, n))
    return out.reshape(depth, 2, N_MOD, d)


def _ffn_kernel(h_ref, g_ref, sh_ref, sc_ref, gate_ref, wa_ref, wb_ref, wo_ref, fg_ref, o_ref,
                xn_sc, acc_sc, *, final_norm):
    j = pl.program_id(1)

    @pl.when(j == 0)
    def _():
        xn = _norm_modulate(h_ref[...], g_ref[...], sh_ref[...], sc_ref[...])
        xn_sc[...] = xn.astype(BF16)
        acc_sc[...] = jnp.zeros_like(acc_sc)

    xn = xn_sc[...]
    a = _dot(xn, wa_ref[...])
    b = _dot(xn, wb_ref[...])
    hm = (_silu(a) * b).astype(BF16)
    acc_sc[...] += _dot(hm, wo_ref[...])

    @pl.when(j == pl.num_programs(1) - 1)
    def _():
        out = h_ref[...] + (HALF_STEP * gate_ref[...]) * acc_sc[...]
        if final_norm:
            var = jnp.mean(out * out, axis=-1, keepdims=True)
            out = (out * lax.rsqrt(var + EPS)) * fg_ref[...]
        o_ref[...] = out


def ffn_half_step(h, g, shift, scale, gate, w_in, w_out, layer, k, final_g=None):
    m, d = h.shape
    f = w_out.shape[2]
    tm = min(m, 512)
    tf = 512
    nf = f // tf
    row = lambda i, j: (0, 0)
    fg = final_g if final_g is not None else g
    return pl.pallas_call(
        functools.partial(_ffn_kernel, final_norm=final_g is not None),
        grid=(m // tm, nf),
        in_specs=[
            pl.BlockSpec((tm, d), lambda i, j: (i, 0)),
            pl.BlockSpec((1, d), row), pl.BlockSpec((1, d), row),
            pl.BlockSpec((1, d), row), pl.BlockSpec((1, d), row),
            pl.BlockSpec((None, None, d, tf), lambda i, j: (layer, k, 0, j)),
            pl.BlockSpec((None, None, d, tf), lambda i, j: (layer, k, 0, j + nf)),
            pl.BlockSpec((None, None, tf, d), lambda i, j: (layer, k, j, 0)),
            pl.BlockSpec((1, d), row),
        ],
        out_specs=pl.BlockSpec((tm, d), lambda i, j: (i, 0)),
        out_shape=jax.ShapeDtypeStruct((m, d), F32),
        scratch_shapes=[pltpu.VMEM((tm, d), BF16), pltpu.VMEM((tm, d), F32)],
        compiler_params=_params("parallel", "arbitrary"),
        name="ffn_half_step",
    )(h, g, shift, scale, gate, w_in, w_in, w_out, fg)


def _nmm_kernel(h_ref, g_ref, sh_ref, sc_ref, w_ref, o_ref, xn_sc, *, precise):
    @pl.when(pl.program_id(1) == 0)
    def _():
        xn = _norm_modulate(h_ref[...], g_ref[...], sh_ref[...], sc_ref[...])
        xn_sc[...] = xn.astype(xn_sc.dtype)

    if precise:
        y = jnp.dot(xn_sc[...], w_ref[...], preferred_element_type=F32, precision=lax.Precision.HIGHEST)
    else:
        y = _dot(xn_sc[...], w_ref[...])
    o_ref[...] = y.astype(o_ref.dtype)


def norm_mod_matmul(h, g, shift, scale, w, layer, out_dtype, n=None, precise=False):
    m, d = h.shape
    n = w.shape[2] if n is None else n
    tm = min(m, 1024)
    tn = n if n <= 1024 else (1024 if n % 1024 == 0 else 512)
    row = lambda i, j: (0, 0)
    return pl.pallas_call(
        functools.partial(_nmm_kernel, precise=precise),
        grid=(m // tm, n // tn),
        in_specs=[
            pl.BlockSpec((tm, d), lambda i, j: (i, 0)),
            pl.BlockSpec((1, d), row), pl.BlockSpec((1, d), row), pl.BlockSpec((1, d), row),
            pl.BlockSpec((None, d, tn), lambda i, j: (layer, 0, j)),
        ],
        out_specs=pl.BlockSpec((tm, tn), lambda i, j: (i, j)),
        out_shape=jax.ShapeDtypeStruct((m, n), out_dtype),
        scratch_shapes=[pltpu.VMEM((tm, d), F32 if precise else BF16)],
        compiler_params=_params("parallel", "arbitrary"),
        name="norm_mod_matmul",
    )(h, g, shift, scale, w)


def _mgr_kernel(a_ref, w_ref, h_ref, gate_ref, o_ref):
    o_ref[...] = h_ref[...] + gate_ref[...] * _dot(a_ref[...], w_ref[...])


def matmul_gated_residual(a, w, layer, h, gate):
    m, k = a.shape
    n = w.shape[2]
    tm = min(m, 512)
    tn = n if k * n * 2 <= 8 * 1024 * 1024 else min(n, 1024)
    return pl.pallas_call(
        _mgr_kernel,
        grid=(m // tm, n // tn),
        in_specs=[
            pl.BlockSpec((tm, k), lambda i, j: (i, 0)),
            pl.BlockSpec((None, k, tn), lambda i, j: (layer, 0, j)),
            pl.BlockSpec((tm, tn), lambda i, j: (i, j)),
            pl.BlockSpec((1, tn), lambda i, j: (0, j)),
        ],
        out_specs=pl.BlockSpec((tm, tn), lambda i, j: (i, j)),
        out_shape=jax.ShapeDtypeStruct((m, n), F32),
        compiler_params=_params("parallel", "parallel"),
        name="matmul_gated_residual",
    )(a, w, h, gate)


def _ctx_attn_kernel(q_ref, k_ref, v_ref, o_ref):
    q = q_ref[...]
    s = _dot_nt(q, k_ref[...]) * (q.shape[-1] ** -0.5)
    m = jnp.max(s, axis=-1, keepdims=True)
    p = jnp.exp(s - m)
    l = jnp.sum(p, axis=-1, keepdims=True)
    o_ref[...] = (_dot(p.astype(BF16), v_ref[...]) / l).astype(o_ref.dtype)


def ctx_attention(qkv, n_heads, q_blk, k_blk, v_blk):
    n_ctx = qkv.shape[0]
    spec = lambda f: pl.BlockSpec((n_ctx, LANES), lambda h: (0, f(h)))
    return pl.pallas_call(
        _ctx_attn_kernel,
        grid=(n_heads,),
        in_specs=[spec(q_blk), spec(k_blk), spec(v_blk)],
        out_specs=pl.BlockSpec((n_ctx, LANES), lambda h: (0, h)),
        out_shape=jax.ShapeDtypeStruct((n_ctx, n_heads * LANES), BF16),
        compiler_params=_params("parallel"),
        name="ctx_attention",
    )(qkv, qkv, qkv)


def _na_kernel(q_ref, k_ref, v_ref, kc_ref, vc_ref, *rest, n_rows):
    bias_refs, o_ref = rest[:-1], rest[-1]
    nq = NA_Q_ROWS * GRID_W
    n_band = NA_BAND_ROWS * GRID_W
    scale = q_ref.shape[-1] ** -0.5 * LOG2_E
    for blk, bias_ref in enumerate(bias_refs):
        r0 = (pl.program_id(1) * len(bias_refs) + blk) * NA_Q_ROWS
        b0 = jnp.clip(r0 - NA_WIN_ROWS // 2, 0, n_rows - NA_BAND_ROWS)
        start = pl.multiple_of(b0 * GRID_W, GRID_W)
        q = q_ref[blk * nq:(blk + 1) * nq, :]
        s_w = _dot_nt(q, k_ref[pl.ds(start, n_band), :]) * scale + bias_ref[0, 0]
        s_c = _dot_nt(q, kc_ref[...]) * scale
        m = jnp.maximum(jnp.max(s_w, axis=-1, keepdims=True), jnp.max(s_c, axis=-1, keepdims=True))
        p_w = jnp.exp2(s_w - m)
        p_c = jnp.exp2(s_c - m)
        l = jnp.sum(p_w, axis=-1, keepdims=True) + jnp.sum(p_c, axis=-1, keepdims=True)
        o = _dot(p_w.astype(BF16), v_ref[pl.ds(start, n_band), :]) + _dot(p_c.astype(BF16), vc_ref[...])
        o_ref[blk * nq:(blk + 1) * nq, :] = (o / l).astype(o_ref.dtype)


def _na_block_geometry(n_rows):
    nq, nk = NA_Q_ROWS * GRID_W, NA_BAND_ROWS * GRID_W
    valid, drow = [], []
    for r0 in (0, NA_Q_ROWS, n_rows - NA_Q_ROWS):
        b0 = int(np.clip(r0 - NA_WIN_ROWS // 2, 0, n_rows - NA_BAND_ROWS))
        qr = r0 + np.arange(nq) // GRID_W
        qc = np.arange(nq) % GRID_W
        kr = b0 + np.arange(nk) // GRID_W
        kc = np.arange(nk) % GRID_W
        rs = np.clip(qr - NA_WIN_ROWS // 2, 0, n_rows - NA_WIN_ROWS)
        cs = np.clip(qc - NA_WIN_COLS // 2, 0, GRID_W - NA_WIN_COLS)
        ok = ((kr[None] >= rs[:, None]) & (kr[None] < rs[:, None] + NA_WIN_ROWS)
              & (kc[None] >= cs[:, None]) & (kc[None] < cs[:, None] + NA_WIN_COLS))
        valid.append(ok)
        drow.append((b0 + np.arange(NA_BAND_ROWS))[None, :] - (r0 + np.arange(NA_Q_ROWS))[:, None])
    return np.stack(valid), np.stack(drow)


def _na_bias_table(rpb, n_rows):
    n_heads = rpb.shape[0]
    valid, drow = _na_block_geometry(n_rows)
    dcol = np.arange(GRID_W)[None, :] - np.arange(GRID_W)[:, None] + NA_WIN_COLS - 1
    in_table = (dcol >= 0) & (dcol < 2 * NA_WIN_COLS - 1)
    onehot = (np.clip(dcol, 0, 2 * NA_WIN_COLS - 2)[None] == np.arange(2 * NA_WIN_COLS - 1)[:, None, None]) & in_table
    toeplitz = jnp.einsum("hrd,dqk->hrqk", rpb, jnp.asarray(onehot, F32), precision=lax.Precision.HIGHEST)
    n_dr = 2 * NA_WIN_ROWS - 1
    dr = drow + NA_WIN_ROWS - 1
    dr = np.where((dr >= 0) & (dr < n_dr), dr, n_dr).reshape(-1)
    padded = jnp.concatenate([toeplitz, jnp.zeros((n_heads, 1, GRID_W, GRID_W), F32)], axis=1)
    tiles = jnp.take(padded, jnp.asarray(dr, jnp.int32), axis=1)
    tiles = tiles.reshape(n_heads, 3, NA_Q_ROWS, NA_BAND_ROWS, GRID_W, GRID_W)
    table = tiles.transpose(1, 0, 2, 4, 3, 5).reshape(3, n_heads, NA_Q_ROWS * GRID_W, NA_BAND_ROWS * GRID_W)
    return jnp.where(jnp.asarray(valid)[:, None], table, MASK_VALUE)


def neighbourhood_attention(qkv, qkv_c, rpb):
    n_tok = qkv.shape[0]
    n_ctx = qkv_c.shape[0]
    n_rows = n_tok // GRID_W
    n_blocks = n_rows // NA_Q_ROWS
    nb = NA_BLOCKS_PER_STEP
    nq, nk = NA_Q_ROWS * GRID_W, NA_BAND_ROWS * GRID_W
    bias = _na_bias_table(rpb, n_rows) * LOG2_E
    h_ = NA_HEADS

    def bias_spec(blk):
        def index(h, s):
            rb = s * nb + blk
            return (jnp.where(rb == 0, 0, jnp.where(rb == n_blocks - 1, 2, 1)), h, 0, 0)
        return pl.BlockSpec((1, 1, nq, nk), index)

    return pl.pallas_call(
        functools.partial(_na_kernel, n_rows=n_rows),
        grid=(h_, n_blocks // nb),
        in_specs=[
            pl.BlockSpec((nb * nq, LANES), lambda h, s: (s, h)),
            pl.BlockSpec((n_tok, LANES), lambda h, s: (0, h_ + h)),
            pl.BlockSpec((n_tok, LANES), lambda h, s: (0, 2 * h_ + h)),
            pl.BlockSpec((n_ctx, LANES), lambda h, s: (0, h_ + h)),
            pl.BlockSpec((n_ctx, LANES), lambda h, s: (0, 2 * h_ + h)),
        ] + [bias_spec(blk) for blk in range(nb)],
        out_specs=pl.BlockSpec((nb * nq, LANES), lambda h, s: (s, h)),
        out_shape=jax.ShapeDtypeStruct((n_tok, h_ * LANES), BF16),
        compiler_params=_params("parallel", "arbitrary"),
        name="neighbourhood_attention",
    )(qkv, qkv, qkv, qkv_c, qkv_c, *([bias] * nb))


def _gqa_proj_kernel(h_ref, g_ref, sh_ref, sc_ref, w_ref, qn_ref, kn_ref, cos_ref, sin_ref, o_ref, xn_sc,
                     *, rope, q_scale):
    j = pl.program_id(1)
    tn = o_ref.shape[1]
    n_q = GQA_HEADS * LANES // tn

    @pl.when(j == 0)
    def _():
        xn = _norm_modulate(h_ref[...], g_ref[...], sh_ref[...], sc_ref[...])
        xn_sc[...] = xn.astype(BF16)

    y = _dot(xn_sc[...], w_ref[...])

    def normed(w, mult):
        for s in range(tn // LANES):
            cols = slice(s * LANES, (s + 1) * LANES)
            x = y[:, cols]
            var = jnp.mean(x * x, axis=-1, keepdims=True)
            t = (x * lax.rsqrt(var + EPS)) * w
            if rope:
                t = t * cos_ref[...] + pltpu.roll(t, LANES // 2, 1) * sin_ref[...]
            if mult != 1.0:
                t = t * mult
            o_ref[:, cols] = t.astype(o_ref.dtype)

    @pl.when(j < n_q)
    def _():
        normed(qn_ref[...], q_scale)

    @pl.when(j == n_q)
    def _():
        normed(kn_ref[...], 1.0)

    @pl.when(j > n_q)
    def _():
        o_ref[...] = y.astype(o_ref.dtype)


def gqa_projection(h, g, shift, scale, w, layer, q_norm, k_norm, cos_full, sin_signed, rope, q_scale):
    m, d = h.shape
    n = w.shape[2]
    tm = min(m, 1024)
    tn = GQA_KV_HEADS * LANES
    row = lambda i, j: (0, 0)
    head = pl.BlockSpec((1, LANES), row)
    pos = pl.BlockSpec((tm, LANES), lambda i, j: (i, 0))
    return pl.pallas_call(
        functools.partial(_gqa_proj_kernel, rope=rope, q_scale=q_scale),
        grid=(m // tm, n // tn),
        in_specs=[
            pl.BlockSpec((tm, d), lambda i, j: (i, 0)),
            pl.BlockSpec((1, d), row), pl.BlockSpec((1, d), row), pl.BlockSpec((1, d), row),
            pl.BlockSpec((None, d, tn), lambda i, j: (layer, 0, j)),
            head, head, pos, pos,
        ],
        out_specs=pl.BlockSpec((tm, tn), lambda i, j: (i, j)),
        out_shape=jax.ShapeDtypeStruct((m, n), BF16),
        scratch_shapes=[pltpu.VMEM((tm, d), BF16)],
        compiler_params=_params("parallel", "arbitrary"),
        name="gqa_projection",
    )(h, g, shift, scale, w, q_norm[None], k_norm[None], cos_full, sin_signed)


def _gqa_kernel(q_ref, k_ref, v_ref, kc_ref, vc_ref, o_ref, m_sc, l_sc, acc_sc, *, tk):
    tq = q_ref.shape[0]
    n_tok = k_ref.shape[0]
    q = jnp.concatenate([q_ref[:, g * LANES:(g + 1) * LANES] for g in range(GQA_GROUP)], axis=0)
    m_sc[...] = jnp.full_like(m_sc, -jnp.inf)
    l_sc[...] = jnp.zeros_like(l_sc)
    acc_sc[...] = jnp.zeros_like(acc_sc)

    def update(kb, vb):
        s = _dot_nt(q, kb)
        blocks = [s[:, c * LANES:(c + 1) * LANES] for c in range(kb.shape[0] // LANES)]
        m_old = m_sc[...]
        block_max = jnp.max(functools.reduce(jnp.maximum, blocks), axis=-1, keepdims=True)
        m_new = jnp.maximum(m_old, block_max)
        alpha = jnp.exp2(m_old - m_new)
        p = [jnp.exp2(b - m_new) for b in blocks]
        l_sc[...] = alpha * l_sc[...] + functools.reduce(jnp.add, p)
        acc_sc[...] = alpha * acc_sc[...] + _dot(jnp.concatenate(p, axis=-1).astype(BF16), vb)
        m_sc[...] = m_new

    def body(j, carry):
        r = pl.multiple_of(j * tk, tk)
        update(k_ref[pl.ds(r, tk), :], v_ref[pl.ds(r, tk), :])
        return carry

    lax.fori_loop(0, n_tok // tk, body, 0)
    update(kc_ref[...], vc_ref[...])
    o = acc_sc[...] / jnp.sum(l_sc[...], axis=-1, keepdims=True)
    for g in range(GQA_GROUP):
        o_ref[:, g * LANES:(g + 1) * LANES] = o[g * tq:(g + 1) * tq].astype(o_ref.dtype)


def gqa_attention(qkv, qkv_c):
    n_tok = qkv.shape[0]
    n_ctx = qkv_c.shape[0]
    tq = min(n_tok, 512)
    tk = min(n_tok, 1024)
    kv0 = GQA_HEADS
    v0 = GQA_HEADS + GQA_KV_HEADS
    rows = GQA_GROUP * tq
    return pl.pallas_call(
        functools.partial(_gqa_kernel, tk=tk),
        grid=(GQA_KV_HEADS, n_tok // tq),
        in_specs=[
            pl.BlockSpec((tq, GQA_GROUP * LANES), lambda kv, i: (i, kv)),
            pl.BlockSpec((n_tok, LANES), lambda kv, i: (0, kv0 + kv)),
            pl.BlockSpec((n_tok, LANES), lambda kv, i: (0, v0 + kv)),
            pl.BlockSpec((n_ctx, LANES), lambda kv, i: (0, kv0 + kv)),
            pl.BlockSpec((n_ctx, LANES), lambda kv, i: (0, v0 + kv)),
        ],
        out_specs=pl.BlockSpec((tq, GQA_GROUP * LANES), lambda kv, i: (i, kv)),
        out_shape=jax.ShapeDtypeStruct((n_tok, GQA_HEADS * LANES), BF16),
        scratch_shapes=[pltpu.VMEM((rows, LANES), F32)] * 3,
        compiler_params=_params("parallel", "arbitrary"),
        name="gqa_attention",
    )(qkv, qkv, qkv, qkv_c, qkv_c)


def axial_rope_tables(n_tok):
    t = jnp.arange(n_tok, dtype=jnp.int32)
    row = (t // GRID_W).astype(F32)
    col = (t % GRID_W).astype(F32)
    n_freq = LANES // 4
    inv_freq = ROPE_THETA ** (-jnp.arange(n_freq, dtype=F32) / n_freq)
    ang = jnp.concatenate([row[:, None] * inv_freq, col[:, None] * inv_freq], axis=-1)
    cos, sin = jnp.cos(ang), jnp.sin(ang)
    return jnp.concatenate([cos, cos], axis=-1), jnp.concatenate([-sin, sin], axis=-1)


def _conv_silu_kernel(prev_ref, cur_ref, next_ref, w_ref, b_ref, o_ref):
    i = pl.program_id(0)
    tl = cur_ref.shape[0]
    prev = jnp.where(i == 0, 0.0, prev_ref[...].astype(F32))
    nxt = jnp.where(i == pl.num_programs(0) - 1, 0.0, next_ref[...].astype(F32))
    ext = jnp.concatenate([prev, cur_ref[...].astype(F32), nxt], axis=0)
    n_ext = ext.shape[0]
    pad = (SSM_CONV_W - 1) // 2
    acc = jnp.zeros((tl, ext.shape[1]), F32) + b_ref[...]
    for k in range(SSM_CONV_W):
        shifted = ext if k == pad else pltpu.roll(ext, (pad - k) % n_ext, 0)
        acc = acc + shifted[CONV_HALO:CONV_HALO + tl] * w_ref[k:k + 1, :]
    o_ref[...] = _silu(acc).astype(o_ref.dtype)


def conv_silu(zxbc, conv_w, conv_b):
    n = zxbc.shape[0]
    tl = min(n, 512)
    tc = 512
    c0 = SSM_D_INNER // tc
    hb = tl // CONV_HALO
    n_halo = n // CONV_HALO
    return pl.pallas_call(
        _conv_silu_kernel,
        grid=(n // tl, SSM_CONV_DIM // tc),
        in_specs=[
            pl.BlockSpec((CONV_HALO, tc), lambda i, j: (jnp.maximum(i * hb - 1, 0), c0 + j)),
            pl.BlockSpec((tl, tc), lambda i, j: (i, c0 + j)),
            pl.BlockSpec((CONV_HALO, tc), lambda i, j: (jnp.minimum((i + 1) * hb, n_halo - 1), c0 + j)),
            pl.BlockSpec((SSM_CONV_W, tc), lambda i, j: (0, j)),
            pl.BlockSpec((1, tc), lambda i, j: (0, j)),
        ],
        out_specs=pl.BlockSpec((tl, tc), lambda i, j: (i, j)),
        out_shape=jax.ShapeDtypeStruct((n, SSM_CONV_DIM), BF16),
        compiler_params=_params("parallel", "parallel"),
        name="conv_silu",
    )(zxbc, zxbc, zxbc, conv_w, conv_b)


def _ssd_prep_kernel(dt_ref, bias_ref, alog_ref, delta_ref, acs_ref):
    x = dt_ref[...] + bias_ref[...]
    delta = jnp.maximum(x, 0.0) + jnp.log1p(jnp.exp(-jnp.abs(x)))
    da = delta * (-jnp.exp(alog_ref[...]))
    q = x.shape[0]
    i = lax.broadcasted_iota(jnp.int32, (q, q), 0)
    j = lax.broadcasted_iota(jnp.int32, (q, q), 1)
    hi = lax.Precision.HIGHEST
    fwd = jnp.dot((i >= j).astype(F32), da, preferred_element_type=F32, precision=hi)
    bwd = jnp.dot((i <= j).astype(F32), da, preferred_element_type=F32, precision=hi)
    lane = lax.broadcasted_iota(jnp.int32, da.shape, 1)
    delta_ref[...] = delta
    acs_ref[...] = jnp.where(lane < SSM_HEADS, fwd, bwd)


def ssd_prep(dt, dt_bias, a_log):
    n = dt.shape[0]
    q = SSM_CHUNK
    blk = pl.BlockSpec((q, 2 * SSM_HEADS), lambda c: (c, 0))
    row = pl.BlockSpec((1, 2 * SSM_HEADS), lambda c: (0, 0))
    return pl.pallas_call(
        _ssd_prep_kernel,
        grid=(n // q,),
        in_specs=[blk, row, row],
        out_specs=[blk, blk],
        out_shape=[jax.ShapeDtypeStruct(dt.shape, F32)] * 2,
        compiler_params=_params("parallel"),
        name="ssd_prep",
    )(dt, dt_bias.reshape(1, -1), a_log.reshape(1, -1))


def _ssd_kernel(x_ref, b_ref, c_ref, ac_ref, ar_ref, dr_ref, init_ref, y_ref, fin_ref, state_sc, *, n_sub):
    d = pl.program_id(0)
    q = SSM_CHUNK

    @pl.when(pl.program_id(2) == 0)
    def _():
        state_sc[...] = init_ref[0]

    i = lax.broadcasted_iota(jnp.int32, (q, q), 0)
    j = lax.broadcasted_iota(jnp.int32, (q, q), 1)
    causal = jnp.where(d == 0, i - j, j - i) >= 0
    lo = lax.broadcasted_iota(jnp.int32, (q, LANES), 1) < SSM_HEAD_DIM

    def chunk(k, carry):
        ci = k + d * (n_sub - 1 - 2 * k)
        r = pl.multiple_of(ci * q, q)
        bm = b_ref[pl.ds(r, q), :]
        cm = c_ref[pl.ds(r, q), :]
        ac = ac_ref[0, 0, pl.ds(r, q), :]
        ar = ar_ref[0, 0, ci]
        dr = dr_ref[0, 0, ci]
        tot = jnp.where(d == 0, ar[:, q - 1:q], ar[:, 0:1])
        w_out = jnp.exp(tot - ar) * dr
        e_tot = jnp.exp(tot)
        cb = _dot_nt(cm, bm)
        bt = bm.astype(F32).T
        cf = cm.astype(F32)
        state = state_sc[...]
        for p in range(SSM_HEADS_PER_GROUP // 2):
            cols = slice(p * LANES, (p + 1) * LANES)
            x = x_ref[pl.ds(r, q), cols]
            lhs_y, lhs_s = [], []
            for hd in (2 * p, 2 * p + 1):
                a_i = jnp.broadcast_to(ac[:, hd:hd + 1], (q, q))
                seg = jnp.where(causal, a_i - ar[hd:hd + 1, :], -jnp.inf)
                intra = cb * jnp.exp(seg) * dr[hd:hd + 1, :]
                lhs_y.append(jnp.concatenate([intra, cf * jnp.exp(a_i)], axis=1).astype(BF16))
                lhs_s.append((bt * w_out[hd:hd + 1, :]).astype(BF16))
            rhs = jnp.concatenate([x, state[:, cols].astype(BF16)], axis=0)
            res = _dot(jnp.concatenate(lhs_y, axis=0), rhs)
            y_ref[0, pl.ds(r, q), cols] = jnp.where(lo, res[:q], res[q:]).astype(y_ref.dtype)
            s_res = _dot(jnp.concatenate(lhs_s, axis=0), x)
            s_new = jnp.where(lo, s_res[:SSM_STATE], s_res[SSM_STATE:])
            keep = jnp.where(lo[0:1], e_tot[2 * p:2 * p + 1, :], e_tot[2 * p + 1:2 * p + 2, :])
            state_sc[:, cols] = state[:, cols] * keep + s_new
        return carry

    lax.fori_loop(0, n_sub, chunk, 0, unroll=True)

    @pl.when(pl.program_id(2) == pl.num_programs(2) - 1)
    def _():
        fin_ref[0] = state_sc[...]


def ssd_scan(xbc, delta, acs, init_state):
    assert SSM_STATE == SSM_CHUNK == LANES
    n = xbc.shape[0]
    q = SSM_CHUNK
    nc = n // q
    n_sub = min(nc, SSD_CHUNKS_PER_STEP)
    ns = nc // n_sub
    rows = n_sub * q
    hg = SSM_HEADS_PER_GROUP
    gw = hg * SSM_HEAD_DIM
    ac_col = acs.reshape(n, 2, SSM_GROUPS, hg).transpose(1, 2, 0, 3)
    row = lambda t: t.reshape(nc, q, 2, SSM_GROUPS, hg).transpose(2, 3, 0, 4, 1)
    ac_row, dl_row = row(acs), row(delta)
    b0 = SSM_D_INNER // SSM_STATE
    c0 = b0 + SSM_GROUPS
    step = lambda d, s: s + d * (ns - 1 - 2 * s)
    row_spec = pl.BlockSpec((1, 1, n_sub, hg, q), lambda d, g, s: (d, g, step(d, s), 0, 0))
    state_spec = pl.BlockSpec((1, SSM_STATE, gw), lambda d, g, s: (d, 0, g))
    return pl.pallas_call(
        functools.partial(_ssd_kernel, n_sub=n_sub),
        grid=(2, SSM_GROUPS, ns),
        in_specs=[
            pl.BlockSpec((rows, gw), lambda d, g, s: (step(d, s), g)),
            pl.BlockSpec((rows, SSM_STATE), lambda d, g, s: (step(d, s), b0 + g)),
            pl.BlockSpec((rows, SSM_STATE), lambda d, g, s: (step(d, s), c0 + g)),
            pl.BlockSpec((1, 1, rows, hg), lambda d, g, s: (d, g, step(d, s), 0)),
            row_spec, row_spec,
            state_spec,
        ],
        out_specs=[pl.BlockSpec((1, rows, gw), lambda d, g, s: (d, step(d, s), g)), state_spec],
        out_shape=[jax.ShapeDtypeStruct((2, n, SSM_D_INNER), BF16),
                   jax.ShapeDtypeStruct((2, SSM_STATE, SSM_D_INNER), F32)],
        scratch_shapes=[pltpu.VMEM((SSM_STATE, gw), F32)],
        compiler_params=_params("parallel", "parallel", "arbitrary"),
        name="ssd_scan",
    )(xbc, xbc, xbc, ac_col, ac_row, dl_row, init_state)


def _ssm_gate_norm_kernel(yf_ref, yb_ref, x_ref, z_ref, dsum_ref, g_ref, o_ref):
    y = yf_ref[0].astype(F32) + yb_ref[0].astype(F32) + dsum_ref[...] * x_ref[...].astype(F32)
    v = y * _silu(z_ref[...].astype(F32))
    var = jnp.mean(v * v, axis=-1, keepdims=True)
    o_ref[...] = ((v * lax.rsqrt(var + EPS)) * g_ref[...]).astype(o_ref.dtype)


def ssm_gate_norm(y, xbc, zxbc, dsum, norm_g):
    n = xbc.shape[0]
    tl = min(n, 256)
    di = SSM_D_INNER
    row = pl.BlockSpec((1, di), lambda i: (0, 0))
    return pl.pallas_call(
        _ssm_gate_norm_kernel,
        grid=(n // tl,),
        in_specs=[
            pl.BlockSpec((1, tl, di), lambda i: (0, i, 0)),
            pl.BlockSpec((1, tl, di), lambda i: (1, i, 0)),
            pl.BlockSpec((tl, di), lambda i: (i, 0)),
            pl.BlockSpec((tl, di), lambda i: (i, 0)),
            row, row,
        ],
        out_specs=pl.BlockSpec((tl, di), lambda i: (i, 0)),
        out_shape=jax.ShapeDtypeStruct((n, di), BF16),
        compiler_params=_params("parallel"),
        name="ssm_gate_norm",
    )(y, y, xbc, zxbc, dsum, norm_g)


def mamba2_bidirectional(h, hc, g, mod_l, mod_c, w_in, w_in_bf16, w_out_bf16, layer, conv_w, conv_b, a_log, dt_bias,
                         d_skip, norm_g):
    n_main = SSM_D_INNER + SSM_CONV_DIM
    w_dt = w_in[layer, :, n_main:][None]
    dsum = jnp.repeat(d_skip[0] + d_skip[1], SSM_HEAD_DIM)[None]
    state = jnp.zeros((2, SSM_STATE, SSM_D_INNER), F32)
    outs = []
    for t, mod in ((hc, mod_c), (h, mod_l)):
        shift, scale, gate = mod[3:4], mod[4:5], mod[5:6]
        zxbc = norm_mod_matmul(t, g, shift, scale, w_in_bf16, layer, BF16, n=n_main)
        dt = norm_mod_matmul(t, g, shift, scale, w_dt, 0, F32, precise=True)
        xbc = conv_silu(zxbc, conv_w, conv_b[None])
        delta, acs = ssd_prep(dt, dt_bias, a_log)
        y, state = ssd_scan(xbc, delta, acs, state)
        yn = ssm_gate_norm(y, xbc, zxbc, dsum, norm_g[None])
        outs.append(matmul_gated_residual(yn, w_out_bf16, layer, t, gate))
    return outs[1], outs[0]


def kernel(x, c, ctx, c_ctx, ada_w, ada_b, norm_g, ffn_w_in, ffn_w_out, na_w_qkv, na_rpb, na_w_o, ssm_w_in, ssm_conv_w, ssm_conv_b, ssm_a_log, ssm_dt_bias, ssm_d, ssm_norm_g, ssm_w_out, gqa_w_qkv, gqa_q_norm, gqa_k_norm, gqa_w_o, final_norm_g):
    bsz, n_tok, d = x.shape
    assert bsz == 1, "the kernels take one sequence"
    depth = ada_w.shape[0]
    h, hc = x[0], ctx[0]
    mod = adaln_mod(c, c_ctx, ada_w, ada_b)
    cos_full, sin_signed = axial_rope_tables(n_tok)
    rope_off = jnp.zeros((ctx.shape[1], LANES), F32)
    ffn_w_in_b, ffn_w_out_b = ffn_w_in.astype(BF16), ffn_w_out.astype(BF16)
    na_w_qkv_b, na_w_o_b = na_w_qkv.astype(BF16), na_w_o.astype(BF16)
    ssm_w_in_b, ssm_w_out_b = ssm_w_in.astype(BF16), ssm_w_out.astype(BF16)
    gqa_w_qkv_b, gqa_w_o_b = gqa_w_qkv.astype(BF16), gqa_w_o.astype(BF16)
    gqa_q_scale = LANES ** -0.5 * LOG2_E

    def ffn(t, m, i, k, final_g=None):
        return ffn_half_step(t, norm_g[i, 2 * k][None], m[6 * k:6 * k + 1], m[6 * k + 1:6 * k + 2],
                             m[6 * k + 2:6 * k + 3], ffn_w_in_b, ffn_w_out_b, i, k, final_g)

    for i in range(depth):
        last = i == depth - 1
        ml, mc = mod[i, 0], mod[i, 1]
        g_mix = norm_g[i, 1][None]
        h = ffn(h, ml, i, 0)
        hc = ffn(hc, mc, i, 0)
        kind, j = i % 3, i // 3
        if kind == 0:
            qkv = norm_mod_matmul(h, g_mix, ml[3:4], ml[4:5], na_w_qkv_b, j, BF16)
            qkv_c = norm_mod_matmul(hc, g_mix, mc[3:4], mc[4:5], na_w_qkv_b, j, BF16)
            o = neighbourhood_attention(qkv, qkv_c, na_rpb[j])
            h = matmul_gated_residual(o, na_w_o_b, j, h, ml[5:6])
            if not last:
                oc = ctx_attention(qkv_c, NA_HEADS, lambda hd: hd, lambda hd: NA_HEADS + hd,
                                   lambda hd: 2 * NA_HEADS + hd)
                hc = matmul_gated_residual(oc, na_w_o_b, j, hc, mc[5:6])
        elif kind == 1:
            h, hc = mamba2_bidirectional(h, hc, g_mix, ml, mc, ssm_w_in, ssm_w_in_b, ssm_w_out_b, j, ssm_conv_w[j],
                                         ssm_conv_b[j], ssm_a_log[j], ssm_dt_bias[j], ssm_d[j], ssm_norm_g[j])
        else:
            qkv = gqa_projection(h, g_mix, ml[3:4], ml[4:5], gqa_w_qkv_b, j, gqa_q_norm[j], gqa_k_norm[j],
                                 cos_full, sin_signed, True, gqa_q_scale)
            qkv_c = gqa_projection(hc, g_mix, mc[3:4], mc[4:5], gqa_w_qkv_b, j, gqa_q_norm[j], gqa_k_norm[j],
                                   rope_off, rope_off, False, 1.0)
            o = gqa_attention(qkv, qkv_c)
            h = matmul_gated_residual(o, gqa_w_o_b, j, h, ml[5:6])
            if not last:
                oc = ctx_attention(qkv_c, GQA_HEADS, lambda hd: hd, lambda hd: GQA_HEADS + hd // GQA_GROUP,
                                   lambda hd: GQA_HEADS + GQA_KV_HEADS + hd // GQA_GROUP)
                hc = matmul_gated_residual(oc, gqa_w_o_b, j, hc, mc[5:6])
        h = ffn(h, ml, i, 1, final_norm_g[None] if last else None)
        if not last:
            hc = ffn(hc, mc, i, 1)
    return h[None]
```

```python
import functools

import jax
import jax.numpy as jnp
import numpy as np
from jax import lax
from jax.experimental import pallas as pl
from jax.experimental.pallas import tpu as pltpu

F32 = jnp.float32
BF16 = jnp.bfloat16

EPS = 1e-6
HALF_STEP = 0.5
N_MOD = 9
GRID_W = 64
LANES = 128

NA_HEADS = 16
NA_WIN_ROWS = 8
NA_WIN_COLS = 16
NA_Q_ROWS = 4
NA_BAND_ROWS = 12
NA_BLOCKS_PER_STEP = 4

SSM_HEADS = 64
SSM_HEAD_DIM = 64
SSM_GROUPS = 8
SSM_STATE = 128
SSM_CHUNK = 128
SSM_D_INNER = SSM_HEADS * SSM_HEAD_DIM
SSM_GN = SSM_GROUPS * SSM_STATE
SSM_CONV_W = 5
SSM_CONV_DIM = SSM_D_INNER + 2 * SSM_GN
SSM_HEADS_PER_GROUP = SSM_HEADS // SSM_GROUPS
SSD_CHUNKS_PER_STEP = 4
CONV_HALO = 16

GQA_HEADS = 16
GQA_KV_HEADS = 4
GQA_GROUP = GQA_HEADS // GQA_KV_HEADS
ROPE_THETA = 10000.0
LOG2_E = 1.4426950408889634

MASK_VALUE = -1e30
VMEM_LIMIT = 56 * 1024 * 1024


def _params(*sem):
    return pltpu.CompilerParams(dimension_semantics=sem, vmem_limit_bytes=VMEM_LIMIT)


def _silu(x):
    return x / (1.0 + jnp.exp(-x))


def _dot(a, b):
    return jnp.dot(a, b, preferred_element_type=F32)


def _dot_nt(a, b):
    return lax.dot_general(a, b, (((1,), (1,)), ((), ())), preferred_element_type=F32)


def _dot_tn(a, b):
    return lax.dot_general(a, b, (((0,), (0,)), ((), ())), preferred_element_type=F32)


def _norm_modulate(x, g, shift, scale):
    var = jnp.mean(x * x, axis=-1, keepdims=True)
    return (x * lax.rsqrt(var + EPS)) * g * (1.0 + scale) + shift


def _adaln_kernel(c_ref, w_ref, b_ref, o_ref, s_sc, *, rows, group):
    tk, tn = w_ref.shape
    s_sc[...] = _silu(c_ref[...])

    @pl.when(pl.program_id(2) == 0)
    def _():
        o_ref[...] = jnp.broadcast_to(b_ref[...], o_ref.shape)

    for lg in range(tn // group):
        cols = slice(lg * group, (lg + 1) * group)

        def body(i, carry):
            a0, a1 = carry
            r = pl.multiple_of(i * rows, rows)
            w = w_ref[pl.ds(r, rows), cols]
            s0 = jnp.concatenate([s_sc[0, pl.ds(r, rows), :]] * (group // LANES), axis=1)
            s1 = jnp.concatenate([s_sc[1, pl.ds(r, rows), :]] * (group // LANES), axis=1)
            return a0 + w * s0, a1 + w * s1

        zero = jnp.zeros((rows, group), F32)
        a0, a1 = lax.fori_loop(0, tk // rows, body, (zero, zero), unroll=2)
        o_ref[0:1, cols] += jnp.sum(a0, axis=0, keepdims=True)
        o_ref[1:2, cols] += jnp.sum(a1, axis=0, keepdims=True)


def adaln_mod(c, c_ctx, ada_w, ada_b):
    depth, d, n = ada_w.shape
    tk = 256
    tn = n // 4 if n % 2048 == 0 else 512
    cb = jnp.broadcast_to(jnp.stack([c[0], c_ctx])[:, :, None], (2, d, LANES))
    out = pl.pallas_call(
        functools.partial(_adaln_kernel, rows=16, group=512),
        grid=(depth, n // tn, d // tk),
        in_specs=[
            pl.BlockSpec((2, tk, LANES), lambda l, j, k: (0, k, 0)),
            pl.BlockSpec((None, tk, tn), lambda l, j, k: (l, k, j)),
            pl.BlockSpec((None, 1, tn), lambda l, j, k: (l, 0, j)),
        ],
        out_specs=pl.BlockSpec((None, 2, tn), lambda l, j, k: (l, 0, j)),
        out_shape=jax.ShapeDtypeStruct((depth, 2, n), F32),
        scratch_shapes=[pltpu.VMEM((2, tk, LANES), F32)],
        compiler_params=_params("parallel", "parallel", "arbitrary"),
        name="adaln_mod",
    )(cb, ada_w, ada_b.reshape(depth, 1, n))
    return out.reshape(depth, 2, N_MOD, d)


def _ffn_kernel(h_ref, g_ref, sh_ref, sc_ref, gate_ref, wa_ref, wb_ref, wo_ref, fg_ref, *rest, final_norm, cast_next):
    if cast_next:
        nwi_ref, nwo_ref, o_ref, nwi_out, nwo_out, xn_sc, acc_sc = rest
        nwi_out[...] = nwi_ref[...].astype(BF16)
        nwo_out[...] = nwo_ref[...].astype(BF16)
    else:
        o_ref, xn_sc, acc_sc = rest
    j = pl.program_id(1)

    @pl.when(j == 0)
    def _():
        xn = _norm_modulate(h_ref[...], g_ref[...], sh_ref[...], sc_ref[...])
        xn_sc[...] = xn.astype(BF16)
        acc_sc[...] = jnp.zeros_like(acc_sc)

    xn = xn_sc[...]
    a = _dot(xn, wa_ref[...])
    b = _dot(xn, wb_ref[...])
    hm = (_silu(a) * b).astype(BF16)
    acc_sc[...] += _dot(hm, wo_ref[...])

    @pl.when(j == pl.num_programs(1) - 1)
    def _():
        out = h_ref[...] + (HALF_STEP * gate_ref[...]) * acc_sc[...]
        if final_norm:
            var = jnp.mean(out * out, axis=-1, keepdims=True)
            out = (out * lax.rsqrt(var + EPS)) * fg_ref[...]
        o_ref[...] = out


def ffn_half_step(h, g, shift, scale, gate, w_in, w_out, next_w=None, final_g=None):
    m, d = h.shape
    f = w_out.shape[0]
    tm = min(m, 512)
    tf = 512
    nf = f // tf
    n_i = m // tm
    row = lambda i, j: (0, 0)
    fg = final_g if final_g is not None else g
    in_specs = [
        pl.BlockSpec((tm, d), lambda i, j: (i, 0)),
        pl.BlockSpec((1, d), row), pl.BlockSpec((1, d), row),
        pl.BlockSpec((1, d), row), pl.BlockSpec((1, d), row),
        pl.BlockSpec((d, tf), lambda i, j: (0, j)),
        pl.BlockSpec((d, tf), lambda i, j: (0, j + nf)),
        pl.BlockSpec((tf, d), lambda i, j: (j, 0)),
        pl.BlockSpec((1, d), row),
    ]
    out_specs = [pl.BlockSpec((tm, d), lambda i, j: (i, 0))]
    out_shape = [jax.ShapeDtypeStruct((m, d), F32)]
    args = [h, g, shift, scale, gate, w_in, w_in, w_out, fg]
    if next_w is not None:
        nw_in, nw_out, layer, k = next_w
        ri, ci, ro = d // n_i, 2 * f // nf, f // (n_i * nf)
        assert ri % 16 == 0 and ci % LANES == 0 and ro % 16 == 0
        in_specs += [pl.BlockSpec((None, None, ri, ci), lambda i, j: (layer, k, i, j)),
                     pl.BlockSpec((None, None, ro, d), lambda i, j: (layer, k, i * nf + j, 0))]
        out_specs += [pl.BlockSpec((ri, ci), lambda i, j: (i, j)),
                      pl.BlockSpec((ro, d), lambda i, j: (i * nf + j, 0))]
        out_shape += [jax.ShapeDtypeStruct((d, 2 * f), BF16), jax.ShapeDtypeStruct((f, d), BF16)]
        args += [nw_in, nw_out]
    outs = pl.pallas_call(
        functools.partial(_ffn_kernel, final_norm=final_g is not None, cast_next=next_w is not None),
        grid=(n_i, nf),
        in_specs=in_specs,
        out_specs=out_specs,
        out_shape=out_shape,
        scratch_shapes=[pltpu.VMEM((tm, d), BF16), pltpu.VMEM((tm, d), F32)],
        compiler_params=_params("parallel", "arbitrary"),
        name="ffn_half_step",
    )(*args)
    return outs if next_w is not None else outs[0]


def _nmm_kernel(h_ref, g_ref, sh_ref, sc_ref, w_ref, o_ref, xn_sc, *, precise):
    @pl.when(pl.program_id(1) == 0)
    def _():
        xn = _norm_modulate(h_ref[...], g_ref[...], sh_ref[...], sc_ref[...])
        xn_sc[...] = xn.astype(xn_sc.dtype)

    if precise:
        y = jnp.dot(xn_sc[...], w_ref[...], preferred_element_type=F32, precision=lax.Precision.HIGHEST)
    else:
        y = _dot(xn_sc[...], w_ref[...].astype(BF16))
    o_ref[...] = y.astype(o_ref.dtype)


def norm_mod_matmul(h, g, shift, scale, w, layer, out_dtype, n=None, precise=False):
    m, d = h.shape
    n = w.shape[2] if n is None else n
    tm = min(m, 1024)
    tn = n if n <= 1024 else (1024 if n % 1024 == 0 else 512)
    row = lambda i, j: (0, 0)
    return pl.pallas_call(
        functools.partial(_nmm_kernel, precise=precise),
        grid=(m // tm, n // tn),
        in_specs=[
            pl.BlockSpec((tm, d), lambda i, j: (i, 0)),
            pl.BlockSpec((1, d), row), pl.BlockSpec((1, d), row), pl.BlockSpec((1, d), row),
            pl.BlockSpec((None, d, tn), lambda i, j: (layer, 0, j)),
        ],
        out_specs=pl.BlockSpec((tm, tn), lambda i, j: (i, j)),
        out_shape=jax.ShapeDtypeStruct((m, n), out_dtype),
        scratch_shapes=[pltpu.VMEM((tm, d), F32 if precise else BF16)],
        compiler_params=_params("parallel", "arbitrary"),
        name="norm_mod_matmul",
    )(h, g, shift, scale, w)


def _mgr_kernel(a_ref, w_ref, h_ref, gate_ref, o_ref):
    o_ref[...] = h_ref[...] + gate_ref[...] * _dot(a_ref[...], w_ref[...])


def matmul_gated_residual(a, w, layer, h, gate):
    m, k = a.shape
    n = w.shape[2]
    tm = min(m, 512)
    tn = n if k * n * 2 <= 8 * 1024 * 1024 else min(n, 1024)
    return pl.pallas_call(
        _mgr_kernel,
        grid=(m // tm, n // tn),
        in_specs=[
            pl.BlockSpec((tm, k), lambda i, j: (i, 0)),
            pl.BlockSpec((None, k, tn), lambda i, j: (layer, 0, j)),
            pl.BlockSpec((tm, tn), lambda i, j: (i, j)),
            pl.BlockSpec((1, tn), lambda i, j: (0, j)),
        ],
        out_specs=pl.BlockSpec((tm, tn), lambda i, j: (i, j)),
        out_shape=jax.ShapeDtypeStruct((m, n), F32),
        compiler_params=_params("parallel", "parallel"),
        name="matmul_gated_residual",
    )(a, w, h, gate)


def _ctx_attn_kernel(q_ref, k_ref, v_ref, o_ref):
    q = q_ref[...]
    s = _dot_nt(q, k_ref[...]) * (q.shape[-1] ** -0.5)
    m = jnp.max(s, axis=-1, keepdims=True)
    p = jnp.exp(s - m)
    l = jnp.sum(p, axis=-1, keepdims=True)
    o_ref[...] = (_dot(p.astype(BF16), v_ref[...]) / l).astype(o_ref.dtype)


def ctx_attention(qkv, n_heads, q_blk, k_blk, v_blk):
    n_ctx = qkv.shape[0]
    spec = lambda f: pl.BlockSpec((n_ctx, LANES), lambda h: (0, f(h)))
    return pl.pallas_call(
        _ctx_attn_kernel,
        grid=(n_heads,),
        in_specs=[spec(q_blk), spec(k_blk), spec(v_blk)],
        out_specs=pl.BlockSpec((n_ctx, LANES), lambda h: (0, h)),
        out_shape=jax.ShapeDtypeStruct((n_ctx, n_heads * LANES), BF16),
        compiler_params=_params("parallel"),
        name="ctx_attention",
    )(qkv, qkv, qkv)


def _na_kernel(q_ref, k_ref, v_ref, kc_ref, vc_ref, pair_ref, *rest, n_rows):
    mask_refs, o_ref = rest[:-1], rest[-1]
    nq = NA_Q_ROWS * GRID_W
    n_band = NA_BAND_ROWS * GRID_W
    scale = q_ref.shape[-1] ** -0.5 * LOG2_E
    for blk, mask_ref in enumerate(mask_refs):
        r0 = (pl.program_id(1) * len(mask_refs) + blk) * NA_Q_ROWS
        b0 = jnp.clip(r0 - NA_WIN_ROWS // 2, 0, n_rows - NA_BAND_ROWS)
        start = pl.multiple_of(b0 * GRID_W, GRID_W)
        q = q_ref[blk * nq:(blk + 1) * nq, :]
        bias = jnp.concatenate([
            jnp.concatenate([
                pair_ref[0, jnp.clip(b0 + 2 * t - r0 - qi + NA_WIN_ROWS, 0, 2 * NA_WIN_ROWS - 1)]
                for t in range(NA_BAND_ROWS // 2)], axis=1)
            for qi in range(NA_Q_ROWS)], axis=0)
        s_w = _dot_nt(q, k_ref[pl.ds(start, n_band), :]) * scale + (bias + mask_ref[0])
        s_c = _dot_nt(q, kc_ref[...]) * scale
        m = jnp.maximum(jnp.max(s_w, axis=-1, keepdims=True), jnp.max(s_c, axis=-1, keepdims=True))
        p_w = jnp.exp2(s_w - m)
        p_c = jnp.exp2(s_c - m)
        l = jnp.sum(p_w, axis=-1, keepdims=True) + jnp.sum(p_c, axis=-1, keepdims=True)
        o = _dot(p_w.astype(BF16), v_ref[pl.ds(start, n_band), :]) + _dot(p_c.astype(BF16), vc_ref[...])
        o_ref[blk * nq:(blk + 1) * nq, :] = (o / l).astype(o_ref.dtype)


def _na_window_masks(n_rows):
    nq, nk = NA_Q_ROWS * GRID_W, NA_BAND_ROWS * GRID_W
    valid = []
    for r0 in (0, NA_Q_ROWS, n_rows - NA_Q_ROWS):
        b0 = int(np.clip(r0 - NA_WIN_ROWS // 2, 0, n_rows - NA_BAND_ROWS))
        qr = r0 + np.arange(nq) // GRID_W
        qc = np.arange(nq) % GRID_W
        kr = b0 + np.arange(nk) // GRID_W
        kc = np.arange(nk) % GRID_W
        rs = np.clip(qr - NA_WIN_ROWS // 2, 0, n_rows - NA_WIN_ROWS)
        cs = np.clip(qc - NA_WIN_COLS // 2, 0, GRID_W - NA_WIN_COLS)
        ok = ((kr[None] >= rs[:, None]) & (kr[None] < rs[:, None] + NA_WIN_ROWS)
              & (kc[None] >= cs[:, None]) & (kc[None] < cs[:, None] + NA_WIN_COLS))
        valid.append(ok)
    return np.where(np.stack(valid), 0.0, MASK_VALUE).astype(np.float32)


def _na_pair_tiles(rpb):
    n_heads = rpb.shape[0]
    dcol = np.arange(GRID_W)[None, :] - np.arange(GRID_W)[:, None] + NA_WIN_COLS - 1
    in_table = (dcol >= 0) & (dcol < 2 * NA_WIN_COLS - 1)
    onehot = (np.clip(dcol, 0, 2 * NA_WIN_COLS - 2)[None] == np.arange(2 * NA_WIN_COLS - 1)[:, None, None]) & in_table
    toeplitz = jnp.einsum("hrd,dqk->hrqk", rpb, jnp.asarray(onehot, F32), precision=lax.Precision.HIGHEST)
    zero = jnp.zeros((n_heads, 1, GRID_W, GRID_W), F32)
    padded = jnp.concatenate([zero, toeplitz, zero], axis=1)
    return jnp.concatenate([padded[:, :-1], padded[:, 1:]], axis=-1)


def neighbourhood_attention(qkv, qkv_c, rpb):
    n_tok = qkv.shape[0]
    n_ctx = qkv_c.shape[0]
    n_rows = n_tok // GRID_W
    n_blocks = n_rows // NA_Q_ROWS
    nb = NA_BLOCKS_PER_STEP
    nq, nk = NA_Q_ROWS * GRID_W, NA_BAND_ROWS * GRID_W
    assert NA_BAND_ROWS % 2 == 0
    pairs = _na_pair_tiles(rpb) * LOG2_E
    masks = jnp.asarray(_na_window_masks(n_rows))
    h_ = NA_HEADS

    def mask_spec(blk):
        def index(h, s):
            rb = s * nb + blk
            return (jnp.where(rb == 0, 0, jnp.where(rb == n_blocks - 1, 2, 1)), 0, 0)
        return pl.BlockSpec((1, nq, nk), index)

    return pl.pallas_call(
        functools.partial(_na_kernel, n_rows=n_rows),
        grid=(h_, n_blocks // nb),
        in_specs=[
            pl.BlockSpec((nb * nq, LANES), lambda h, s: (s, h)),
            pl.BlockSpec((n_tok, LANES), lambda h, s: (0, h_ + h)),
            pl.BlockSpec((n_tok, LANES), lambda h, s: (0, 2 * h_ + h)),
            pl.BlockSpec((n_ctx, LANES), lambda h, s: (0, h_ + h)),
            pl.BlockSpec((n_ctx, LANES), lambda h, s: (0, 2 * h_ + h)),
            pl.BlockSpec((1,) + pairs.shape[1:], lambda h, s: (h, 0, 0, 0)),
        ] + [mask_spec(blk) for blk in range(nb)],
        out_specs=pl.BlockSpec((nb * nq, LANES), lambda h, s: (s, h)),
        out_shape=jax.ShapeDtypeStruct((n_tok, h_ * LANES), BF16),
        compiler_params=_params("parallel", "arbitrary"),
        name="neighbourhood_attention",
    )(qkv, qkv, qkv, qkv_c, qkv_c, pairs, *([masks] * nb))


def _gqa_proj_kernel(h_ref, g_ref, sh_ref, sc_ref, w_ref, qn_ref, kn_ref, cos_ref, sin_ref, o_ref, xn_sc,
                     *, rope, q_scale):
    j = pl.program_id(1)
    tn = o_ref.shape[1]
    n_q = GQA_HEADS * LANES // tn

    @pl.when(j == 0)
    def _():
        xn = _norm_modulate(h_ref[...], g_ref[...], sh_ref[...], sc_ref[...])
        xn_sc[...] = xn.astype(BF16)

    y = _dot(xn_sc[...], w_ref[...].astype(BF16))

    def normed(w, mult):
        for s in range(tn // LANES):
            cols = slice(s * LANES, (s + 1) * LANES)
            x = y[:, cols]
            var = jnp.mean(x * x, axis=-1, keepdims=True)
            t = (x * lax.rsqrt(var + EPS)) * w
            if rope:
                t = t * cos_ref[...] + pltpu.roll(t, LANES // 2, 1) * sin_ref[...]
            if mult != 1.0:
                t = t * mult
            o_ref[:, cols] = t.astype(o_ref.dtype)

    @pl.when(j < n_q)
    def _():
        normed(qn_ref[...], q_scale)

    @pl.when(j == n_q)
    def _():
        normed(kn_ref[...], 1.0)

    @pl.when(j > n_q)
    def _():
        o_ref[...] = y.astype(o_ref.dtype)


def gqa_projection(h, g, shift, scale, w, layer, q_norm, k_norm, cos_full, sin_signed, rope, q_scale):
    m, d = h.shape
    n = w.shape[2]
    tm = min(m, 1024)
    tn = GQA_KV_HEADS * LANES
    row = lambda i, j: (0, 0)
    head = pl.BlockSpec((1, LANES), row)
    pos = pl.BlockSpec((tm, LANES), lambda i, j: (i, 0))
    return pl.pallas_call(
        functools.partial(_gqa_proj_kernel, rope=rope, q_scale=q_scale),
        grid=(m // tm, n // tn),
        in_specs=[
            pl.BlockSpec((tm, d), lambda i, j: (i, 0)),
            pl.BlockSpec((1, d), row), pl.BlockSpec((1, d), row), pl.BlockSpec((1, d), row),
            pl.BlockSpec((None, d, tn), lambda i, j: (layer, 0, j)),
            head, head, pos, pos,
        ],
        out_specs=pl.BlockSpec((tm, tn), lambda i, j: (i, j)),
        out_shape=jax.ShapeDtypeStruct((m, n), BF16),
        scratch_shapes=[pltpu.VMEM((tm, d), BF16)],
        compiler_params=_params("parallel", "arbitrary"),
        name="gqa_projection",
    )(h, g, shift, scale, w, q_norm[None], k_norm[None], cos_full, sin_signed)


def _gqa_kernel(q_ref, k_ref, v_ref, kc_ref, vc_ref, o_ref, m_sc, l_sc, acc_sc, *, tk):
    tq = q_ref.shape[0]
    n_tok = k_ref.shape[0]
    q = jnp.concatenate([q_ref[:, g * LANES:(g + 1) * LANES] for g in range(GQA_GROUP)], axis=0)
    m_sc[...] = jnp.full_like(m_sc, -jnp.inf)
    l_sc[...] = jnp.zeros_like(l_sc)
    acc_sc[...] = jnp.zeros_like(acc_sc)

    def update(kb, vb):
        s = _dot_nt(q, kb)
        blocks = [s[:, c * LANES:(c + 1) * LANES] for c in range(kb.shape[0] // LANES)]
        m_old = m_sc[...]
        block_max = jnp.max(functools.reduce(jnp.maximum, blocks), axis=-1, keepdims=True)
        m_new = jnp.maximum(m_old, block_max)
        alpha = jnp.exp2(m_old - m_new)
        p = [jnp.exp2(b - m_new) for b in blocks]
        l_sc[...] = alpha * l_sc[...] + functools.reduce(jnp.add, p)
        acc_sc[...] = alpha * acc_sc[...] + _dot(jnp.concatenate(p, axis=-1).astype(BF16), vb)
        m_sc[...] = m_new

    def body(j, carry):
        r = pl.multiple_of(j * tk, tk)
        update(k_ref[pl.ds(r, tk), :], v_ref[pl.ds(r, tk), :])
        return carry

    lax.fori_loop(0, n_tok // tk, body, 0)
    update(kc_ref[...], vc_ref[...])
    o = acc_sc[...] / jnp.sum(l_sc[...], axis=-1, keepdims=True)
    for g in range(GQA_GROUP):
        o_ref[:, g * LANES:(g + 1) * LANES] = o[g * tq:(g + 1) * tq].astype(o_ref.dtype)


def gqa_attention(qkv, qkv_c):
    n_tok = qkv.shape[0]
    n_ctx = qkv_c.shape[0]
    tq = min(n_tok, 512)
    tk = min(n_tok, 1024)
    kv0 = GQA_HEADS
    v0 = GQA_HEADS + GQA_KV_HEADS
    rows = GQA_GROUP * tq
    return pl.pallas_call(
        functools.partial(_gqa_kernel, tk=tk),
        grid=(GQA_KV_HEADS, n_tok // tq),
        in_specs=[
            pl.BlockSpec((tq, GQA_GROUP * LANES), lambda kv, i: (i, kv)),
            pl.BlockSpec((n_tok, LANES), lambda kv, i: (0, kv0 + kv)),
            pl.BlockSpec((n_tok, LANES), lambda kv, i: (0, v0 + kv)),
            pl.BlockSpec((n_ctx, LANES), lambda kv, i: (0, kv0 + kv)),
            pl.BlockSpec((n_ctx, LANES), lambda kv, i: (0, v0 + kv)),
        ],
        out_specs=pl.BlockSpec((tq, GQA_GROUP * LANES), lambda kv, i: (i, kv)),
        out_shape=jax.ShapeDtypeStruct((n_tok, GQA_HEADS * LANES), BF16),
        scratch_shapes=[pltpu.VMEM((rows, LANES), F32)] * 3,
        compiler_params=_params("parallel", "arbitrary"),
        name="gqa_attention",
    )(qkv, qkv, qkv, qkv_c, qkv_c)


def axial_rope_tables(n_tok):
    t = jnp.arange(n_tok, dtype=jnp.int32)
    row = (t // GRID_W).astype(F32)
    col = (t % GRID_W).astype(F32)
    n_freq = LANES // 4
    inv_freq = ROPE_THETA ** (-jnp.arange(n_freq, dtype=F32) / n_freq)
    ang = jnp.concatenate([row[:, None] * inv_freq, col[:, None] * inv_freq], axis=-1)
    cos, sin = jnp.cos(ang), jnp.sin(ang)
    return jnp.concatenate([cos, cos], axis=-1), jnp.concatenate([-sin, sin], axis=-1)


def _conv_silu_kernel(prev_ref, cur_ref, next_ref, w_ref, b_ref, o_ref):
    i = pl.program_id(0)
    tl = cur_ref.shape[0]
    prev = jnp.where(i == 0, 0.0, prev_ref[...].astype(F32))
    nxt = jnp.where(i == pl.num_programs(0) - 1, 0.0, next_ref[...].astype(F32))
    ext = jnp.concatenate([prev, cur_ref[...].astype(F32), nxt], axis=0)
    n_ext = ext.shape[0]
    pad = (SSM_CONV_W - 1) // 2
    acc = jnp.zeros((tl, ext.shape[1]), F32) + b_ref[...]
    for k in range(SSM_CONV_W):
        shifted = ext if k == pad else pltpu.roll(ext, (pad - k) % n_ext, 0)
        acc = acc + shifted[CONV_HALO:CONV_HALO + tl] * w_ref[k:k + 1, :]
    o_ref[...] = _silu(acc).astype(o_ref.dtype)


def conv_silu(zxbc, conv_w, conv_b):
    n = zxbc.shape[0]
    tl = min(n, 512)
    tc = 512
    c0 = SSM_D_INNER // tc
    hb = tl // CONV_HALO
    n_halo = n // CONV_HALO
    return pl.pallas_call(
        _conv_silu_kernel,
        grid=(n // tl, SSM_CONV_DIM // tc),
        in_specs=[
            pl.BlockSpec((CONV_HALO, tc), lambda i, j: (jnp.maximum(i * hb - 1, 0), c0 + j)),
            pl.BlockSpec((tl, tc), lambda i, j: (i, c0 + j)),
            pl.BlockSpec((CONV_HALO, tc), lambda i, j: (jnp.minimum((i + 1) * hb, n_halo - 1), c0 + j)),
            pl.BlockSpec((SSM_CONV_W, tc), lambda i, j: (0, j)),
            pl.BlockSpec((1, tc), lambda i, j: (0, j)),
        ],
        out_specs=pl.BlockSpec((tl, tc), lambda i, j: (i, j)),
        out_shape=jax.ShapeDtypeStruct((n, SSM_CONV_DIM), BF16),
        compiler_params=_params("parallel", "parallel"),
        name="conv_silu",
    )(zxbc, zxbc, zxbc, conv_w, conv_b)


def _ssd_prep_kernel(dt_ref, bias_ref, alog_ref, delta_ref, acs_ref):
    x = dt_ref[...] + bias_ref[...]
    delta = jnp.maximum(x, 0.0) + jnp.log1p(jnp.exp(-jnp.abs(x)))
    da = delta * (-jnp.exp(alog_ref[...]))
    q = x.shape[0]
    i = lax.broadcasted_iota(jnp.int32, (q, q), 0)
    j = lax.broadcasted_iota(jnp.int32, (q, q), 1)
    hi = lax.Precision.HIGHEST
    fwd = jnp.dot((i >= j).astype(F32), da, preferred_element_type=F32, precision=hi)
    bwd = jnp.dot((i <= j).astype(F32), da, preferred_element_type=F32, precision=hi)
    lane = lax.broadcasted_iota(jnp.int32, da.shape, 1)
    delta_ref[...] = delta
    acs_ref[...] = jnp.where(lane < SSM_HEADS, fwd, bwd)


def ssd_prep(dt, dt_bias, a_log):
    n = dt.shape[0]
    q = SSM_CHUNK
    blk = pl.BlockSpec((q, 2 * SSM_HEADS), lambda c: (c, 0))
    row = pl.BlockSpec((1, 2 * SSM_HEADS), lambda c: (0, 0))
    return pl.pallas_call(
        _ssd_prep_kernel,
        grid=(n // q,),
        in_specs=[blk, row, row],
        out_specs=[blk, blk],
        out_shape=[jax.ShapeDtypeStruct(dt.shape, F32)] * 2,
        compiler_params=_params("parallel"),
        name="ssd_prep",
    )(dt, dt_bias.reshape(1, -1), a_log.reshape(1, -1))


def _ssd_kernel(x_ref, b_ref, c_ref, ac_ref, ar_ref, dr_ref, init_ref, y_ref, fin_ref, state_sc, *, n_sub):
    d = pl.program_id(0)
    q = SSM_CHUNK

    @pl.when(pl.program_id(2) == 0)
    def _():
        state_sc[...] = init_ref[0]

    i = lax.broadcasted_iota(jnp.int32, (q, q), 0)
    j = lax.broadcasted_iota(jnp.int32, (q, q), 1)
    causal = jnp.where(d == 0, i - j, j - i) >= 0
    lo = lax.broadcasted_iota(jnp.int32, (q, LANES), 1) < SSM_HEAD_DIM

    def chunk(k, carry):
        ci = k + d * (n_sub - 1 - 2 * k)
        r = pl.multiple_of(ci * q, q)
        bm = b_ref[pl.ds(r, q), :]
        cm = c_ref[pl.ds(r, q), :]
        ac = ac_ref[0, 0, pl.ds(r, q), :]
        ar = ar_ref[0, 0, ci]
        dr = dr_ref[0, 0, ci]
        tot = jnp.where(d == 0, ar[:, q - 1:q], ar[:, 0:1])
        w_out = jnp.exp(tot - ar) * dr
        e_tot = jnp.exp(tot)
        cb = _dot_nt(cm, bm)
        bt = bm.astype(F32).T
        cf = cm.astype(F32)
        state = state_sc[...]
        for p in range(SSM_HEADS_PER_GROUP // 2):
            cols = slice(p * LANES, (p + 1) * LANES)
            x = x_ref[pl.ds(r, q), cols]
            lhs_y, lhs_s = [], []
            for hd in (2 * p, 2 * p + 1):
                a_i = jnp.broadcast_to(ac[:, hd:hd + 1], (q, q))
                seg = jnp.where(causal, a_i - ar[hd:hd + 1, :], -jnp.inf)
                intra = cb * jnp.exp(seg) * dr[hd:hd + 1, :]
                lhs_y.append(jnp.concatenate([intra, cf * jnp.exp(a_i)], axis=1).astype(BF16))
                lhs_s.append((bt * w_out[hd:hd + 1, :]).astype(BF16))
            rhs = jnp.concatenate([x, state[:, cols].astype(BF16)], axis=0)
            res = _dot(jnp.concatenate(lhs_y, axis=0), rhs)
            y_ref[0, pl.ds(r, q), cols] = jnp.where(lo, res[:q], res[q:]).astype(y_ref.dtype)
            s_res = _dot(jnp.concatenate(lhs_s, axis=0), x)
            s_new = jnp.where(lo, s_res[:SSM_STATE], s_res[SSM_STATE:])
            keep = jnp.where(lo[0:1], e_tot[2 * p:2 * p + 1, :], e_tot[2 * p + 1:2 * p + 2, :])
            state_sc[:, cols] = state[:, cols] * keep + s_new
        return carry

    lax.fori_loop(0, n_sub, chunk, 0, unroll=True)

    @pl.when(pl.program_id(2) == pl.num_programs(2) - 1)
    def _():
        fin_ref[0] = state_sc[...]


def ssd_scan(xbc, delta, acs, init_state):
    assert SSM_STATE == SSM_CHUNK == LANES
    n = xbc.shape[0]
    q = SSM_CHUNK
    nc = n // q
    n_sub = min(nc, SSD_CHUNKS_PER_STEP)
    ns = nc // n_sub
    rows = n_sub * q
    hg = SSM_HEADS_PER_GROUP
    gw = hg * SSM_HEAD_DIM
    ac_col = acs.reshape(n, 2, SSM_GROUPS, hg).transpose(1, 2, 0, 3)
    row = lambda t: t.reshape(nc, q, 2, SSM_GROUPS, hg).transpose(2, 3, 0, 4, 1)
    ac_row, dl_row = row(acs), row(delta)
    b0 = SSM_D_INNER // SSM_STATE
    c0 = b0 + SSM_GROUPS
    step = lambda d, s: s + d * (ns - 1 - 2 * s)
    row_spec = pl.BlockSpec((1, 1, n_sub, hg, q), lambda d, g, s: (d, g, step(d, s), 0, 0))
    state_spec = pl.BlockSpec((1, SSM_STATE, gw), lambda d, g, s: (d, 0, g))
    return pl.pallas_call(
        functools.partial(_ssd_kernel, n_sub=n_sub),
        grid=(2, SSM_GROUPS, ns),
        in_specs=[
            pl.BlockSpec((rows, gw), lambda d, g, s: (step(d, s), g)),
            pl.BlockSpec((rows, SSM_STATE), lambda d, g, s: (step(d, s), b0 + g)),
            pl.BlockSpec((rows, SSM_STATE), lambda d, g, s: (step(d, s), c0 + g)),
            pl.BlockSpec((1, 1, rows, hg), lambda d, g, s: (d, g, step(d, s), 0)),
            row_spec, row_spec,
            state_spec,
        ],
        out_specs=[pl.BlockSpec((1, rows, gw), lambda d, g, s: (d, step(d, s), g)), state_spec],
        out_shape=[jax.ShapeDtypeStruct((2, n, SSM_D_INNER), BF16),
                   jax.ShapeDtypeStruct((2, SSM_STATE, SSM_D_INNER), F32)],
        scratch_shapes=[pltpu.VMEM((SSM_STATE, gw), F32)],
        compiler_params=_params("parallel", "parallel", "arbitrary"),
        name="ssd_scan",
    )(xbc, xbc, xbc, ac_col, ac_row, dl_row, init_state)


def _ssm_gate_norm_kernel(yf_ref, yb_ref, x_ref, z_ref, dsum_ref, g_ref, o_ref):
    y = yf_ref[0].astype(F32) + yb_ref[0].astype(F32) + dsum_ref[...] * x_ref[...].astype(F32)
    v = y * _silu(z_ref[...].astype(F32))
    var = jnp.mean(v * v, axis=-1, keepdims=True)
    o_ref[...] = ((v * lax.rsqrt(var + EPS)) * g_ref[...]).astype(o_ref.dtype)


def ssm_gate_norm(y, xbc, zxbc, dsum, norm_g):
    n = xbc.shape[0]
    tl = min(n, 256)
    di = SSM_D_INNER
    row = pl.BlockSpec((1, di), lambda i: (0, 0))
    return pl.pallas_call(
        _ssm_gate_norm_kernel,
        grid=(n // tl,),
        in_specs=[
            pl.BlockSpec((1, tl, di), lambda i: (0, i, 0)),
            pl.BlockSpec((1, tl, di), lambda i: (1, i, 0)),
            pl.BlockSpec((tl, di), lambda i: (i, 0)),
            pl.BlockSpec((tl, di), lambda i: (i, 0)),
            row, row,
        ],
        out_specs=pl.BlockSpec((tl, di), lambda i: (i, 0)),
        out_shape=jax.ShapeDtypeStruct((n, di), BF16),
        compiler_params=_params("parallel"),
        name="ssm_gate_norm",
    )(y, y, xbc, zxbc, dsum, norm_g)


def mamba2_bidirectional(h, hc, g, mod_l, mod_c, w_in, w_out_bf16, layer, conv_w, conv_b, a_log, dt_bias,
                         d_skip, norm_g):
    n_main = SSM_D_INNER + SSM_CONV_DIM
    w_dt = w_in[layer, :, n_main:][None]
    dsum = jnp.repeat(d_skip[0] + d_skip[1], SSM_HEAD_DIM)[None]
    state = jnp.zeros((2, SSM_STATE, SSM_D_INNER), F32)
    outs = []
    for t, mod in ((hc, mod_c), (h, mod_l)):
        shift, scale, gate = mod[3:4], mod[4:5], mod[5:6]
        zxbc = norm_mod_matmul(t, g, shift, scale, w_in, layer, BF16, n=n_main)
        dt = norm_mod_matmul(t, g, shift, scale, w_dt, 0, F32, precise=True)
        xbc = conv_silu(zxbc, conv_w, conv_b[None])
        delta, acs = ssd_prep(dt, dt_bias, a_log)
        y, state = ssd_scan(xbc, delta, acs, state)
        yn = ssm_gate_norm(y, xbc, zxbc, dsum, norm_g[None])
        outs.append(matmul_gated_residual(yn, w_out_bf16, layer, t, gate))
    return outs[1], outs[0]


def kernel(x, c, ctx, c_ctx, ada_w, ada_b, norm_g, ffn_w_in, ffn_w_out, na_w_qkv, na_rpb, na_w_o, ssm_w_in, ssm_conv_w, ssm_conv_b, ssm_a_log, ssm_dt_bias, ssm_d, ssm_norm_g, ssm_w_out, gqa_w_qkv, gqa_q_norm, gqa_k_norm, gqa_w_o, final_norm_g):
    bsz, n_tok, d = x.shape
    assert bsz == 1, "the kernels take one sequence"
    depth = ada_w.shape[0]
    h, hc = x[0], ctx[0]
    mod = adaln_mod(c, c_ctx, ada_w, ada_b)
    cos_full, sin_signed = axial_rope_tables(n_tok)
    rope_off = jnp.zeros((ctx.shape[1], LANES), F32)
    ffn_w = {(0, 0): (ffn_w_in[0, 0].astype(BF16), ffn_w_out[0, 0].astype(BF16))}
    na_w_o_b, ssm_w_out_b, gqa_w_o_b = na_w_o.astype(BF16), ssm_w_out.astype(BF16), gqa_w_o.astype(BF16)
    gqa_q_scale = LANES ** -0.5 * LOG2_E

    def ffn(t, m, i, k, latent, final_g=None):
        w_in_b, w_out_b = ffn_w[(i, k)]
        nxt = (i + (k + 1) // 2, (k + 1) % 2)
        cast_next = latent and nxt[0] < depth
        out = ffn_half_step(t, norm_g[i, 2 * k][None], m[6 * k:6 * k + 1], m[6 * k + 1:6 * k + 2],
                            m[6 * k + 2:6 * k + 3], w_in_b, w_out_b,
                            (ffn_w_in, ffn_w_out) + nxt if cast_next else None, final_g)
        if cast_next:
            out, *ffn_w[nxt] = out
        return out

    for i in range(depth):
        last = i == depth - 1
        ml, mc = mod[i, 0], mod[i, 1]
        g_mix = norm_g[i, 1][None]
        h = ffn(h, ml, i, 0, True)
        hc = ffn(hc, mc, i, 0, False)
        kind, j = i % 3, i // 3
        if kind == 0:
            qkv = norm_mod_matmul(h, g_mix, ml[3:4], ml[4:5], na_w_qkv, j, BF16)
            qkv_c = norm_mod_matmul(hc, g_mix, mc[3:4], mc[4:5], na_w_qkv, j, BF16)
            o = neighbourhood_attention(qkv, qkv_c, na_rpb[j])
            h = matmul_gated_residual(o, na_w_o_b, j, h, ml[5:6])
            if not last:
                oc = ctx_attention(qkv_c, NA_HEADS, lambda hd: hd, lambda hd: NA_HEADS + hd,
                                   lambda hd: 2 * NA_HEADS + hd)
                hc = matmul_gated_residual(oc, na_w_o_b, j, hc, mc[5:6])
        elif kind == 1:
            h, hc = mamba2_bidirectional(h, hc, g_mix, ml, mc, ssm_w_in, ssm_w_out_b, j, ssm_conv_w[j],
                                         ssm_conv_b[j], ssm_a_log[j], ssm_dt_bias[j], ssm_d[j], ssm_norm_g[j])
        else:
            qkv = gqa_projection(h, g_mix, ml[3:4], ml[4:5], gqa_w_qkv, j, gqa_q_norm[j], gqa_k_norm[j],
                                 cos_full, sin_signed, True, gqa_q_scale)
            qkv_c = gqa_projection(hc, g_mix, mc[3:4], mc[4:5], gqa_w_qkv, j, gqa_q_norm[j], gqa_k_norm[j],
                                   rope_off, rope_off, False, 1.0)
            o = gqa_attention(qkv, qkv_c)
            h = matmul_gated_residual(o, gqa_w_o_b, j, h, ml[5:6])
            if not last:
                oc = ctx_attention(qkv_c, GQA_HEADS, lambda hd: hd, lambda hd: GQA_HEADS + hd // GQA_GROUP,
                                   lambda hd: GQA_HEADS + GQA_KV_HEADS + hd // GQA_GROUP)
                hc = matmul_gated_residual(oc, gqa_w_o_b, j, hc, mc[5:6])
        h = ffn(h, ml, i, 1, True, final_norm_g[None] if last else None)
        if not last:
            hc = ffn(hc, mc, i, 1, False)
    return h[None]
```

```python
import functools

import jax
import jax.numpy as jnp
import numpy as np
from jax import lax
from jax.experimental import pallas as pl
from jax.experimental.pallas import tpu as pltpu

F32 = jnp.float32
BF16 = jnp.bfloat16

EPS = 1e-6
HALF_STEP = 0.5
N_MOD = 9
GRID_W = 64
LANES = 128
NORM_ROWS = 16

NA_HEADS = 16
NA_WIN_ROWS = 8
NA_WIN_COLS = 16
NA_Q_ROWS = 4
NA_BAND_ROWS = 12
NA_BLOCKS_PER_STEP = 4

SSM_HEADS = 64
SSM_HEAD_DIM = 64
SSM_GROUPS = 8
SSM_STATE = 128
SSM_CHUNK = 128
SSM_D_INNER = SSM_HEADS * SSM_HEAD_DIM
SSM_GN = SSM_GROUPS * SSM_STATE
SSM_CONV_W = 5
SSM_CONV_DIM = SSM_D_INNER + 2 * SSM_GN
SSM_HEADS_PER_GROUP = SSM_HEADS // SSM_GROUPS
SSD_CHUNKS_PER_STEP = 4
CONV_HALO = 16

GQA_HEADS = 16
GQA_KV_HEADS = 4
GQA_GROUP = GQA_HEADS // GQA_KV_HEADS
ROPE_THETA = 10000.0
LOG2_E = 1.4426950408889634

MASK_VALUE = -1e30
VMEM_LIMIT = 56 * 1024 * 1024


def _params(*sem):
    return pltpu.CompilerParams(dimension_semantics=sem, vmem_limit_bytes=VMEM_LIMIT)


def _silu(x):
    return x / (1.0 + jnp.exp(-x))


def _dot(a, b):
    return jnp.dot(a, b, preferred_element_type=F32)


def _dot_nt(a, b):
    return lax.dot_general(a, b, (((1,), (1,)), ((), ())), preferred_element_type=F32)


def _dot_tn(a, b):
    return lax.dot_general(a, b, (((0,), (0,)), ((), ())), preferred_element_type=F32)


def _rowwise(n_rows, chunk, body, unroll=2):
    def step(i, carry):
        body(pl.ds(pl.multiple_of(i * chunk, chunk), chunk))
        return carry

    lax.fori_loop(0, n_rows // chunk, step, 0, unroll=unroll)


def _norm_modulate_into(dst_ref, src_ref, g_ref, sh_ref, sc_ref, rs_ref):
    gain = g_ref[...] * (1.0 + sc_ref[...])
    shift = sh_ref[...]

    n_rows, d = src_ref.shape

    def stats(rows):
        x = src_ref[rows, :]
        sq = x * x
        rs_ref[rows, :] = functools.reduce(jnp.add, [sq[:, c * LANES:(c + 1) * LANES] for c in range(d // LANES)])

    def scale(rows):
        x = src_ref[rows, :]
        rs = jnp.concatenate([rs_ref[rows, :]] * (d // LANES), axis=1)
        dst_ref[rows, :] = ((x * rs) * gain + shift).astype(dst_ref.dtype)

    _rowwise(n_rows, NORM_ROWS, stats, unroll=4)
    var = jnp.sum(rs_ref[...], axis=-1, keepdims=True) * (1.0 / d)
    rs_ref[...] = jnp.broadcast_to(lax.rsqrt(var + EPS), rs_ref.shape)
    _rowwise(n_rows, NORM_ROWS, scale, unroll=2)


def _adaln_kernel(c_ref, w_ref, b_ref, o_ref, s_sc, *, rows, group):
    tk, tn = w_ref.shape
    s_sc[...] = _silu(c_ref[...])

    @pl.when(pl.program_id(2) == 0)
    def _():
        o_ref[...] = jnp.broadcast_to(b_ref[...], o_ref.shape)

    for lg in range(tn // group):
        cols = slice(lg * group, (lg + 1) * group)

        def body(i, carry):
            a0, a1 = carry
            r = pl.multiple_of(i * rows, rows)
            w = w_ref[pl.ds(r, rows), cols]
            s0 = jnp.concatenate([s_sc[0, pl.ds(r, rows), :]] * (group // LANES), axis=1)
            s1 = jnp.concatenate([s_sc[1, pl.ds(r, rows), :]] * (group // LANES), axis=1)
            return a0 + w * s0, a1 + w * s1

        zero = jnp.zeros((rows, group), F32)
        a0, a1 = lax.fori_loop(0, tk // rows, body, (zero, zero), unroll=2)
        o_ref[0:1, cols] += jnp.sum(a0, axis=0, keepdims=True)
        o_ref[1:2, cols] += jnp.sum(a1, axis=0, keepdims=True)


def adaln_mod(c, c_ctx, ada_w, ada_b):
    depth, d, n = ada_w.shape
    tk = 256
    tn = n // 4 if n % 2048 == 0 else 512
    cb = jnp.broadcast_to(jnp.stack([c[0], c_ctx])[:, :, None], (2, d, LANES))
    out = pl.pallas_call(
        functools.partial(_adaln_kernel, rows=16, group=512),
        grid=(depth, n // tn, d // tk),
        in_specs=[
            pl.BlockSpec((2, tk, LANES), lambda l, j, k: (0, k, 0)),
            pl.BlockSpec((None, tk, tn), lambda l, j, k: (l, k, j)),
            pl.BlockSpec((None, 1, tn), lambda l, j, k: (l, 0, j)),
        ],
        out_specs=pl.BlockSpec((None, 2, tn), lambda l, j, k: (l, 0, j)),
        out_shape=jax.ShapeDtypeStruct((depth, 2, n), F32),
        scratch_shapes=[pltpu.VMEM((2, tk, LANES), F32)],
        compiler_params=_params("parallel", "parallel", "arbitrary"),
        name="adaln_mod",
    )(cb, ada_w, ada_b.reshape(depth, 1, n))
    return out.reshape(depth, 2, N_MOD, d)


def _ffn_kernel(h_ref, g_ref, sh_ref, sc_ref, gate_ref, wa_ref, wb_ref, wo_ref, fg_ref, *rest, final_norm, cast_next):
    if cast_next:
        nwi_ref, nwo_ref, o_ref, nwi_out, nwo_out, xn_sc, acc_sc, rs_sc = rest
    else:
        o_ref, xn_sc, acc_sc, rs_sc = rest
    j = pl.program_id(1)

    @pl.when(j == 0)
    def _():
        _norm_modulate_into(xn_sc, h_ref, g_ref, sh_ref, sc_ref, rs_sc)
        acc_sc[...] = jnp.zeros_like(acc_sc)

    if cast_next:
        nwi_out[...] = nwi_ref[...].astype(BF16)
        nwo_out[...] = nwo_ref[...].astype(BF16)
    xn = xn_sc[...]
    a = _dot(xn, wa_ref[...])
    b = _dot(xn, wb_ref[...])
    hm = (_silu(a) * b).astype(BF16)
    acc_sc[...] += _dot(hm, wo_ref[...])

    @pl.when(j == pl.num_programs(1) - 1)
    def _():
        out = h_ref[...] + (HALF_STEP * gate_ref[...]) * acc_sc[...]
        if final_norm:
            var = jnp.mean(out * out, axis=-1, keepdims=True)
            out = (out * lax.rsqrt(var + EPS)) * fg_ref[...]
        o_ref[...] = out


def ffn_half_step(h, g, shift, scale, gate, w_in, w_out, next_w=None, final_g=None):
    m, d = h.shape
    f = w_out.shape[0]
    tm = min(m, 512)
    tf = 512
    nf = f // tf
    n_i = m // tm
    row = lambda i, j: (0, 0)
    fg = final_g if final_g is not None else g
    in_specs = [
        pl.BlockSpec((tm, d), lambda i, j: (i, 0)),
        pl.BlockSpec((1, d), row), pl.BlockSpec((1, d), row),
        pl.BlockSpec((1, d), row), pl.BlockSpec((1, d), row),
        pl.BlockSpec((d, tf), lambda i, j: (0, j)),
        pl.BlockSpec((d, tf), lambda i, j: (0, j + nf)),
        pl.BlockSpec((tf, d), lambda i, j: (j, 0)),
        pl.BlockSpec((1, d), row),
    ]
    out_specs = [pl.BlockSpec((tm, d), lambda i, j: (i, 0))]
    out_shape = [jax.ShapeDtypeStruct((m, d), F32)]
    args = [h, g, shift, scale, gate, w_in, w_in, w_out, fg]
    if next_w is not None:
        nw_in, nw_out, layer, k = next_w
        ri, ci, ro = d // n_i, 2 * f // nf, f // (n_i * nf)
        assert ri % 16 == 0 and ci % LANES == 0 and ro % 16 == 0
        in_specs += [pl.BlockSpec((None, None, ri, ci), lambda i, j: (layer, k, i, j)),
                     pl.BlockSpec((None, None, ro, d), lambda i, j: (layer, k, i * nf + j, 0))]
        out_specs += [pl.BlockSpec((ri, ci), lambda i, j: (i, j)),
                      pl.BlockSpec((ro, d), lambda i, j: (i * nf + j, 0))]
        out_shape += [jax.ShapeDtypeStruct((d, 2 * f), BF16), jax.ShapeDtypeStruct((f, d), BF16)]
        args += [nw_in, nw_out]
    outs = pl.pallas_call(
        functools.partial(_ffn_kernel, final_norm=final_g is not None, cast_next=next_w is not None),
        grid=(n_i, nf),
        in_specs=in_specs,
        out_specs=out_specs,
        out_shape=out_shape,
        scratch_shapes=[pltpu.VMEM((tm, d), BF16), pltpu.VMEM((tm, d), F32), pltpu.VMEM((tm, LANES), F32)],
        compiler_params=_params("parallel", "arbitrary"),
        name="ffn_half_step",
    )(*args)
    return outs if next_w is not None else outs[0]


def _nmm_kernel(h_ref, g_ref, sh_ref, sc_ref, w_ref, o_ref, xn_sc, rs_sc, *, precise):
    @pl.when(pl.program_id(1) == 0)
    def _():
        _norm_modulate_into(xn_sc, h_ref, g_ref, sh_ref, sc_ref, rs_sc)

    if precise:
        y = jnp.dot(xn_sc[...], w_ref[...], preferred_element_type=F32, precision=lax.Precision.HIGHEST)
    else:
        y = _dot(xn_sc[...], w_ref[...].astype(BF16))
    o_ref[...] = y.astype(o_ref.dtype)


def norm_mod_matmul(h, g, shift, scale, w, layer, out_dtype, n=None, precise=False):
    m, d = h.shape
    n = w.shape[2] if n is None else n
    tm = min(m, 1024)
    tn = n if n <= 1024 else (1024 if n % 1024 == 0 else 512)
    row = lambda i, j: (0, 0)
    return pl.pallas_call(
        functools.partial(_nmm_kernel, precise=precise),
        grid=(m // tm, n // tn),
        in_specs=[
            pl.BlockSpec((tm, d), lambda i, j: (i, 0)),
            pl.BlockSpec((1, d), row), pl.BlockSpec((1, d), row), pl.BlockSpec((1, d), row),
            pl.BlockSpec((None, d, tn), lambda i, j: (layer, 0, j)),
        ],
        out_specs=pl.BlockSpec((tm, tn), lambda i, j: (i, j)),
        out_shape=jax.ShapeDtypeStruct((m, n), out_dtype),
        scratch_shapes=[pltpu.VMEM((tm, d), F32 if precise else BF16), pltpu.VMEM((tm, LANES), F32)],
        compiler_params=_params("parallel", "arbitrary"),
        name="norm_mod_matmul",
    )(h, g, shift, scale, w)


def _mgr_kernel(a_ref, w_ref, h_ref, gate_ref, o_ref):
    o_ref[...] = h_ref[...] + gate_ref[...] * _dot(a_ref[...], w_ref[...])


def matmul_gated_residual(a, w, layer, h, gate):
    m, k = a.shape
    n = w.shape[2]
    tm = min(m, 512)
    tn = n if k * n * 2 <= 8 * 1024 * 1024 else min(n, 1024)
    return pl.pallas_call(
        _mgr_kernel,
        grid=(m // tm, n // tn),
        in_specs=[
            pl.BlockSpec((tm, k), lambda i, j: (i, 0)),
            pl.BlockSpec((None, k, tn), lambda i, j: (layer, 0, j)),
            pl.BlockSpec((tm, tn), lambda i, j: (i, j)),
            pl.BlockSpec((1, tn), lambda i, j: (0, j)),
        ],
        out_specs=pl.BlockSpec((tm, tn), lambda i, j: (i, j)),
        out_shape=jax.ShapeDtypeStruct((m, n), F32),
        compiler_params=_params("parallel", "parallel"),
        name="matmul_gated_residual",
    )(a, w, h, gate)


def _ctx_attn_kernel(q_ref, k_ref, v_ref, o_ref):
    q = q_ref[...]
    s = _dot_nt(q, k_ref[...]) * (q.shape[-1] ** -0.5)
    m = jnp.max(s, axis=-1, keepdims=True)
    p = jnp.exp(s - m)
    l = jnp.sum(p, axis=-1, keepdims=True)
    o_ref[...] = (_dot(p.astype(BF16), v_ref[...]) / l).astype(o_ref.dtype)


def ctx_attention(qkv, n_heads, q_blk, k_blk, v_blk):
    n_ctx = qkv.shape[0]
    spec = lambda f: pl.BlockSpec((n_ctx, LANES), lambda h: (0, f(h)))
    return pl.pallas_call(
        _ctx_attn_kernel,
        grid=(n_heads,),
        in_specs=[spec(q_blk), spec(k_blk), spec(v_blk)],
        out_specs=pl.BlockSpec((n_ctx, LANES), lambda h: (0, h)),
        out_shape=jax.ShapeDtypeStruct((n_ctx, n_heads * LANES), BF16),
        compiler_params=_params("parallel"),
        name="ctx_attention",
    )(qkv, qkv, qkv)


def _na_kernel(q_ref, k_ref, v_ref, kc_ref, vc_ref, pair_ref, *rest, n_rows):
    mask_refs, o_ref = rest[:-1], rest[-1]
    nq = NA_Q_ROWS * GRID_W
    n_band = NA_BAND_ROWS * GRID_W
    scale = q_ref.shape[-1] ** -0.5 * LOG2_E
    for blk, mask_ref in enumerate(mask_refs):
        r0 = (pl.program_id(1) * len(mask_refs) + blk) * NA_Q_ROWS
        b0 = jnp.clip(r0 - NA_WIN_ROWS // 2, 0, n_rows - NA_BAND_ROWS)
        start = pl.multiple_of(b0 * GRID_W, GRID_W)
        q = q_ref[blk * nq:(blk + 1) * nq, :]
        bias = jnp.concatenate([
            jnp.concatenate([
                pair_ref[0, jnp.clip(b0 + 2 * t - r0 - qi + NA_WIN_ROWS, 0, 2 * NA_WIN_ROWS - 1)]
                for t in range(NA_BAND_ROWS // 2)], axis=1)
            for qi in range(NA_Q_ROWS)], axis=0)
        s_w = _dot_nt(q, k_ref[pl.ds(start, n_band), :]) * scale + (bias + mask_ref[0])
        s_c = _dot_nt(q, kc_ref[...]) * scale
        m = jnp.maximum(jnp.max(s_w, axis=-1, keepdims=True), jnp.max(s_c, axis=-1, keepdims=True))
        p_w = jnp.exp2(s_w - m)
        p_c = jnp.exp2(s_c - m)
        l = jnp.sum(p_w, axis=-1, keepdims=True) + jnp.sum(p_c, axis=-1, keepdims=True)
        o = _dot(p_w.astype(BF16), v_ref[pl.ds(start, n_band), :]) + _dot(p_c.astype(BF16), vc_ref[...])
        o_ref[blk * nq:(blk + 1) * nq, :] = (o / l).astype(o_ref.dtype)


def _na_window_masks(n_rows):
    nq, nk = NA_Q_ROWS * GRID_W, NA_BAND_ROWS * GRID_W
    valid = []
    for r0 in (0, NA_Q_ROWS, n_rows - NA_Q_ROWS):
        b0 = int(np.clip(r0 - NA_WIN_ROWS // 2, 0, n_rows - NA_BAND_ROWS))
        qr = r0 + np.arange(nq) // GRID_W
        qc = np.arange(nq) % GRID_W
        kr = b0 + np.arange(nk) // GRID_W
        kc = np.arange(nk) % GRID_W
        rs = np.clip(qr - NA_WIN_ROWS // 2, 0, n_rows - NA_WIN_ROWS)
        cs = np.clip(qc - NA_WIN_COLS // 2, 0, GRID_W - NA_WIN_COLS)
        ok = ((kr[None] >= rs[:, None]) & (kr[None] < rs[:, None] + NA_WIN_ROWS)
              & (kc[None] >= cs[:, None]) & (kc[None] < cs[:, None] + NA_WIN_COLS))
        valid.append(ok)
    return np.where(np.stack(valid), 0.0, MASK_VALUE).astype(np.float32)


def _na_pair_tiles(rpb):
    n_heads = rpb.shape[0]
    dcol = np.arange(GRID_W)[None, :] - np.arange(GRID_W)[:, None] + NA_WIN_COLS - 1
    in_table = (dcol >= 0) & (dcol < 2 * NA_WIN_COLS - 1)
    onehot = (np.clip(dcol, 0, 2 * NA_WIN_COLS - 2)[None] == np.arange(2 * NA_WIN_COLS - 1)[:, None, None]) & in_table
    toeplitz = jnp.einsum("hrd,dqk->hrqk", rpb, jnp.asarray(onehot, F32), precision=lax.Precision.HIGHEST)
    zero = jnp.zeros((n_heads, 1, GRID_W, GRID_W), F32)
    padded = jnp.concatenate([zero, toeplitz, zero], axis=1)
    return jnp.concatenate([padded[:, :-1], padded[:, 1:]], axis=-1)


def neighbourhood_attention(qkv, qkv_c, rpb):
    n_tok = qkv.shape[0]
    n_ctx = qkv_c.shape[0]
    n_rows = n_tok // GRID_W
    n_blocks = n_rows // NA_Q_ROWS
    nb = NA_BLOCKS_PER_STEP
    nq, nk = NA_Q_ROWS * GRID_W, NA_BAND_ROWS * GRID_W
    assert NA_BAND_ROWS % 2 == 0
    pairs = _na_pair_tiles(rpb) * LOG2_E
    masks = jnp.asarray(_na_window_masks(n_rows))
    h_ = NA_HEADS

    def mask_spec(blk):
        def index(h, s):
            rb = s * nb + blk
            return (jnp.where(rb == 0, 0, jnp.where(rb == n_blocks - 1, 2, 1)), 0, 0)
        return pl.BlockSpec((1, nq, nk), index)

    return pl.pallas_call(
        functools.partial(_na_kernel, n_rows=n_rows),
        grid=(h_, n_blocks // nb),
        in_specs=[
            pl.BlockSpec((nb * nq, LANES), lambda h, s: (s, h)),
            pl.BlockSpec((n_tok, LANES), lambda h, s: (0, h_ + h)),
            pl.BlockSpec((n_tok, LANES), lambda h, s: (0, 2 * h_ + h)),
            pl.BlockSpec((n_ctx, LANES), lambda h, s: (0, h_ + h)),
            pl.BlockSpec((n_ctx, LANES), lambda h, s: (0, 2 * h_ + h)),
            pl.BlockSpec((1,) + pairs.shape[1:], lambda h, s: (h, 0, 0, 0)),
        ] + [mask_spec(blk) for blk in range(nb)],
        out_specs=pl.BlockSpec((nb * nq, LANES), lambda h, s: (s, h)),
        out_shape=jax.ShapeDtypeStruct((n_tok, h_ * LANES), BF16),
        compiler_params=_params("parallel", "arbitrary"),
        name="neighbourhood_attention",
    )(qkv, qkv, qkv, qkv_c, qkv_c, pairs, *([masks] * nb))


def _gqa_proj_kernel(h_ref, g_ref, sh_ref, sc_ref, w_ref, qn_ref, kn_ref, cos_ref, sin_ref, o_ref, xn_sc,
                     rs_sc, *, rope, q_scale):
    j = pl.program_id(1)
    tn = o_ref.shape[1]
    n_q = GQA_HEADS * LANES // tn

    @pl.when(j == 0)
    def _():
        _norm_modulate_into(xn_sc, h_ref, g_ref, sh_ref, sc_ref, rs_sc)

    y = _dot(xn_sc[...], w_ref[...].astype(BF16))

    def normed(w, mult):
        for s in range(tn // LANES):
            cols = slice(s * LANES, (s + 1) * LANES)
            x = y[:, cols]
            var = jnp.mean(x * x, axis=-1, keepdims=True)
            t = (x * lax.rsqrt(var + EPS)) * w
            if rope:
                t = t * cos_ref[...] + pltpu.roll(t, LANES // 2, 1) * sin_ref[...]
            if mult != 1.0:
                t = t * mult
            o_ref[:, cols] = t.astype(o_ref.dtype)

    @pl.when(j < n_q)
    def _():
        normed(qn_ref[...], q_scale)

    @pl.when(j == n_q)
    def _():
        normed(kn_ref[...], 1.0)

    @pl.when(j > n_q)
    def _():
        o_ref[...] = y.astype(o_ref.dtype)


def gqa_projection(h, g, shift, scale, w, layer, q_norm, k_norm, cos_full, sin_signed, rope, q_scale):
    m, d = h.shape
    n = w.shape[2]
    tm = min(m, 1024)
    tn = GQA_KV_HEADS * LANES
    row = lambda i, j: (0, 0)
    head = pl.BlockSpec((1, LANES), row)
    pos = pl.BlockSpec((tm, LANES), lambda i, j: (i, 0))
    return pl.pallas_call(
        functools.partial(_gqa_proj_kernel, rope=rope, q_scale=q_scale),
        grid=(m // tm, n // tn),
        in_specs=[
            pl.BlockSpec((tm, d), lambda i, j: (i, 0)),
            pl.BlockSpec((1, d), row), pl.BlockSpec((1, d), row), pl.BlockSpec((1, d), row),
            pl.BlockSpec((None, d, tn), lambda i, j: (layer, 0, j)),
            head, head, pos, pos,
        ],
        out_specs=pl.BlockSpec((tm, tn), lambda i, j: (i, j)),
        out_shape=jax.ShapeDtypeStruct((m, n), BF16),
        scratch_shapes=[pltpu.VMEM((tm, d), BF16), pltpu.VMEM((tm, LANES), F32)],
        compiler_params=_params("parallel", "arbitrary"),
        name="gqa_projection",
    )(h, g, shift, scale, w, q_norm[None], k_norm[None], cos_full, sin_signed)


def _gqa_kernel(q_ref, k_ref, v_ref, kc_ref, vc_ref, o_ref, m_sc, l_sc, acc_sc, *, tk):
    tq = q_ref.shape[0]
    n_tok = k_ref.shape[0]
    q = jnp.concatenate([q_ref[:, g * LANES:(g + 1) * LANES] for g in range(GQA_GROUP)], axis=0)
    m_sc[...] = jnp.full_like(m_sc, -jnp.inf)
    l_sc[...] = jnp.zeros_like(l_sc)
    acc_sc[...] = jnp.zeros_like(acc_sc)

    def update(kb, vb):
        s = _dot_nt(q, kb)
        blocks = [s[:, c * LANES:(c + 1) * LANES] for c in range(kb.shape[0] // LANES)]
        m_old = m_sc[...]
        block_max = jnp.max(functools.reduce(jnp.maximum, blocks), axis=-1, keepdims=True)
        m_new = jnp.maximum(m_old, block_max)
        alpha = jnp.exp2(m_old - m_new)
        p = [jnp.exp2(b - m_new) for b in blocks]
        l_sc[...] = alpha * l_sc[...] + functools.reduce(jnp.add, p)
        acc_sc[...] = alpha * acc_sc[...] + _dot(jnp.concatenate(p, axis=-1).astype(BF16), vb)
        m_sc[...] = m_new

    def body(j, carry):
        r = pl.multiple_of(j * tk, tk)
        update(k_ref[pl.ds(r, tk), :], v_ref[pl.ds(r, tk), :])
        return carry

    lax.fori_loop(0, n_tok // tk, body, 0)
    update(kc_ref[...], vc_ref[...])
    o = acc_sc[...] / jnp.sum(l_sc[...], axis=-1, keepdims=True)
    for g in range(GQA_GROUP):
        o_ref[:, g * LANES:(g + 1) * LANES] = o[g * tq:(g + 1) * tq].astype(o_ref.dtype)


def gqa_attention(qkv, qkv_c):
    n_tok = qkv.shape[0]
    n_ctx = qkv_c.shape[0]
    tq = min(n_tok, 512)
    tk = min(n_tok, 1024)
    kv0 = GQA_HEADS
    v0 = GQA_HEADS + GQA_KV_HEADS
    rows = GQA_GROUP * tq
    return pl.pallas_call(
        functools.partial(_gqa_kernel, tk=tk),
        grid=(GQA_KV_HEADS, n_tok // tq),
        in_specs=[
            pl.BlockSpec((tq, GQA_GROUP * LANES), lambda kv, i: (i, kv)),
            pl.BlockSpec((n_tok, LANES), lambda kv, i: (0, kv0 + kv)),
            pl.BlockSpec((n_tok, LANES), lambda kv, i: (0, v0 + kv)),
            pl.BlockSpec((n_ctx, LANES), lambda kv, i: (0, kv0 + kv)),
            pl.BlockSpec((n_ctx, LANES), lambda kv, i: (0, v0 + kv)),
        ],
        out_specs=pl.BlockSpec((tq, GQA_GROUP * LANES), lambda kv, i: (i, kv)),
        out_shape=jax.ShapeDtypeStruct((n_tok, GQA_HEADS * LANES), BF16),
        scratch_shapes=[pltpu.VMEM((rows, LANES), F32)] * 3,
        compiler_params=_params("parallel", "arbitrary"),
        name="gqa_attention",
    )(qkv, qkv, qkv, qkv_c, qkv_c)


def axial_rope_tables(n_tok):
    t = jnp.arange(n_tok, dtype=jnp.int32)
    row = (t // GRID_W).astype(F32)
    col = (t % GRID_W).astype(F32)
    n_freq = LANES // 4
    inv_freq = ROPE_THETA ** (-jnp.arange(n_freq, dtype=F32) / n_freq)
    ang = jnp.concatenate([row[:, None] * inv_freq, col[:, None] * inv_freq], axis=-1)
    cos, sin = jnp.cos(ang), jnp.sin(ang)
    return jnp.concatenate([cos, cos], axis=-1), jnp.concatenate([-sin, sin], axis=-1)


def _conv_silu_kernel(prev_ref, cur_ref, next_ref, w_ref, b_ref, o_ref):
    i = pl.program_id(0)
    tl = cur_ref.shape[0]
    prev = jnp.where(i == 0, 0.0, prev_ref[...].astype(F32))
    nxt = jnp.where(i == pl.num_programs(0) - 1, 0.0, next_ref[...].astype(F32))
    ext = jnp.concatenate([prev, cur_ref[...].astype(F32), nxt], axis=0)
    n_ext = ext.shape[0]
    pad = (SSM_CONV_W - 1) // 2
    acc = jnp.zeros((tl, ext.shape[1]), F32) + b_ref[...]
    for k in range(SSM_CONV_W):
        shifted = ext if k == pad else pltpu.roll(ext, (pad - k) % n_ext, 0)
        acc = acc + shifted[CONV_HALO:CONV_HALO + tl] * w_ref[k:k + 1, :]
    o_ref[...] = _silu(acc).astype(o_ref.dtype)


def conv_silu(zxbc, conv_w, conv_b):
    n = zxbc.shape[0]
    tl = min(n, 512)
    tc = 512
    c0 = SSM_D_INNER // tc
    hb = tl // CONV_HALO
    n_halo = n // CONV_HALO
    return pl.pallas_call(
        _conv_silu_kernel,
        grid=(n // tl, SSM_CONV_DIM // tc),
        in_specs=[
            pl.BlockSpec((CONV_HALO, tc), lambda i, j: (jnp.maximum(i * hb - 1, 0), c0 + j)),
            pl.BlockSpec((tl, tc), lambda i, j: (i, c0 + j)),
            pl.BlockSpec((CONV_HALO, tc), lambda i, j: (jnp.minimum((i + 1) * hb, n_halo - 1), c0 + j)),
            pl.BlockSpec((SSM_CONV_W, tc), lambda i, j: (0, j)),
            pl.BlockSpec((1, tc), lambda i, j: (0, j)),
        ],
        out_specs=pl.BlockSpec((tl, tc), lambda i, j: (i, j)),
        out_shape=jax.ShapeDtypeStruct((n, SSM_CONV_DIM), BF16),
        compiler_params=_params("parallel", "parallel"),
        name="conv_silu",
    )(zxbc, zxbc, zxbc, conv_w, conv_b)


def _ssd_prep_kernel(dt_ref, bias_ref, alog_ref, delta_ref, acs_ref):
    x = dt_ref[...] + bias_ref[...]
    delta = jnp.maximum(x, 0.0) + jnp.log1p(jnp.exp(-jnp.abs(x)))
    da = delta * (-jnp.exp(alog_ref[...]))
    q = x.shape[0]
    i = lax.broadcasted_iota(jnp.int32, (q, q), 0)
    j = lax.broadcasted_iota(jnp.int32, (q, q), 1)
    hi = lax.Precision.HIGHEST
    fwd = jnp.dot((i >= j).astype(F32), da, preferred_element_type=F32, precision=hi)
    bwd = jnp.dot((i <= j).astype(F32), da, preferred_element_type=F32, precision=hi)
    lane = lax.broadcasted_iota(jnp.int32, da.shape, 1)
    delta_ref[...] = delta
    acs_ref[...] = jnp.where(lane < SSM_HEADS, fwd, bwd)


def ssd_prep(dt, dt_bias, a_log):
    n = dt.shape[0]
    q = SSM_CHUNK
    blk = pl.BlockSpec((q, 2 * SSM_HEADS), lambda c: (c, 0))
    row = pl.BlockSpec((1, 2 * SSM_HEADS), lambda c: (0, 0))
    return pl.pallas_call(
        _ssd_prep_kernel,
        grid=(n // q,),
        in_specs=[blk, row, row],
        out_specs=[blk, blk],
        out_shape=[jax.ShapeDtypeStruct(dt.shape, F32)] * 2,
        compiler_params=_params("parallel"),
        name="ssd_prep",
    )(dt, dt_bias.reshape(1, -1), a_log.reshape(1, -1))


def _ssd_kernel(x_ref, b_ref, c_ref, ac_ref, ar_ref, dr_ref, init_ref, y_ref, fin_ref, state_sc, *, n_sub):
    d = pl.program_id(0)
    q = SSM_CHUNK

    @pl.when(pl.program_id(2) == 0)
    def _():
        state_sc[...] = init_ref[0]

    i = lax.broadcasted_iota(jnp.int32, (q, q), 0)
    j = lax.broadcasted_iota(jnp.int32, (q, q), 1)
    causal = jnp.where(d == 0, i - j, j - i) >= 0
    lo = lax.broadcasted_iota(jnp.int32, (q, LANES), 1) < SSM_HEAD_DIM

    def chunk(k, carry):
        ci = k + d * (n_sub - 1 - 2 * k)
        r = pl.multiple_of(ci * q, q)
        bm = b_ref[pl.ds(r, q), :]
        cm = c_ref[pl.ds(r, q), :]
        ac = ac_ref[0, 0, pl.ds(r, q), :]
        ar = ar_ref[0, 0, ci]
        dr = dr_ref[0, 0, ci]
        tot = jnp.where(d == 0, ar[:, q - 1:q], ar[:, 0:1])
        w_out = jnp.exp(tot - ar) * dr
        e_tot = jnp.exp(tot)
        cb = _dot_nt(cm, bm)
        bt = bm.astype(F32).T
        cf = cm.astype(F32)
        state = state_sc[...]
        for p in range(SSM_HEADS_PER_GROUP // 2):
            cols = slice(p * LANES, (p + 1) * LANES)
            x = x_ref[pl.ds(r, q), cols]
            lhs_y, lhs_s = [], []
            for hd in (2 * p, 2 * p + 1):
                a_i = jnp.broadcast_to(ac[:, hd:hd + 1], (q, q))
                seg = jnp.where(causal, a_i - ar[hd:hd + 1, :], -jnp.inf)
                intra = cb * jnp.exp(seg) * dr[hd:hd + 1, :]
                lhs_y.append(jnp.concatenate([intra, cf * jnp.exp(a_i)], axis=1).astype(BF16))
                lhs_s.append((bt * w_out[hd:hd + 1, :]).astype(BF16))
            rhs = jnp.concatenate([x, state[:, cols].astype(BF16)], axis=0)
            res = _dot(jnp.concatenate(lhs_y, axis=0), rhs)
            y_ref[0, pl.ds(r, q), cols] = jnp.where(lo, res[:q], res[q:]).astype(y_ref.dtype)
            s_res = _dot(jnp.concatenate(lhs_s, axis=0), x)
            s_new = jnp.where(lo, s_res[:SSM_STATE], s_res[SSM_STATE:])
            keep = jnp.where(lo[0:1], e_tot[2 * p:2 * p + 1, :], e_tot[2 * p + 1:2 * p + 2, :])
            state_sc[:, cols] = state[:, cols] * keep + s_new
        return carry

    lax.fori_loop(0, n_sub, chunk, 0, unroll=True)

    @pl.when(pl.program_id(2) == pl.num_programs(2) - 1)
    def _():
        fin_ref[0] = state_sc[...]


def ssd_scan(xbc, delta, acs, init_state):
    assert SSM_STATE == SSM_CHUNK == LANES
    n = xbc.shape[0]
    q = SSM_CHUNK
    nc = n // q
    n_sub = min(nc, SSD_CHUNKS_PER_STEP)
    ns = nc // n_sub
    rows = n_sub * q
    hg = SSM_HEADS_PER_GROUP
    gw = hg * SSM_HEAD_DIM
    ac_col = acs.reshape(n, 2, SSM_GROUPS, hg).transpose(1, 2, 0, 3)
    row = lambda t: t.reshape(nc, q, 2, SSM_GROUPS, hg).transpose(2, 3, 0, 4, 1)
    ac_row, dl_row = row(acs), row(delta)
    b0 = SSM_D_INNER // SSM_STATE
    c0 = b0 + SSM_GROUPS
    step = lambda d, s: s + d * (ns - 1 - 2 * s)
    row_spec = pl.BlockSpec((1, 1, n_sub, hg, q), lambda d, g, s: (d, g, step(d, s), 0, 0))
    state_spec = pl.BlockSpec((1, SSM_STATE, gw), lambda d, g, s: (d, 0, g))
    return pl.pallas_call(
        functools.partial(_ssd_kernel, n_sub=n_sub),
        grid=(2, SSM_GROUPS, ns),
        in_specs=[
            pl.BlockSpec((rows, gw), lambda d, g, s: (step(d, s), g)),
            pl.BlockSpec((rows, SSM_STATE), lambda d, g, s: (step(d, s), b0 + g)),
            pl.BlockSpec((rows, SSM_STATE), lambda d, g, s: (step(d, s), c0 + g)),
            pl.BlockSpec((1, 1, rows, hg), lambda d, g, s: (d, g, step(d, s), 0)),
            row_spec, row_spec,
            state_spec,
        ],
        out_specs=[pl.BlockSpec((1, rows, gw), lambda d, g, s: (d, step(d, s), g)), state_spec],
        out_shape=[jax.ShapeDtypeStruct((2, n, SSM_D_INNER), BF16),
                   jax.ShapeDtypeStruct((2, SSM_STATE, SSM_D_INNER), F32)],
        scratch_shapes=[pltpu.VMEM((SSM_STATE, gw), F32)],
        compiler_params=_params("parallel", "parallel", "arbitrary"),
        name="ssd_scan",
    )(xbc, xbc, xbc, ac_col, ac_row, dl_row, init_state)


def _ssm_out_kernel(yf_ref, yb_ref, x_ref, z_ref, dsum_ref, g_ref, w_ref, h_ref, gate_ref, o_ref):
    y = yf_ref[0].astype(F32) + yb_ref[0].astype(F32) + dsum_ref[...] * x_ref[...].astype(F32)
    v = y * _silu(z_ref[...].astype(F32))
    var = jnp.mean(v * v, axis=-1, keepdims=True)
    yn = ((v * lax.rsqrt(var + EPS)) * g_ref[...]).astype(BF16)
    o_ref[...] = h_ref[...] + gate_ref[...] * _dot(yn, w_ref[...])


def ssm_out_projection(y, xbc, zxbc, dsum, norm_g, w, layer, h, gate):
    n, d = h.shape
    tl = min(n, 256)
    di = SSM_D_INNER
    row = pl.BlockSpec((1, di), lambda i: (0, 0))
    return pl.pallas_call(
        _ssm_out_kernel,
        grid=(n // tl,),
        in_specs=[
            pl.BlockSpec((1, tl, di), lambda i: (0, i, 0)),
            pl.BlockSpec((1, tl, di), lambda i: (1, i, 0)),
            pl.BlockSpec((tl, di), lambda i: (i, 0)),
            pl.BlockSpec((tl, di), lambda i: (i, 0)),
            row, row,
            pl.BlockSpec((None, di, d), lambda i: (layer, 0, 0), pipeline_mode=pl.Buffered(1)),
            pl.BlockSpec((tl, d), lambda i: (i, 0)),
            pl.BlockSpec((1, d), lambda i: (0, 0)),
        ],
        out_specs=pl.BlockSpec((tl, d), lambda i: (i, 0)),
        out_shape=jax.ShapeDtypeStruct((n, d), F32),
        compiler_params=_params("parallel"),
        name="ssm_out_projection",
    )(y, y, xbc, zxbc, dsum, norm_g, w, h, gate)


def mamba2_bidirectional(h, hc, g, mod_l, mod_c, w_in, w_out_bf16, layer, conv_w, conv_b, a_log, dt_bias,
                         d_skip, norm_g):
    n_main = SSM_D_INNER + SSM_CONV_DIM
    w_dt = w_in[layer, :, n_main:][None]
    dsum = jnp.repeat(d_skip[0] + d_skip[1], SSM_HEAD_DIM)[None]
    state = jnp.zeros((2, SSM_STATE, SSM_D_INNER), F32)
    outs = []
    for t, mod in ((hc, mod_c), (h, mod_l)):
        shift, scale, gate = mod[3:4], mod[4:5], mod[5:6]
        zxbc = norm_mod_matmul(t, g, shift, scale, w_in, layer, BF16, n=n_main)
        dt = norm_mod_matmul(t, g, shift, scale, w_dt, 0, F32, precise=True)
        xbc = conv_silu(zxbc, conv_w, conv_b[None])
        delta, acs = ssd_prep(dt, dt_bias, a_log)
        y, state = ssd_scan(xbc, delta, acs, state)
        outs.append(ssm_out_projection(y, xbc, zxbc, dsum, norm_g[None], w_out_bf16, layer, t, gate))
    return outs[1], outs[0]


def kernel(x, c, ctx, c_ctx, ada_w, ada_b, norm_g, ffn_w_in, ffn_w_out, na_w_qkv, na_rpb, na_w_o, ssm_w_in, ssm_conv_w, ssm_conv_b, ssm_a_log, ssm_dt_bias, ssm_d, ssm_norm_g, ssm_w_out, gqa_w_qkv, gqa_q_norm, gqa_k_norm, gqa_w_o, final_norm_g):
    bsz, n_tok, d = x.shape
    assert bsz == 1, "the kernels take one sequence"
    depth = ada_w.shape[0]
    h, hc = x[0], ctx[0]
    mod = adaln_mod(c, c_ctx, ada_w, ada_b)
    cos_full, sin_signed = axial_rope_tables(n_tok)
    rope_off = jnp.zeros((ctx.shape[1], LANES), F32)
    ffn_w = {(0, 0): (ffn_w_in[0, 0].astype(BF16), ffn_w_out[0, 0].astype(BF16))}
    na_w_o_b, ssm_w_out_b, gqa_w_o_b = na_w_o.astype(BF16), ssm_w_out.astype(BF16), gqa_w_o.astype(BF16)
    gqa_q_scale = LANES ** -0.5 * LOG2_E

    def ffn(t, m, i, k, latent, final_g=None):
        w_in_b, w_out_b = ffn_w[(i, k)]
        nxt = (i + (k + 1) // 2, (k + 1) % 2)
        cast_next = latent and nxt[0] < depth
        out = ffn_half_step(t, norm_g[i, 2 * k][None], m[6 * k:6 * k + 1], m[6 * k + 1:6 * k + 2],
                            m[6 * k + 2:6 * k + 3], w_in_b, w_out_b,
                            (ffn_w_in, ffn_w_out) + nxt if cast_next else None, final_g)
        if cast_next:
            out, *ffn_w[nxt] = out
        return out

    for i in range(depth):
        last = i == depth - 1
        ml, mc = mod[i, 0], mod[i, 1]
        g_mix = norm_g[i, 1][None]
        h = ffn(h, ml, i, 0, True)
        hc = ffn(hc, mc, i, 0, False)
        kind, j = i % 3, i // 3
        if kind == 0:
            qkv = norm_mod_matmul(h, g_mix, ml[3:4], ml[4:5], na_w_qkv, j, BF16)
            qkv_c = norm_mod_matmul(hc, g_mix, mc[3:4], mc[4:5], na_w_qkv, j, BF16)
            o = neighbourhood_attention(qkv, qkv_c, na_rpb[j])
            h = matmul_gated_residual(o, na_w_o_b, j, h, ml[5:6])
            if not last:
                oc = ctx_attention(qkv_c, NA_HEADS, lambda hd: hd, lambda hd: NA_HEADS + hd,
                                   lambda hd: 2 * NA_HEADS + hd)
                hc = matmul_gated_residual(oc, na_w_o_b, j, hc, mc[5:6])
        elif kind == 1:
            h, hc = mamba2_bidirectional(h, hc, g_mix, ml, mc, ssm_w_in, ssm_w_out_b, j, ssm_conv_w[j],
                                         ssm_conv_b[j], ssm_a_log[j], ssm_dt_bias[j], ssm_d[j], ssm_norm_g[j])
        else:
            qkv = gqa_projection(h, g_mix, ml[3:4], ml[4:5], gqa_w_qkv, j, gqa_q_norm[j], gqa_k_norm[j],
                                 cos_full, sin_signed, True, gqa_q_scale)
            qkv_c = gqa_projection(hc, g_mix, mc[3:4], mc[4:5], gqa_w_qkv, j, gqa_q_norm[j], gqa_k_norm[j],
                                   rope_off, rope_off, False, 1.0)
            o = gqa_attention(qkv, qkv_c)
            h = matmul_gated_residual(o, gqa_w_o_b, j, h, ml[5:6])
            if not last:
                oc = ctx_attention(qkv_c, GQA_HEADS, lambda hd: hd, lambda hd: GQA_HEADS + hd // GQA_GROUP,
                                   lambda hd: GQA_HEADS + GQA_KV_HEADS + hd // GQA_GROUP)
                hc = matmul_gated_residual(oc, gqa_w_o_b, j, hc, mc[5:6])
        h = ffn(h, ml, i, 1, True, final_norm_g[None] if last else None)
        if not last:
            hc = ffn(hc, mc, i, 1, False)
    return h[None]
```

```python
import functools

import jax
import jax.numpy as jnp
import numpy as np
from jax import lax
from jax.experimental import pallas as pl
from jax.experimental.pallas import tpu as pltpu

F32 = jnp.float32
BF16 = jnp.bfloat16

EPS = 1e-6
HALF_STEP = 0.5
N_MOD = 9
GRID_W = 64
LANES = 128
NORM_ROWS = 16

NA_HEADS = 16
NA_WIN_ROWS = 8
NA_WIN_COLS = 16
NA_Q_ROWS = 4
NA_BAND_ROWS = 12
NA_BLOCKS_PER_STEP = 4

SSM_HEADS = 64
SSM_HEAD_DIM = 64
SSM_GROUPS = 8
SSM_STATE = 128
SSM_CHUNK = 128
SSM_D_INNER = SSM_HEADS * SSM_HEAD_DIM
SSM_GN = SSM_GROUPS * SSM_STATE
SSM_CONV_W = 5
SSM_CONV_DIM = SSM_D_INNER + 2 * SSM_GN
SSM_HEADS_PER_GROUP = SSM_HEADS // SSM_GROUPS
SSD_CHUNKS_PER_STEP = 4
CONV_HALO = 16

GQA_HEADS = 16
GQA_KV_HEADS = 4
GQA_GROUP = GQA_HEADS // GQA_KV_HEADS
ROPE_THETA = 10000.0
LOG2_E = 1.4426950408889634

MASK_VALUE = -1e30
VMEM_LIMIT = 56 * 1024 * 1024


def _params(*sem):
    return pltpu.CompilerParams(dimension_semantics=sem, vmem_limit_bytes=VMEM_LIMIT)


def _silu(x):
    return x / (1.0 + jnp.exp(-x))


def _dot(a, b):
    return jnp.dot(a, b, preferred_element_type=F32)


def _dot_nt(a, b):
    return lax.dot_general(a, b, (((1,), (1,)), ((), ())), preferred_element_type=F32)


def _dot_tn(a, b):
    return lax.dot_general(a, b, (((0,), (0,)), ((), ())), preferred_element_type=F32)


def _rowwise(n_rows, chunk, body, unroll=2):
    def step(i, carry):
        body(pl.ds(pl.multiple_of(i * chunk, chunk), chunk))
        return carry

    lax.fori_loop(0, n_rows // chunk, step, 0, unroll=unroll)


def _norm_modulate_into(dst_ref, src_ref, g_ref, sh_ref, sc_ref, rs_ref):
    gain = g_ref[...] * (1.0 + sc_ref[...])
    shift = sh_ref[...]

    n_rows, d = src_ref.shape

    def stats(rows):
        x = src_ref[rows, :]
        sq = x * x
        rs_ref[rows, :] = functools.reduce(jnp.add, [sq[:, c * LANES:(c + 1) * LANES] for c in range(d // LANES)])

    def scale(rows):
        x = src_ref[rows, :]
        rs = jnp.concatenate([rs_ref[rows, :]] * (d // LANES), axis=1)
        dst_ref[rows, :] = ((x * rs) * gain + shift).astype(dst_ref.dtype)

    _rowwise(n_rows, NORM_ROWS, stats, unroll=4)
    var = jnp.sum(rs_ref[...], axis=-1, keepdims=True) * (1.0 / d)
    rs_ref[...] = jnp.broadcast_to(lax.rsqrt(var + EPS), rs_ref.shape)
    _rowwise(n_rows, NORM_ROWS, scale, unroll=2)


def _adaln_kernel(c_ref, w_ref, b_ref, o_ref, s_sc, *, rows, group):
    tk, tn = w_ref.shape
    s_sc[...] = _silu(c_ref[...])

    @pl.when(pl.program_id(2) == 0)
    def _():
        o_ref[...] = jnp.broadcast_to(b_ref[...], o_ref.shape)

    for lg in range(tn // group):
        cols = slice(lg * group, (lg + 1) * group)

        def body(i, carry):
            a0, a1 = carry
            r = pl.multiple_of(i * rows, rows)
            w = w_ref[pl.ds(r, rows), cols]
            s0 = jnp.concatenate([s_sc[0, pl.ds(r, rows), :]] * (group // LANES), axis=1)
            s1 = jnp.concatenate([s_sc[1, pl.ds(r, rows), :]] * (group // LANES), axis=1)
            return a0 + w * s0, a1 + w * s1

        zero = jnp.zeros((rows, group), F32)
        a0, a1 = lax.fori_loop(0, tk // rows, body, (zero, zero), unroll=2)
        o_ref[0:1, cols] += jnp.sum(a0, axis=0, keepdims=True)
        o_ref[1:2, cols] += jnp.sum(a1, axis=0, keepdims=True)


def adaln_mod(c, c_ctx, ada_w, ada_b):
    depth, d, n = ada_w.shape
    tk = 256
    tn = n // 4 if n % 2048 == 0 else 512
    cb = jnp.broadcast_to(jnp.stack([c[0], c_ctx])[:, :, None], (2, d, LANES))
    out = pl.pallas_call(
        functools.partial(_adaln_kernel, rows=16, group=512),
        grid=(depth, n // tn, d // tk),
        in_specs=[
            pl.BlockSpec((2, tk, LANES), lambda l, j, k: (0, k, 0)),
            pl.BlockSpec((None, tk, tn), lambda l, j, k: (l, k, j)),
            pl.BlockSpec((None, 1, tn), lambda l, j, k: (l, 0, j)),
        ],
        out_specs=pl.BlockSpec((None, 2, tn), lambda l, j, k: (l, 0, j)),
        out_shape=jax.ShapeDtypeStruct((depth, 2, n), F32),
        scratch_shapes=[pltpu.VMEM((2, tk, LANES), F32)],
        compiler_params=_params("parallel", "parallel", "arbitrary"),
        name="adaln_mod",
    )(cb, ada_w, ada_b.reshape(depth, 1, n))
    return out.reshape(depth, 2, N_MOD, d)


def _ffn_kernel(h_ref, g_ref, sh_ref, sc_ref, gate_ref, wa_ref, wb_ref, wo_ref, fg_ref, *rest, final_norm, cast_next):
    if cast_next:
        nwi_ref, nwo_ref, o_ref, nwi_out, nwo_out, xn_sc, acc_sc, rs_sc = rest
    else:
        o_ref, xn_sc, acc_sc, rs_sc = rest
    j = pl.program_id(1)

    @pl.when(j == 0)
    def _():
        _norm_modulate_into(xn_sc, h_ref, g_ref, sh_ref, sc_ref, rs_sc)
        acc_sc[...] = jnp.zeros_like(acc_sc)

    if cast_next:
        nwi_out[...] = nwi_ref[...].astype(BF16)
        nwo_out[...] = nwo_ref[...].astype(BF16)
    xn = xn_sc[...]
    a = _dot(xn, wa_ref[...])
    b = _dot(xn, wb_ref[...])
    hm = (_silu(a) * b).astype(BF16)
    acc_sc[...] += _dot(hm, wo_ref[...])

    @pl.when(j == pl.num_programs(1) - 1)
    def _():
        out = h_ref[...] + (HALF_STEP * gate_ref[...]) * acc_sc[...]
        if final_norm:
            var = jnp.mean(out * out, axis=-1, keepdims=True)
            out = (out * lax.rsqrt(var + EPS)) * fg_ref[...]
        o_ref[...] = out


def ffn_half_step(h, g, shift, scale, gate, w_in, w_out, next_w=None, final_g=None):
    m, d = h.shape
    f = w_out.shape[0]
    tm = min(m, 512)
    tf = 512
    nf = f // tf
    n_i = m // tm
    row = lambda i, j: (0, 0)
    fg = final_g if final_g is not None else g
    in_specs = [
        pl.BlockSpec((tm, d), lambda i, j: (i, 0)),
        pl.BlockSpec((1, d), row), pl.BlockSpec((1, d), row),
        pl.BlockSpec((1, d), row), pl.BlockSpec((1, d), row),
        pl.BlockSpec((d, tf), lambda i, j: (0, j)),
        pl.BlockSpec((d, tf), lambda i, j: (0, j + nf)),
        pl.BlockSpec((tf, d), lambda i, j: (j, 0)),
        pl.BlockSpec((1, d), row),
    ]
    out_specs = [pl.BlockSpec((tm, d), lambda i, j: (i, 0))]
    out_shape = [jax.ShapeDtypeStruct((m, d), F32)]
    args = [h, g, shift, scale, gate, w_in, w_in, w_out, fg]
    if next_w is not None:
        nw_in, nw_out, layer, k = next_w
        ri, ci, ro = d // n_i, 2 * f // nf, f // (n_i * nf)
        assert ri % 16 == 0 and ci % LANES == 0 and ro % 16 == 0
        in_specs += [pl.BlockSpec((None, None, ri, ci), lambda i, j: (layer, k, i, j)),
                     pl.BlockSpec((None, None, ro, d), lambda i, j: (layer, k, i * nf + j, 0))]
        out_specs += [pl.BlockSpec((ri, ci), lambda i, j: (i, j)),
                      pl.BlockSpec((ro, d), lambda i, j: (i * nf + j, 0))]
        out_shape += [jax.ShapeDtypeStruct((d, 2 * f), BF16), jax.ShapeDtypeStruct((f, d), BF16)]
        args += [nw_in, nw_out]
    outs = pl.pallas_call(
        functools.partial(_ffn_kernel, final_norm=final_g is not None, cast_next=next_w is not None),
        grid=(n_i, nf),
        in_specs=in_specs,
        out_specs=out_specs,
        out_shape=out_shape,
        scratch_shapes=[pltpu.VMEM((tm, d), BF16), pltpu.VMEM((tm, d), F32), pltpu.VMEM((tm, LANES), F32)],
        compiler_params=_params("parallel", "arbitrary"),
        name="ffn_half_step",
    )(*args)
    return outs if next_w is not None else outs[0]


def _nmm_kernel(h_ref, g_ref, sh_ref, sc_ref, w_ref, o_ref, xn_sc, rs_sc, *, precise):
    @pl.when(pl.program_id(1) == 0)
    def _():
        _norm_modulate_into(xn_sc, h_ref, g_ref, sh_ref, sc_ref, rs_sc)

    if precise:
        xf, wf = xn_sc[...], w_ref[...]
        xh, wh = xf.astype(BF16), wf.astype(BF16)
        xl, wl = (xf - xh.astype(F32)).astype(BF16), (wf - wh.astype(F32)).astype(BF16)
        y = _dot(xh, wh) + (_dot(xh, wl) + _dot(xl, wh))
    else:
        y = _dot(xn_sc[...], w_ref[...].astype(BF16))
    o_ref[...] = y.astype(o_ref.dtype)


def norm_mod_matmul(h, g, shift, scale, w, layer, out_dtype, n=None, precise=False):
    m, d = h.shape
    n = w.shape[2] if n is None else n
    tm = min(m, 1024)
    tn = n if n <= 1024 else (1024 if n % 1024 == 0 else 512)
    row = lambda i, j: (0, 0)
    return pl.pallas_call(
        functools.partial(_nmm_kernel, precise=precise),
        grid=(m // tm, n // tn),
        in_specs=[
            pl.BlockSpec((tm, d), lambda i, j: (i, 0)),
            pl.BlockSpec((1, d), row), pl.BlockSpec((1, d), row), pl.BlockSpec((1, d), row),
            pl.BlockSpec((None, d, tn), lambda i, j: (layer, 0, j)),
        ],
        out_specs=pl.BlockSpec((tm, tn), lambda i, j: (i, j)),
        out_shape=jax.ShapeDtypeStruct((m, n), out_dtype),
        scratch_shapes=[pltpu.VMEM((tm, d), F32 if precise else BF16), pltpu.VMEM((tm, LANES), F32)],
        compiler_params=_params("parallel", "arbitrary"),
        name="norm_mod_matmul",
    )(h, g, shift, scale, w)


def _mgr_kernel(a_ref, w_ref, h_ref, gate_ref, o_ref):
    o_ref[...] = h_ref[...] + gate_ref[...] * _dot(a_ref[...], w_ref[...])


def matmul_gated_residual(a, w, layer, h, gate):
    m, k = a.shape
    n = w.shape[2]
    tm = min(m, 512)
    tn = n if k * n * 2 <= 8 * 1024 * 1024 else min(n, 1024)
    return pl.pallas_call(
        _mgr_kernel,
        grid=(m // tm, n // tn),
        in_specs=[
            pl.BlockSpec((tm, k), lambda i, j: (i, 0)),
            pl.BlockSpec((None, k, tn), lambda i, j: (layer, 0, j)),
            pl.BlockSpec((tm, tn), lambda i, j: (i, j)),
            pl.BlockSpec((1, tn), lambda i, j: (0, j)),
        ],
        out_specs=pl.BlockSpec((tm, tn), lambda i, j: (i, j)),
        out_shape=jax.ShapeDtypeStruct((m, n), F32),
        compiler_params=_params("parallel", "parallel"),
        name="matmul_gated_residual",
    )(a, w, h, gate)


def _ctx_attn_kernel(q_ref, k_ref, v_ref, o_ref):
    q = q_ref[...]
    s = _dot_nt(q, k_ref[...]) * (q.shape[-1] ** -0.5)
    m = jnp.max(s, axis=-1, keepdims=True)
    p = jnp.exp(s - m)
    l = jnp.sum(p, axis=-1, keepdims=True)
    o_ref[...] = (_dot(p.astype(BF16), v_ref[...]) / l).astype(o_ref.dtype)


def ctx_attention(qkv, n_heads, q_blk, k_blk, v_blk):
    n_ctx = qkv.shape[0]
    spec = lambda f: pl.BlockSpec((n_ctx, LANES), lambda h: (0, f(h)))
    return pl.pallas_call(
        _ctx_attn_kernel,
        grid=(n_heads,),
        in_specs=[spec(q_blk), spec(k_blk), spec(v_blk)],
        out_specs=pl.BlockSpec((n_ctx, LANES), lambda h: (0, h)),
        out_shape=jax.ShapeDtypeStruct((n_ctx, n_heads * LANES), BF16),
        compiler_params=_params("parallel"),
        name="ctx_attention",
    )(qkv, qkv, qkv)


def _na_kernel(q_ref, k_ref, v_ref, kc_ref, vc_ref, pair_ref, *rest, n_rows):
    mask_refs, o_ref = rest[:-1], rest[-1]
    nq = NA_Q_ROWS * GRID_W
    n_band = NA_BAND_ROWS * GRID_W
    scale = q_ref.shape[-1] ** -0.5 * LOG2_E
    for blk, mask_ref in enumerate(mask_refs):
        r0 = (pl.program_id(1) * len(mask_refs) + blk) * NA_Q_ROWS
        b0 = jnp.clip(r0 - NA_WIN_ROWS // 2, 0, n_rows - NA_BAND_ROWS)
        start = pl.multiple_of(b0 * GRID_W, GRID_W)
        q = q_ref[blk * nq:(blk + 1) * nq, :]
        bias = jnp.concatenate([
            jnp.concatenate([
                pair_ref[0, jnp.clip(b0 + 2 * t - r0 - qi + NA_WIN_ROWS, 0, 2 * NA_WIN_ROWS - 1)]
                for t in range(NA_BAND_ROWS // 2)], axis=1)
            for qi in range(NA_Q_ROWS)], axis=0)
        keys = jnp.concatenate([k_ref[pl.ds(start, n_band), :], kc_ref[...]], axis=0)
        vals = jnp.concatenate([v_ref[pl.ds(start, n_band), :], vc_ref[...]], axis=0)
        s = _dot_nt(q, keys) * scale
        s_w = s[:, :n_band] + (bias + mask_ref[0])
        s_c = s[:, n_band:]
        m = jnp.maximum(jnp.max(s_w, axis=-1, keepdims=True), jnp.max(s_c, axis=-1, keepdims=True))
        p = jnp.concatenate([jnp.exp2(s_w - m), jnp.exp2(s_c - m)], axis=1)
        l = jnp.sum(p, axis=-1, keepdims=True)
        o_ref[blk * nq:(blk + 1) * nq, :] = (_dot(p.astype(BF16), vals) / l).astype(o_ref.dtype)


def _na_window_masks(n_rows):
    nq, nk = NA_Q_ROWS * GRID_W, NA_BAND_ROWS * GRID_W
    valid = []
    for r0 in (0, NA_Q_ROWS, n_rows - NA_Q_ROWS):
        b0 = int(np.clip(r0 - NA_WIN_ROWS // 2, 0, n_rows - NA_BAND_ROWS))
        qr = r0 + np.arange(nq) // GRID_W
        qc = np.arange(nq) % GRID_W
        kr = b0 + np.arange(nk) // GRID_W
        kc = np.arange(nk) % GRID_W
        rs = np.clip(qr - NA_WIN_ROWS // 2, 0, n_rows - NA_WIN_ROWS)
        cs = np.clip(qc - NA_WIN_COLS // 2, 0, GRID_W - NA_WIN_COLS)
        ok = ((kr[None] >= rs[:, None]) & (kr[None] < rs[:, None] + NA_WIN_ROWS)
              & (kc[None] >= cs[:, None]) & (kc[None] < cs[:, None] + NA_WIN_COLS))
        valid.append(ok)
    return np.where(np.stack(valid), 0.0, MASK_VALUE).astype(np.float32)


def _na_pair_tiles(rpb):
    n_heads = rpb.shape[0]
    dcol = np.arange(GRID_W)[None, :] - np.arange(GRID_W)[:, None] + NA_WIN_COLS - 1
    in_table = (dcol >= 0) & (dcol < 2 * NA_WIN_COLS - 1)
    onehot = (np.clip(dcol, 0, 2 * NA_WIN_COLS - 2)[None] == np.arange(2 * NA_WIN_COLS - 1)[:, None, None]) & in_table
    toeplitz = jnp.einsum("hrd,dqk->hrqk", rpb, jnp.asarray(onehot, F32), precision=lax.Precision.HIGHEST)
    zero = jnp.zeros((n_heads, 1, GRID_W, GRID_W), F32)
    padded = jnp.concatenate([zero, toeplitz, zero], axis=1)
    return jnp.concatenate([padded[:, :-1], padded[:, 1:]], axis=-1)


def neighbourhood_attention(qkv, qkv_c, rpb):
    n_tok = qkv.shape[0]
    n_ctx = qkv_c.shape[0]
    n_rows = n_tok // GRID_W
    n_blocks = n_rows // NA_Q_ROWS
    nb = NA_BLOCKS_PER_STEP
    nq, nk = NA_Q_ROWS * GRID_W, NA_BAND_ROWS * GRID_W
    assert NA_BAND_ROWS % 2 == 0
    pairs = _na_pair_tiles(rpb) * LOG2_E
    masks = jnp.asarray(_na_window_masks(n_rows))
    h_ = NA_HEADS

    def mask_spec(blk):
        def index(h, s):
            rb = s * nb + blk
            return (jnp.where(rb == 0, 0, jnp.where(rb == n_blocks - 1, 2, 1)), 0, 0)
        return pl.BlockSpec((1, nq, nk), index)

    return pl.pallas_call(
        functools.partial(_na_kernel, n_rows=n_rows),
        grid=(h_, n_blocks // nb),
        in_specs=[
            pl.BlockSpec((nb * nq, LANES), lambda h, s: (s, h)),
            pl.BlockSpec((n_tok, LANES), lambda h, s: (0, h_ + h)),
            pl.BlockSpec((n_tok, LANES), lambda h, s: (0, 2 * h_ + h)),
            pl.BlockSpec((n_ctx, LANES), lambda h, s: (0, h_ + h)),
            pl.BlockSpec((n_ctx, LANES), lambda h, s: (0, 2 * h_ + h)),
            pl.BlockSpec((1,) + pairs.shape[1:], lambda h, s: (h, 0, 0, 0)),
        ] + [mask_spec(blk) for blk in range(nb)],
        out_specs=pl.BlockSpec((nb * nq, LANES), lambda h, s: (s, h)),
        out_shape=jax.ShapeDtypeStruct((n_tok, h_ * LANES), BF16),
        compiler_params=_params("parallel", "arbitrary"),
        name="neighbourhood_attention",
    )(qkv, qkv, qkv, qkv_c, qkv_c, pairs, *([masks] * nb))


def _gqa_proj_kernel(h_ref, g_ref, sh_ref, sc_ref, w_ref, qn_ref, kn_ref, cos_ref, sin_ref, o_ref, xn_sc,
                     rs_sc, *, rope, q_scale):
    j = pl.program_id(1)
    half = o_ref.shape[1] // 2
    n_q = GQA_HEADS * LANES // o_ref.shape[1]

    @pl.when(j == 0)
    def _():
        _norm_modulate_into(xn_sc, h_ref, g_ref, sh_ref, sc_ref, rs_sc)

    def project(hf):
        return _dot(xn_sc[...], w_ref[:, hf * half:(hf + 1) * half].astype(BF16))

    def normed(y, hf, gain):
        for s in range(half // LANES):
            x = y[:, s * LANES:(s + 1) * LANES]
            var = jnp.mean(x * x, axis=-1, keepdims=True)
            t = (x * lax.rsqrt(var + EPS)) * gain
            if rope:
                t = t * cos_ref[...] + pltpu.roll(t, LANES // 2, 1) * sin_ref[...]
            o_ref[:, hf * half + s * LANES:hf * half + (s + 1) * LANES] = t.astype(o_ref.dtype)

    @pl.when(j < n_q)
    def _():
        y0, y1 = project(0), project(1)
        gain = qn_ref[...] * q_scale
        normed(y0, 0, gain)
        normed(y1, 1, gain)

    @pl.when(j == n_q)
    def _():
        y0, y1 = project(0), project(1)
        normed(y0, 0, kn_ref[...])
        o_ref[:, half:] = y1.astype(o_ref.dtype)


def gqa_projection(h, g, shift, scale, w, layer, q_norm, k_norm, cos_full, sin_signed, rope, q_scale):
    m, d = h.shape
    n = w.shape[2]
    tm = min(m, 1024)
    tn = 2 * GQA_KV_HEADS * LANES
    assert (GQA_HEADS * LANES) % tn == 0 and n == GQA_HEADS * LANES + tn
    row = lambda i, j: (0, 0)
    head = pl.BlockSpec((1, LANES), row)
    pos = pl.BlockSpec((tm, LANES), lambda i, j: (i, 0))
    return pl.pallas_call(
        functools.partial(_gqa_proj_kernel, rope=rope, q_scale=q_scale),
        grid=(m // tm, n // tn),
        in_specs=[
            pl.BlockSpec((tm, d), lambda i, j: (i, 0)),
            pl.BlockSpec((1, d), row), pl.BlockSpec((1, d), row), pl.BlockSpec((1, d), row),
            pl.BlockSpec((None, d, tn), lambda i, j: (layer, 0, j)),
            head, head, pos, pos,
        ],
        out_specs=pl.BlockSpec((tm, tn), lambda i, j: (i, j)),
        out_shape=jax.ShapeDtypeStruct((m, n), BF16),
        scratch_shapes=[pltpu.VMEM((tm, d), BF16), pltpu.VMEM((tm, LANES), F32)],
        compiler_params=_params("parallel", "arbitrary"),
        name="gqa_projection",
    )(h, g, shift, scale, w, q_norm[None], k_norm[None], cos_full, sin_signed)


def _gqa_kernel(q_ref, k_ref, v_ref, kc_ref, vc_ref, o_ref, m_sc, l_sc, acc_sc, *, tk):
    tq = q_ref.shape[0]
    n_tok = k_ref.shape[0]
    q = jnp.concatenate([q_ref[:, g * LANES:(g + 1) * LANES] for g in range(GQA_GROUP)], axis=0)
    m_sc[...] = jnp.full_like(m_sc, -jnp.inf)
    l_sc[...] = jnp.zeros_like(l_sc)
    acc_sc[...] = jnp.zeros_like(acc_sc)

    def update(kb, vb):
        s = _dot_nt(q, kb)
        blocks = [s[:, c * LANES:(c + 1) * LANES] for c in range(kb.shape[0] // LANES)]
        m_old = m_sc[...]
        block_max = jnp.max(functools.reduce(jnp.maximum, blocks), axis=-1, keepdims=True)
        m_new = jnp.maximum(m_old, block_max)
        alpha = jnp.exp2(m_old - m_new)
        p = [jnp.exp2(b - m_new) for b in blocks]
        l_sc[...] = alpha * l_sc[...] + functools.reduce(jnp.add, p)
        acc_sc[...] = alpha * acc_sc[...] + _dot(jnp.concatenate(p, axis=-1).astype(BF16), vb)
        m_sc[...] = m_new

    def body(j, carry):
        r = pl.multiple_of(j * tk, tk)
        update(k_ref[pl.ds(r, tk), :], v_ref[pl.ds(r, tk), :])
        return carry

    lax.fori_loop(0, n_tok // tk, body, 0, unroll=4)
    update(kc_ref[...], vc_ref[...])
    o = acc_sc[...] / jnp.sum(l_sc[...], axis=-1, keepdims=True)
    for g in range(GQA_GROUP):
        o_ref[:, g * LANES:(g + 1) * LANES] = o[g * tq:(g + 1) * tq].astype(o_ref.dtype)


def gqa_attention(qkv, qkv_c):
    n_tok = qkv.shape[0]
    n_ctx = qkv_c.shape[0]
    tq = min(n_tok, 512)
    tk = min(n_tok, 1024)
    kv0 = GQA_HEADS
    v0 = GQA_HEADS + GQA_KV_HEADS
    rows = GQA_GROUP * tq
    return pl.pallas_call(
        functools.partial(_gqa_kernel, tk=tk),
        grid=(GQA_KV_HEADS, n_tok // tq),
        in_specs=[
            pl.BlockSpec((tq, GQA_GROUP * LANES), lambda kv, i: (i, kv)),
            pl.BlockSpec((n_tok, LANES), lambda kv, i: (0, kv0 + kv)),
            pl.BlockSpec((n_tok, LANES), lambda kv, i: (0, v0 + kv)),
            pl.BlockSpec((n_ctx, LANES), lambda kv, i: (0, kv0 + kv)),
            pl.BlockSpec((n_ctx, LANES), lambda kv, i: (0, v0 + kv)),
        ],
        out_specs=pl.BlockSpec((tq, GQA_GROUP * LANES), lambda kv, i: (i, kv)),
        out_shape=jax.ShapeDtypeStruct((n_tok, GQA_HEADS * LANES), BF16),
        scratch_shapes=[pltpu.VMEM((rows, LANES), F32)] * 3,
        compiler_params=_params("parallel", "arbitrary"),
        name="gqa_attention",
    )(qkv, qkv, qkv, qkv_c, qkv_c)


def axial_rope_tables(n_tok):
    t = jnp.arange(n_tok, dtype=jnp.int32)
    row = (t // GRID_W).astype(F32)
    col = (t % GRID_W).astype(F32)
    n_freq = LANES // 4
    inv_freq = ROPE_THETA ** (-jnp.arange(n_freq, dtype=F32) / n_freq)
    ang = jnp.concatenate([row[:, None] * inv_freq, col[:, None] * inv_freq], axis=-1)
    cos, sin = jnp.cos(ang), jnp.sin(ang)
    return jnp.concatenate([cos, cos], axis=-1), jnp.concatenate([-sin, sin], axis=-1)


def _conv_silu_kernel(prev_ref, cur_ref, next_ref, w_ref, b_ref, o_ref):
    i = pl.program_id(0)
    tl = cur_ref.shape[0]
    prev = jnp.where(i == 0, 0.0, prev_ref[...].astype(F32))
    nxt = jnp.where(i == pl.num_programs(0) - 1, 0.0, next_ref[...].astype(F32))
    ext = jnp.concatenate([prev, cur_ref[...].astype(F32), nxt], axis=0)
    n_ext = ext.shape[0]
    pad = (SSM_CONV_W - 1) // 2
    acc = jnp.zeros((tl, ext.shape[1]), F32) + b_ref[...]
    for k in range(SSM_CONV_W):
        shifted = ext if k == pad else pltpu.roll(ext, (pad - k) % n_ext, 0)
        acc = acc + shifted[CONV_HALO:CONV_HALO + tl] * w_ref[k:k + 1, :]
    o_ref[...] = _silu(acc).astype(o_ref.dtype)


def conv_silu(zxbc, conv_w, conv_b):
    n = zxbc.shape[0]
    tl = min(n, 512)
    tc = 512
    c0 = SSM_D_INNER // tc
    hb = tl // CONV_HALO
    n_halo = n // CONV_HALO
    return pl.pallas_call(
        _conv_silu_kernel,
        grid=(n // tl, SSM_CONV_DIM // tc),
        in_specs=[
            pl.BlockSpec((CONV_HALO, tc), lambda i, j: (jnp.maximum(i * hb - 1, 0), c0 + j)),
            pl.BlockSpec((tl, tc), lambda i, j: (i, c0 + j)),
            pl.BlockSpec((CONV_HALO, tc), lambda i, j: (jnp.minimum((i + 1) * hb, n_halo - 1), c0 + j)),
            pl.BlockSpec((SSM_CONV_W, tc), lambda i, j: (0, j)),
            pl.BlockSpec((1, tc), lambda i, j: (0, j)),
        ],
        out_specs=pl.BlockSpec((tl, tc), lambda i, j: (i, j)),
        out_shape=jax.ShapeDtypeStruct((n, SSM_CONV_DIM), BF16),
        compiler_params=_params("parallel", "parallel"),
        name="conv_silu",
    )(zxbc, zxbc, zxbc, conv_w, conv_b)


def _ssd_prep_kernel(dt_ref, bias_ref, alog_ref, delta_ref, acs_ref):
    x = dt_ref[...] + bias_ref[...]
    delta = jnp.maximum(x, 0.0) + jnp.log1p(jnp.exp(-jnp.abs(x)))
    da = delta * (-jnp.exp(alog_ref[...]))
    q = x.shape[0]
    i = lax.broadcasted_iota(jnp.int32, (q, q), 0)
    j = lax.broadcasted_iota(jnp.int32, (q, q), 1)
    hi = lax.Precision.HIGHEST
    fwd = jnp.dot((i >= j).astype(F32), da, preferred_element_type=F32, precision=hi)
    bwd = jnp.dot((i <= j).astype(F32), da, preferred_element_type=F32, precision=hi)
    lane = lax.broadcasted_iota(jnp.int32, da.shape, 1)
    delta_ref[...] = delta
    acs_ref[...] = jnp.where(lane < SSM_HEADS, fwd, bwd)


def ssd_prep(dt, dt_bias, a_log):
    n = dt.shape[0]
    q = SSM_CHUNK
    blk = pl.BlockSpec((q, 2 * SSM_HEADS), lambda c: (c, 0))
    row = pl.BlockSpec((1, 2 * SSM_HEADS), lambda c: (0, 0))
    return pl.pallas_call(
        _ssd_prep_kernel,
        grid=(n // q,),
        in_specs=[blk, row, row],
        out_specs=[blk, blk],
        out_shape=[jax.ShapeDtypeStruct(dt.shape, F32)] * 2,
        compiler_params=_params("parallel"),
        name="ssd_prep",
    )(dt, dt_bias.reshape(1, -1), a_log.reshape(1, -1))


def _ssd_kernel(x_ref, b_ref, c_ref, ac_ref, ar_ref, dr_ref, init_ref, y_ref, fin_ref, state_sc, *, n_sub):
    d = pl.program_id(0)
    q = SSM_CHUNK

    @pl.when(pl.program_id(2) == 0)
    def _():
        state_sc[...] = init_ref[0]

    i = lax.broadcasted_iota(jnp.int32, (q, q), 0)
    j = lax.broadcasted_iota(jnp.int32, (q, q), 1)
    causal = jnp.where(d == 0, i - j, j - i) >= 0
    lo = lax.broadcasted_iota(jnp.int32, (q, LANES), 1) < SSM_HEAD_DIM

    def chunk(k, carry):
        ci = k + d * (n_sub - 1 - 2 * k)
        r = pl.multiple_of(ci * q, q)
        bm = b_ref[pl.ds(r, q), :]
        cm = c_ref[pl.ds(r, q), :]
        ac = ac_ref[0, 0, pl.ds(r, q), :]
        ar = ar_ref[0, 0, ci]
        dr = dr_ref[0, 0, ci]
        tot = jnp.where(d == 0, ar[:, q - 1:q], ar[:, 0:1])
        w_out = jnp.exp(tot - ar) * dr
        e_tot = jnp.exp(tot)
        cb = _dot_nt(cm, bm)
        bt = bm.astype(F32).T
        cf = cm.astype(F32)
        state = state_sc[...]
        for p in range(SSM_HEADS_PER_GROUP // 2):
            cols = slice(p * LANES, (p + 1) * LANES)
            x = x_ref[pl.ds(r, q), cols]
            lhs_y, lhs_s = [], []
            for hd in (2 * p, 2 * p + 1):
                a_i = jnp.broadcast_to(ac[:, hd:hd + 1], (q, q))
                seg = jnp.where(causal, a_i - ar[hd:hd + 1, :], -jnp.inf)
                intra = cb * jnp.exp(seg) * dr[hd:hd + 1, :]
                lhs_y.append(jnp.concatenate([intra, cf * jnp.exp(a_i)], axis=1).astype(BF16))
                lhs_s.append((bt * w_out[hd:hd + 1, :]).astype(BF16))
            rhs = jnp.concatenate([x, state[:, cols].astype(BF16)], axis=0)
            res = _dot(jnp.concatenate(lhs_y, axis=0), rhs)
            y_ref[0, pl.ds(r, q), cols] = jnp.where(lo, res[:q], res[q:]).astype(y_ref.dtype)
            s_res = _dot(jnp.concatenate(lhs_s, axis=0), x)
            s_new = jnp.where(lo, s_res[:SSM_STATE], s_res[SSM_STATE:])
            keep = jnp.where(lo[0:1], e_tot[2 * p:2 * p + 1, :], e_tot[2 * p + 1:2 * p + 2, :])
            state_sc[:, cols] = state[:, cols] * keep + s_new
        return carry

    lax.fori_loop(0, n_sub, chunk, 0, unroll=True)

    @pl.when(pl.program_id(2) == pl.num_programs(2) - 1)
    def _():
        fin_ref[0] = state_sc[...]


def ssd_scan(xbc, delta, acs, init_state):
    assert SSM_STATE == SSM_CHUNK == LANES
    n = xbc.shape[0]
    q = SSM_CHUNK
    nc = n // q
    n_sub = min(nc, SSD_CHUNKS_PER_STEP)
    ns = nc // n_sub
    rows = n_sub * q
    hg = SSM_HEADS_PER_GROUP
    gw = hg * SSM_HEAD_DIM
    ac_col = acs.reshape(n, 2, SSM_GROUPS, hg).transpose(1, 2, 0, 3)
    row = lambda t: t.reshape(nc, q, 2, SSM_GROUPS, hg).transpose(2, 3, 0, 4, 1)
    ac_row, dl_row = row(acs), row(delta)
    b0 = SSM_D_INNER // SSM_STATE
    c0 = b0 + SSM_GROUPS
    step = lambda d, s: s + d * (ns - 1 - 2 * s)
    row_spec = pl.BlockSpec((1, 1, n_sub, hg, q), lambda d, g, s: (d, g, step(d, s), 0, 0))
    state_spec = pl.BlockSpec((1, SSM_STATE, gw), lambda d, g, s: (d, 0, g))
    return pl.pallas_call(
        functools.partial(_ssd_kernel, n_sub=n_sub),
        grid=(2, SSM_GROUPS, ns),
        in_specs=[
            pl.BlockSpec((rows, gw), lambda d, g, s: (step(d, s), g)),
            pl.BlockSpec((rows, SSM_STATE), lambda d, g, s: (step(d, s), b0 + g)),
            pl.BlockSpec((rows, SSM_STATE), lambda d, g, s: (step(d, s), c0 + g)),
            pl.BlockSpec((1, 1, rows, hg), lambda d, g, s: (d, g, step(d, s), 0)),
            row_spec, row_spec,
            state_spec,
        ],
        out_specs=[pl.BlockSpec((1, rows, gw), lambda d, g, s: (d, step(d, s), g)), state_spec],
        out_shape=[jax.ShapeDtypeStruct((2, n, SSM_D_INNER), BF16),
                   jax.ShapeDtypeStruct((2, SSM_STATE, SSM_D_INNER), F32)],
        scratch_shapes=[pltpu.VMEM((SSM_STATE, gw), F32)],
        compiler_params=_params("parallel", "parallel", "arbitrary"),
        name="ssd_scan",
    )(xbc, xbc, xbc, ac_col, ac_row, dl_row, init_state)


def _ssm_out_kernel(yf_ref, yb_ref, x_ref, z_ref, dsum_ref, g_ref, w_ref, h_ref, gate_ref, o_ref):
    y = yf_ref[0].astype(F32) + yb_ref[0].astype(F32) + dsum_ref[...] * x_ref[...].astype(F32)
    v = y * _silu(z_ref[...].astype(F32))
    var = jnp.mean(v * v, axis=-1, keepdims=True)
    yn = ((v * lax.rsqrt(var + EPS)) * g_ref[...]).astype(BF16)
    o_ref[...] = h_ref[...] + gate_ref[...] * _dot(yn, w_ref[...])


def ssm_out_projection(y, xbc, zxbc, dsum, norm_g, w, layer, h, gate):
    n, d = h.shape
    tl = min(n, 256)
    di = SSM_D_INNER
    row = pl.BlockSpec((1, di), lambda i: (0, 0))
    return pl.pallas_call(
        _ssm_out_kernel,
        grid=(n // tl,),
        in_specs=[
            pl.BlockSpec((1, tl, di), lambda i: (0, i, 0)),
            pl.BlockSpec((1, tl, di), lambda i: (1, i, 0)),
            pl.BlockSpec((tl, di), lambda i: (i, 0)),
            pl.BlockSpec((tl, di), lambda i: (i, 0)),
            row, row,
            pl.BlockSpec((None, di, d), lambda i: (layer, 0, 0), pipeline_mode=pl.Buffered(1)),
            pl.BlockSpec((tl, d), lambda i: (i, 0)),
            pl.BlockSpec((1, d), lambda i: (0, 0)),
        ],
        out_specs=pl.BlockSpec((tl, d), lambda i: (i, 0)),
        out_shape=jax.ShapeDtypeStruct((n, d), F32),
        compiler_params=_params("parallel"),
        name="ssm_out_projection",
    )(y, y, xbc, zxbc, dsum, norm_g, w, h, gate)


def mamba2_bidirectional(h, hc, g, mod_l, mod_c, w_in, w_out_bf16, layer, conv_w, conv_b, a_log, dt_bias,
                         d_skip, norm_g):
    n_main = SSM_D_INNER + SSM_CONV_DIM
    w_dt = w_in[layer, :, n_main:][None]
    dsum = jnp.repeat(d_skip[0] + d_skip[1], SSM_HEAD_DIM)[None]
    state = jnp.zeros((2, SSM_STATE, SSM_D_INNER), F32)
    outs = []
    for t, mod in ((hc, mod_c), (h, mod_l)):
        shift, scale, gate = mod[3:4], mod[4:5], mod[5:6]
        zxbc = norm_mod_matmul(t, g, shift, scale, w_in, layer, BF16, n=n_main)
        dt = norm_mod_matmul(t, g, shift, scale, w_dt, 0, F32, precise=True)
        xbc = conv_silu(zxbc, conv_w, conv_b[None])
        delta, acs = ssd_prep(dt, dt_bias, a_log)
        y, state = ssd_scan(xbc, delta, acs, state)
        outs.append(ssm_out_projection(y, xbc, zxbc, dsum, norm_g[None], w_out_bf16, layer, t, gate))
    return outs[1], outs[0]


def kernel(x, c, ctx, c_ctx, ada_w, ada_b, norm_g, ffn_w_in, ffn_w_out, na_w_qkv, na_rpb, na_w_o, ssm_w_in, ssm_conv_w, ssm_conv_b, ssm_a_log, ssm_dt_bias, ssm_d, ssm_norm_g, ssm_w_out, gqa_w_qkv, gqa_q_norm, gqa_k_norm, gqa_w_o, final_norm_g):
    bsz, n_tok, d = x.shape
    assert bsz == 1, "the kernels take one sequence"
    depth = ada_w.shape[0]
    h, hc = x[0], ctx[0]
    mod = adaln_mod(c, c_ctx, ada_w, ada_b)
    cos_full, sin_signed = axial_rope_tables(n_tok)
    rope_off = jnp.zeros((ctx.shape[1], LANES), F32)
    ffn_w = {(0, 0): (ffn_w_in[0, 0].astype(BF16), ffn_w_out[0, 0].astype(BF16))}
    na_w_o_b, ssm_w_out_b, gqa_w_o_b = na_w_o.astype(BF16), ssm_w_out.astype(BF16), gqa_w_o.astype(BF16)
    gqa_q_scale = LANES ** -0.5 * LOG2_E

    def ffn(t, m, i, k, latent, final_g=None):
        w_in_b, w_out_b = ffn_w[(i, k)]
        nxt = (i + (k + 1) // 2, (k + 1) % 2)
        cast_next = latent and nxt[0] < depth
        out = ffn_half_step(t, norm_g[i, 2 * k][None], m[6 * k:6 * k + 1], m[6 * k + 1:6 * k + 2],
                            m[6 * k + 2:6 * k + 3], w_in_b, w_out_b,
                            (ffn_w_in, ffn_w_out) + nxt if cast_next else None, final_g)
        if cast_next:
            out, *ffn_w[nxt] = out
        return out

    for i in range(depth):
        last = i == depth - 1
        ml, mc = mod[i, 0], mod[i, 1]
        g_mix = norm_g[i, 1][None]
        h = ffn(h, ml, i, 0, True)
        hc = ffn(hc, mc, i, 0, False)
        kind, j = i % 3, i // 3
        if kind == 0:
            qkv = norm_mod_matmul(h, g_mix, ml[3:4], ml[4:5], na_w_qkv, j, BF16)
            qkv_c = norm_mod_matmul(hc, g_mix, mc[3:4], mc[4:5], na_w_qkv, j, BF16)
            o = neighbourhood_attention(qkv, qkv_c, na_rpb[j])
            h = matmul_gated_residual(o, na_w_o_b, j, h, ml[5:6])
            if not last:
                oc = ctx_attention(qkv_c, NA_HEADS, lambda hd: hd, lambda hd: NA_HEADS + hd,
                                   lambda hd: 2 * NA_HEADS + hd)
                hc = matmul_gated_residual(oc, na_w_o_b, j, hc, mc[5:6])
        elif kind == 1:
            h, hc = mamba2_bidirectional(h, hc, g_mix, ml, mc, ssm_w_in, ssm_w_out_b, j, ssm_conv_w[j],
                                         ssm_conv_b[j], ssm_a_log[j], ssm_dt_bias[j], ssm_d[j], ssm_norm_g[j])
        else:
            qkv = gqa_projection(h, g_mix, ml[3:4], ml[4:5], gqa_w_qkv, j, gqa_q_norm[j], gqa_k_norm[j],
                                 cos_full, sin_signed, True, gqa_q_scale)
            qkv_c = gqa_projection(hc, g_mix, mc[3:4], mc[4:5], gqa_w_qkv, j, gqa_q_norm[j], gqa_k_norm[j],
                                   rope_off, rope_off, False, 1.0)
            o = gqa_attention(qkv, qkv_c)
            h = matmul_gated_residual(o, gqa_w_o_b, j, h, ml[5:6])
            if not last:
                oc = ctx_attention(qkv_c, GQA_HEADS, lambda hd: hd, lambda hd: GQA_HEADS + hd // GQA_GROUP,
                                   lambda hd: GQA_HEADS + GQA_KV_HEADS + hd // GQA_GROUP)
                hc = matmul_gated_residual(oc, gqa_w_o_b, j, hc, mc[5:6])
        h = ffn(h, ml, i, 1, True, final_norm_g[None] if last else None)
        if not last:
            hc = ffn(hc, mc, i, 1, False)
    return h[None]
```

```python
import functools

import jax
import jax.numpy as jnp
import numpy as np
from jax import lax
from jax.experimental import pallas as pl
from jax.experimental.pallas import tpu as pltpu

F32 = jnp.float32
BF16 = jnp.bfloat16

EPS = 1e-6
HALF_STEP = 0.5
N_MOD = 9
GRID_W = 64
LANES = 128
NORM_ROWS = 16

NA_HEADS = 16
NA_WIN_ROWS = 8
NA_WIN_COLS = 16
NA_Q_ROWS = 4
NA_BAND_ROWS = 12
NA_BLOCKS_PER_STEP = 4

SSM_HEADS = 64
SSM_HEAD_DIM = 64
SSM_GROUPS = 8
SSM_STATE = 128
SSM_CHUNK = 128
SSM_D_INNER = SSM_HEADS * SSM_HEAD_DIM
SSM_GN = SSM_GROUPS * SSM_STATE
SSM_CONV_W = 5
SSM_CONV_DIM = SSM_D_INNER + 2 * SSM_GN
SSM_HEADS_PER_GROUP = SSM_HEADS // SSM_GROUPS
SSD_CHUNKS_PER_STEP = 8
CONV_HALO = 16

GQA_HEADS = 16
GQA_KV_HEADS = 4
GQA_GROUP = GQA_HEADS // GQA_KV_HEADS
ROPE_THETA = 10000.0
LOG2_E = 1.4426950408889634

MASK_VALUE = -1e30
VMEM_LIMIT = 56 * 1024 * 1024


def _params(*sem):
    return pltpu.CompilerParams(dimension_semantics=sem, vmem_limit_bytes=VMEM_LIMIT)


def _silu(x):
    half = 0.5 * x
    return half + half * jnp.tanh(half)


def _dot(a, b):
    return jnp.dot(a, b, preferred_element_type=F32)


def _dot_nt(a, b):
    return lax.dot_general(a, b, (((1,), (1,)), ((), ())), preferred_element_type=F32)


def _dot_tn(a, b):
    return lax.dot_general(a, b, (((0,), (0,)), ((), ())), preferred_element_type=F32)


def _rowwise(n_rows, chunk, body, unroll=2):
    def step(i, carry):
        body(pl.ds(pl.multiple_of(i * chunk, chunk), chunk))
        return carry

    lax.fori_loop(0, n_rows // chunk, step, 0, unroll=unroll)


def _norm_modulate_into(dst_ref, src_ref, g_ref, sh_ref, sc_ref, rs_ref):
    gain = g_ref[...] * (1.0 + sc_ref[...])
    shift = sh_ref[...]

    n_rows, d = src_ref.shape

    def stats(rows):
        x = src_ref[rows, :]
        sq = x * x
        rs_ref[rows, :] = functools.reduce(jnp.add, [sq[:, c * LANES:(c + 1) * LANES] for c in range(d // LANES)])

    def scale(rows):
        x = src_ref[rows, :]
        rs = jnp.concatenate([rs_ref[rows, :]] * (d // LANES), axis=1)
        dst_ref[rows, :] = ((x * rs) * gain + shift).astype(dst_ref.dtype)

    _rowwise(n_rows, NORM_ROWS, stats, unroll=4)
    var = jnp.sum(rs_ref[...], axis=-1, keepdims=True) * (1.0 / d)
    rs_ref[...] = jnp.broadcast_to(lax.rsqrt(var + EPS), rs_ref.shape)
    _rowwise(n_rows, NORM_ROWS, scale, unroll=2)


def _adaln_kernel(c_ref, w_ref, b_ref, o_ref, s_sc, *, rows, group):
    tk, tn = w_ref.shape
    s_sc[...] = _silu(c_ref[...])

    @pl.when(pl.program_id(2) == 0)
    def _():
        o_ref[...] = jnp.broadcast_to(b_ref[...], o_ref.shape)

    for lg in range(tn // group):
        cols = slice(lg * group, (lg + 1) * group)

        def body(i, carry):
            a0, a1 = carry
            r = pl.multiple_of(i * rows, rows)
            w = w_ref[pl.ds(r, rows), cols]
            s0 = jnp.concatenate([s_sc[0, pl.ds(r, rows), :]] * (group // LANES), axis=1)
            s1 = jnp.concatenate([s_sc[1, pl.ds(r, rows), :]] * (group // LANES), axis=1)
            return a0 + w * s0, a1 + w * s1

        zero = jnp.zeros((rows, group), F32)
        a0, a1 = lax.fori_loop(0, tk // rows, body, (zero, zero), unroll=2)
        o_ref[0:1, cols] += jnp.sum(a0, axis=0, keepdims=True)
        o_ref[1:2, cols] += jnp.sum(a1, axis=0, keepdims=True)


def adaln_mod(c, c_ctx, ada_w, ada_b):
    depth, d, n = ada_w.shape
    tk = 256
    tn = n // 4 if n % 2048 == 0 else 512
    cb = jnp.broadcast_to(jnp.stack([c[0], c_ctx])[:, :, None], (2, d, LANES))
    out = pl.pallas_call(
        functools.partial(_adaln_kernel, rows=16, group=512),
        grid=(depth, n // tn, d // tk),
        in_specs=[
            pl.BlockSpec((2, tk, LANES), lambda l, j, k: (0, k, 0)),
            pl.BlockSpec((None, tk, tn), lambda l, j, k: (l, k, j)),
            pl.BlockSpec((None, 1, tn), lambda l, j, k: (l, 0, j)),
        ],
        out_specs=pl.BlockSpec((None, 2, tn), lambda l, j, k: (l, 0, j)),
        out_shape=jax.ShapeDtypeStruct((depth, 2, n), F32),
        scratch_shapes=[pltpu.VMEM((2, tk, LANES), F32)],
        compiler_params=_params("parallel", "parallel", "arbitrary"),
        name="adaln_mod",
    )(cb, ada_w, ada_b.reshape(depth, 1, n))
    return out.reshape(depth, 2, N_MOD, d)


def _ffn_kernel(h_ref, g_ref, sh_ref, sc_ref, gate_ref, wa_ref, wb_ref, wo_ref, fg_ref, *rest, final_norm, cast_next):
    if cast_next:
        nwi_ref, nwo_ref, o_ref, nwi_out, nwo_out, xn_sc, acc_sc, rs_sc = rest
    else:
        o_ref, xn_sc, acc_sc, rs_sc = rest
    j = pl.program_id(1)

    @pl.when(j == 0)
    def _():
        _norm_modulate_into(xn_sc, h_ref, g_ref, sh_ref, sc_ref, rs_sc)
        acc_sc[...] = jnp.zeros_like(acc_sc)

    if cast_next:
        nwi_out[...] = nwi_ref[...].astype(BF16)
        nwo_out[...] = nwo_ref[...].astype(BF16)
    xn = xn_sc[...]
    a = _dot(xn, wa_ref[...])
    b = _dot(xn, wb_ref[...])
    hm = (_silu(a) * b).astype(BF16)
    acc_sc[...] += _dot(hm, wo_ref[...])

    @pl.when(j == pl.num_programs(1) - 1)
    def _():
        out = h_ref[...] + (HALF_STEP * gate_ref[...]) * acc_sc[...]
        if final_norm:
            var = jnp.mean(out * out, axis=-1, keepdims=True)
            out = (out * lax.rsqrt(var + EPS)) * fg_ref[...]
        o_ref[...] = out


def ffn_half_step(h, g, shift, scale, gate, w_in, w_out, next_w=None, final_g=None):
    m, d = h.shape
    f = w_out.shape[0]
    tm = min(m, 512)
    tf = 512
    nf = f // tf
    n_i = m // tm
    row = lambda i, j: (0, 0)
    fg = final_g if final_g is not None else g
    in_specs = [
        pl.BlockSpec((tm, d), lambda i, j: (i, 0)),
        pl.BlockSpec((1, d), row), pl.BlockSpec((1, d), row),
        pl.BlockSpec((1, d), row), pl.BlockSpec((1, d), row),
        pl.BlockSpec((d, tf), lambda i, j: (0, j)),
        pl.BlockSpec((d, tf), lambda i, j: (0, j + nf)),
        pl.BlockSpec((tf, d), lambda i, j: (j, 0)),
        pl.BlockSpec((1, d), row),
    ]
    out_specs = [pl.BlockSpec((tm, d), lambda i, j: (i, 0))]
    out_shape = [jax.ShapeDtypeStruct((m, d), F32)]
    args = [h, g, shift, scale, gate, w_in, w_in, w_out, fg]
    if next_w is not None:
        nw_in, nw_out, layer, k = next_w
        ri, ci, ro = d // n_i, 2 * f // nf, f // (n_i * nf)
        assert ri % 16 == 0 and ci % LANES == 0 and ro % 16 == 0
        in_specs += [pl.BlockSpec((None, None, ri, ci), lambda i, j: (layer, k, i, j)),
                     pl.BlockSpec((None, None, ro, d), lambda i, j: (layer, k, i * nf + j, 0))]
        out_specs += [pl.BlockSpec((ri, ci), lambda i, j: (i, j)),
                      pl.BlockSpec((ro, d), lambda i, j: (i * nf + j, 0))]
        out_shape += [jax.ShapeDtypeStruct((d, 2 * f), BF16), jax.ShapeDtypeStruct((f, d), BF16)]
        args += [nw_in, nw_out]
    outs = pl.pallas_call(
        functools.partial(_ffn_kernel, final_norm=final_g is not None, cast_next=next_w is not None),
        grid=(n_i, nf),
        in_specs=in_specs,
        out_specs=out_specs,
        out_shape=out_shape,
        scratch_shapes=[pltpu.VMEM((tm, d), BF16), pltpu.VMEM((tm, d), F32), pltpu.VMEM((tm, LANES), F32)],
        compiler_params=_params("parallel", "arbitrary"),
        name="ffn_half_step",
    )(*args)
    return outs if next_w is not None else outs[0]


def _nmm_kernel(h_ref, g_ref, sh_ref, sc_ref, w_ref, o_ref, xn_sc, rs_sc, *, precise):
    @pl.when(pl.program_id(1) == 0)
    def _():
        _norm_modulate_into(xn_sc, h_ref, g_ref, sh_ref, sc_ref, rs_sc)

    if precise:
        xf, wf = xn_sc[...], w_ref[...]
        xh, wh = xf.astype(BF16), wf.astype(BF16)
        xl, wl = (xf - xh.astype(F32)).astype(BF16), (wf - wh.astype(F32)).astype(BF16)
        y = _dot(xh, wh) + (_dot(xh, wl) + _dot(xl, wh))
    else:
        y = _dot(xn_sc[...], w_ref[...].astype(BF16))
    o_ref[...] = y.astype(o_ref.dtype)


def norm_mod_matmul(h, g, shift, scale, w, layer, out_dtype, n=None, precise=False):
    m, d = h.shape
    n = w.shape[2] if n is None else n
    tm = min(m, 1024)
    tn = n if n <= 1024 else (1024 if n % 1024 == 0 else 512)
    row = lambda i, j: (0, 0)
    return pl.pallas_call(
        functools.partial(_nmm_kernel, precise=precise),
        grid=(m // tm, n // tn),
        in_specs=[
            pl.BlockSpec((tm, d), lambda i, j: (i, 0)),
            pl.BlockSpec((1, d), row), pl.BlockSpec((1, d), row), pl.BlockSpec((1, d), row),
            pl.BlockSpec((None, d, tn), lambda i, j: (layer, 0, j)),
        ],
        out_specs=pl.BlockSpec((tm, tn), lambda i, j: (i, j)),
        out_shape=jax.ShapeDtypeStruct((m, n), out_dtype),
        scratch_shapes=[pltpu.VMEM((tm, d), F32 if precise else BF16), pltpu.VMEM((tm, LANES), F32)],
        compiler_params=_params("parallel", "arbitrary"),
        name="norm_mod_matmul",
    )(h, g, shift, scale, w)


def _mgr_kernel(a_ref, w_ref, h_ref, gate_ref, o_ref):
    o_ref[...] = h_ref[...] + gate_ref[...] * _dot(a_ref[...], w_ref[...])


def matmul_gated_residual(a, w, layer, h, gate):
    m, k = a.shape
    n = w.shape[2]
    tm = min(m, 512)
    tn = n if k * n * 2 <= 8 * 1024 * 1024 else min(n, 1024)
    return pl.pallas_call(
        _mgr_kernel,
        grid=(m // tm, n // tn),
        in_specs=[
            pl.BlockSpec((tm, k), lambda i, j: (i, 0)),
            pl.BlockSpec((None, k, tn), lambda i, j: (layer, 0, j)),
            pl.BlockSpec((tm, tn), lambda i, j: (i, j)),
            pl.BlockSpec((1, tn), lambda i, j: (0, j)),
        ],
        out_specs=pl.BlockSpec((tm, tn), lambda i, j: (i, j)),
        out_shape=jax.ShapeDtypeStruct((m, n), F32),
        compiler_params=_params("parallel", "parallel"),
        name="matmul_gated_residual",
    )(a, w, h, gate)


def _ctx_attn_kernel(q_ref, k_ref, v_ref, o_ref):
    q = q_ref[...]
    s = _dot_nt(q, k_ref[...]) * (q.shape[-1] ** -0.5)
    m = jnp.max(s, axis=-1, keepdims=True)
    p = jnp.exp(s - m)
    l = jnp.sum(p, axis=-1, keepdims=True)
    o_ref[...] = (_dot(p.astype(BF16), v_ref[...]) / l).astype(o_ref.dtype)


def ctx_attention(qkv, n_heads, q_blk, k_blk, v_blk):
    n_ctx = qkv.shape[0]
    spec = lambda f: pl.BlockSpec((n_ctx, LANES), lambda h: (0, f(h)))
    return pl.pallas_call(
        _ctx_attn_kernel,
        grid=(n_heads,),
        in_specs=[spec(q_blk), spec(k_blk), spec(v_blk)],
        out_specs=pl.BlockSpec((n_ctx, LANES), lambda h: (0, h)),
        out_shape=jax.ShapeDtypeStruct((n_ctx, n_heads * LANES), BF16),
        compiler_params=_params("parallel"),
        name="ctx_attention",
    )(qkv, qkv, qkv)


def _na_kernel(q_ref, k_ref, v_ref, kc_ref, vc_ref, pair_ref, *rest, n_rows):
    mask_refs, o_ref = rest[:-1], rest[-1]
    nq = NA_Q_ROWS * GRID_W
    n_band = NA_BAND_ROWS * GRID_W
    scale = q_ref.shape[-1] ** -0.5 * LOG2_E
    for blk, mask_ref in enumerate(mask_refs):
        r0 = (pl.program_id(1) * len(mask_refs) + blk) * NA_Q_ROWS
        b0 = jnp.clip(r0 - NA_WIN_ROWS // 2, 0, n_rows - NA_BAND_ROWS)
        start = pl.multiple_of(b0 * GRID_W, GRID_W)
        q = q_ref[blk * nq:(blk + 1) * nq, :]
        bias = jnp.concatenate([
            jnp.concatenate([
                pair_ref[0, jnp.clip(b0 + 2 * t - r0 - qi + NA_WIN_ROWS, 0, 2 * NA_WIN_ROWS - 1)]
                for t in range(NA_BAND_ROWS // 2)], axis=1)
            for qi in range(NA_Q_ROWS)], axis=0)
        keys = jnp.concatenate([k_ref[pl.ds(start, n_band), :], kc_ref[...]], axis=0)
        vals = jnp.concatenate([v_ref[pl.ds(start, n_band), :], vc_ref[...]], axis=0)
        s = _dot_nt(q, keys) * scale
        s_w = s[:, :n_band] + (bias + mask_ref[0])
        s_c = s[:, n_band:]
        m = jnp.maximum(jnp.max(s_w, axis=-1, keepdims=True), jnp.max(s_c, axis=-1, keepdims=True))
        p = jnp.concatenate([jnp.exp2(s_w - m), jnp.exp2(s_c - m)], axis=1)
        l = jnp.sum(p, axis=-1, keepdims=True)
        o_ref[blk * nq:(blk + 1) * nq, :] = (_dot(p.astype(BF16), vals) / l).astype(o_ref.dtype)


def _na_window_masks(n_rows):
    nq, nk = NA_Q_ROWS * GRID_W, NA_BAND_ROWS * GRID_W
    valid = []
    for r0 in (0, NA_Q_ROWS, n_rows - NA_Q_ROWS):
        b0 = int(np.clip(r0 - NA_WIN_ROWS // 2, 0, n_rows - NA_BAND_ROWS))
        qr = r0 + np.arange(nq) // GRID_W
        qc = np.arange(nq) % GRID_W
        kr = b0 + np.arange(nk) // GRID_W
        kc = np.arange(nk) % GRID_W
        rs = np.clip(qr - NA_WIN_ROWS // 2, 0, n_rows - NA_WIN_ROWS)
        cs = np.clip(qc - NA_WIN_COLS // 2, 0, GRID_W - NA_WIN_COLS)
        ok = ((kr[None] >= rs[:, None]) & (kr[None] < rs[:, None] + NA_WIN_ROWS)
              & (kc[None] >= cs[:, None]) & (kc[None] < cs[:, None] + NA_WIN_COLS))
        valid.append(ok)
    return np.where(np.stack(valid), 0.0, MASK_VALUE).astype(np.float32)


def _na_pair_tiles(rpb):
    n_heads = rpb.shape[0]
    dcol = np.arange(GRID_W)[None, :] - np.arange(GRID_W)[:, None] + NA_WIN_COLS - 1
    in_table = (dcol >= 0) & (dcol < 2 * NA_WIN_COLS - 1)
    onehot = (np.clip(dcol, 0, 2 * NA_WIN_COLS - 2)[None] == np.arange(2 * NA_WIN_COLS - 1)[:, None, None]) & in_table
    toeplitz = jnp.einsum("hrd,dqk->hrqk", rpb, jnp.asarray(onehot, F32), precision=lax.Precision.HIGHEST)
    zero = jnp.zeros((n_heads, 1, GRID_W, GRID_W), F32)
    padded = jnp.concatenate([zero, toeplitz, zero], axis=1)
    return jnp.concatenate([padded[:, :-1], padded[:, 1:]], axis=-1)


def neighbourhood_attention(qkv, qkv_c, rpb):
    n_tok = qkv.shape[0]
    n_ctx = qkv_c.shape[0]
    n_rows = n_tok // GRID_W
    n_blocks = n_rows // NA_Q_ROWS
    nb = NA_BLOCKS_PER_STEP
    nq, nk = NA_Q_ROWS * GRID_W, NA_BAND_ROWS * GRID_W
    assert NA_BAND_ROWS % 2 == 0
    pairs = _na_pair_tiles(rpb) * LOG2_E
    masks = jnp.asarray(_na_window_masks(n_rows))
    h_ = NA_HEADS

    def mask_spec(blk):
        def index(h, s):
            rb = s * nb + blk
            return (jnp.where(rb == 0, 0, jnp.where(rb == n_blocks - 1, 2, 1)), 0, 0)
        return pl.BlockSpec((1, nq, nk), index)

    return pl.pallas_call(
        functools.partial(_na_kernel, n_rows=n_rows),
        grid=(h_, n_blocks // nb),
        in_specs=[
            pl.BlockSpec((nb * nq, LANES), lambda h, s: (s, h)),
            pl.BlockSpec((n_tok, LANES), lambda h, s: (0, h_ + h)),
            pl.BlockSpec((n_tok, LANES), lambda h, s: (0, 2 * h_ + h)),
            pl.BlockSpec((n_ctx, LANES), lambda h, s: (0, h_ + h)),
            pl.BlockSpec((n_ctx, LANES), lambda h, s: (0, 2 * h_ + h)),
            pl.BlockSpec((1,) + pairs.shape[1:], lambda h, s: (h, 0, 0, 0)),
        ] + [mask_spec(blk) for blk in range(nb)],
        out_specs=pl.BlockSpec((nb * nq, LANES), lambda h, s: (s, h)),
        out_shape=jax.ShapeDtypeStruct((n_tok, h_ * LANES), BF16),
        compiler_params=_params("parallel", "arbitrary"),
        name="neighbourhood_attention",
    )(qkv, qkv, qkv, qkv_c, qkv_c, pairs, *([masks] * nb))


def _gqa_proj_kernel(h_ref, g_ref, sh_ref, sc_ref, w_ref, qn_ref, kn_ref, cos_ref, sin_ref, o_ref, xn_sc,
                     rs_sc, *, rope, q_scale):
    j = pl.program_id(1)
    tn = o_ref.shape[1]
    n_q = GQA_HEADS * LANES // tn

    @pl.when(j == 0)
    def _():
        _norm_modulate_into(xn_sc, h_ref, g_ref, sh_ref, sc_ref, rs_sc)

    y = _dot(xn_sc[...], w_ref[...].astype(BF16))

    def normed(w, mult):
        for s in range(tn // LANES):
            cols = slice(s * LANES, (s + 1) * LANES)
            x = y[:, cols]
            var = jnp.mean(x * x, axis=-1, keepdims=True)
            t = (x * lax.rsqrt(var + EPS)) * w
            if rope:
                t = t * cos_ref[...] + pltpu.roll(t, LANES // 2, 1) * sin_ref[...]
            if mult != 1.0:
                t = t * mult
            o_ref[:, cols] = t.astype(o_ref.dtype)

    @pl.when(j < n_q)
    def _():
        normed(qn_ref[...], q_scale)

    @pl.when(j == n_q)
    def _():
        normed(kn_ref[...], 1.0)

    @pl.when(j > n_q)
    def _():
        o_ref[...] = y.astype(o_ref.dtype)


def gqa_projection(h, g, shift, scale, w, layer, q_norm, k_norm, cos_full, sin_signed, rope, q_scale):
    m, d = h.shape
    n = w.shape[2]
    tm = min(m, 1024)
    tn = GQA_KV_HEADS * LANES
    row = lambda i, j: (0, 0)
    head = pl.BlockSpec((1, LANES), row)
    pos = pl.BlockSpec((tm, LANES), lambda i, j: (i, 0))
    return pl.pallas_call(
        functools.partial(_gqa_proj_kernel, rope=rope, q_scale=q_scale),
        grid=(m // tm, n // tn),
        in_specs=[
            pl.BlockSpec((tm, d), lambda i, j: (i, 0)),
            pl.BlockSpec((1, d), row), pl.BlockSpec((1, d), row), pl.BlockSpec((1, d), row),
            pl.BlockSpec((None, d, tn), lambda i, j: (layer, 0, j)),
            head, head, pos, pos,
        ],
        out_specs=pl.BlockSpec((tm, tn), lambda i, j: (i, j)),
        out_shape=jax.ShapeDtypeStruct((m, n), BF16),
        scratch_shapes=[pltpu.VMEM((tm, d), BF16), pltpu.VMEM((tm, LANES), F32)],
        compiler_params=_params("parallel", "arbitrary"),
        name="gqa_projection",
    )(h, g, shift, scale, w, q_norm[None], k_norm[None], cos_full, sin_signed)


def _gqa_kernel(q_ref, k_ref, v_ref, kc_ref, vc_ref, o_ref, m_sc, l_sc, acc_sc, *, tk):
    tq = q_ref.shape[0]
    n_tok = k_ref.shape[0]
    q = jnp.concatenate([q_ref[:, g * LANES:(g + 1) * LANES] for g in range(GQA_GROUP)], axis=0)
    m_sc[...] = jnp.full_like(m_sc, -jnp.inf)
    l_sc[...] = jnp.zeros_like(l_sc)
    acc_sc[...] = jnp.zeros_like(acc_sc)

    def update(kb, vb):
        s = _dot_nt(q, kb)
        blocks = [s[:, c * LANES:(c + 1) * LANES] for c in range(kb.shape[0] // LANES)]
        m_old = m_sc[...]
        block_max = jnp.max(functools.reduce(jnp.maximum, blocks), axis=-1, keepdims=True)
        m_new = jnp.maximum(m_old, block_max)
        alpha = jnp.exp2(m_old - m_new)
        p = [jnp.exp2(b - m_new) for b in blocks]
        l_sc[...] = alpha * l_sc[...] + functools.reduce(jnp.add, p)
        acc_sc[...] = alpha * acc_sc[...] + _dot(jnp.concatenate(p, axis=-1).astype(BF16), vb)
        m_sc[...] = m_new

    def body(j, carry):
        r = pl.multiple_of(j * tk, tk)
        update(k_ref[pl.ds(r, tk), :], v_ref[pl.ds(r, tk), :])
        return carry

    lax.fori_loop(0, n_tok // tk, body, 0, unroll=4)
    update(kc_ref[...], vc_ref[...])
    o = acc_sc[...] / jnp.sum(l_sc[...], axis=-1, keepdims=True)
    for g in range(GQA_GROUP):
        o_ref[:, g * LANES:(g + 1) * LANES] = o[g * tq:(g + 1) * tq].astype(o_ref.dtype)


def gqa_attention(qkv, qkv_c):
    n_tok = qkv.shape[0]
    n_ctx = qkv_c.shape[0]
    tq = min(n_tok, 512)
    tk = min(n_tok, 1024)
    kv0 = GQA_HEADS
    v0 = GQA_HEADS + GQA_KV_HEADS
    rows = GQA_GROUP * tq
    return pl.pallas_call(
        functools.partial(_gqa_kernel, tk=tk),
        grid=(GQA_KV_HEADS, n_tok // tq),
        in_specs=[
            pl.BlockSpec((tq, GQA_GROUP * LANES), lambda kv, i: (i, kv)),
            pl.BlockSpec((n_tok, LANES), lambda kv, i: (0, kv0 + kv)),
            pl.BlockSpec((n_tok, LANES), lambda kv, i: (0, v0 + kv)),
            pl.BlockSpec((n_ctx, LANES), lambda kv, i: (0, kv0 + kv)),
            pl.BlockSpec((n_ctx, LANES), lambda kv, i: (0, v0 + kv)),
        ],
        out_specs=pl.BlockSpec((tq, GQA_GROUP * LANES), lambda kv, i: (i, kv)),
        out_shape=jax.ShapeDtypeStruct((n_tok, GQA_HEADS * LANES), BF16),
        scratch_shapes=[pltpu.VMEM((rows, LANES), F32)] * 3,
        compiler_params=_params("parallel", "arbitrary"),
        name="gqa_attention",
    )(qkv, qkv, qkv, qkv_c, qkv_c)


def axial_rope_tables(n_tok):
    t = jnp.arange(n_tok, dtype=jnp.int32)
    row = (t // GRID_W).astype(F32)
    col = (t % GRID_W).astype(F32)
    n_freq = LANES // 4
    inv_freq = ROPE_THETA ** (-jnp.arange(n_freq, dtype=F32) / n_freq)
    ang = jnp.concatenate([row[:, None] * inv_freq, col[:, None] * inv_freq], axis=-1)
    cos, sin = jnp.cos(ang), jnp.sin(ang)
    return jnp.concatenate([cos, cos], axis=-1), jnp.concatenate([-sin, sin], axis=-1)


def _conv_silu_kernel(prev_ref, cur_ref, next_ref, w_ref, b_ref, o_ref):
    i = pl.program_id(0)
    tl = cur_ref.shape[0]
    prev = jnp.where(i == 0, 0.0, prev_ref[...].astype(F32))
    nxt = jnp.where(i == pl.num_programs(0) - 1, 0.0, next_ref[...].astype(F32))
    ext = jnp.concatenate([prev, cur_ref[...].astype(F32), nxt], axis=0)
    n_ext = ext.shape[0]
    pad = (SSM_CONV_W - 1) // 2
    acc = jnp.zeros((tl, ext.shape[1]), F32) + b_ref[...]
    for k in range(SSM_CONV_W):
        shifted = ext if k == pad else pltpu.roll(ext, (pad - k) % n_ext, 0)
        acc = acc + shifted[CONV_HALO:CONV_HALO + tl] * w_ref[k:k + 1, :]
    o_ref[...] = _silu(acc).astype(o_ref.dtype)


def conv_silu(zxbc, conv_w, conv_b):
    n = zxbc.shape[0]
    tl = min(n, 512)
    tc = 512
    c0 = SSM_D_INNER // tc
    hb = tl // CONV_HALO
    n_halo = n // CONV_HALO
    return pl.pallas_call(
        _conv_silu_kernel,
        grid=(n // tl, SSM_CONV_DIM // tc),
        in_specs=[
            pl.BlockSpec((CONV_HALO, tc), lambda i, j: (jnp.maximum(i * hb - 1, 0), c0 + j)),
            pl.BlockSpec((tl, tc), lambda i, j: (i, c0 + j)),
            pl.BlockSpec((CONV_HALO, tc), lambda i, j: (jnp.minimum((i + 1) * hb, n_halo - 1), c0 + j)),
            pl.BlockSpec((SSM_CONV_W, tc), lambda i, j: (0, j)),
            pl.BlockSpec((1, tc), lambda i, j: (0, j)),
        ],
        out_specs=pl.BlockSpec((tl, tc), lambda i, j: (i, j)),
        out_shape=jax.ShapeDtypeStruct((n, SSM_CONV_DIM), BF16),
        compiler_params=_params("parallel", "parallel"),
        name="conv_silu",
    )(zxbc, zxbc, zxbc, conv_w, conv_b)


def _ssd_prep_kernel(dt_ref, bias_ref, alog_ref, delta_ref, acs_ref):
    x = dt_ref[...] + bias_ref[...]
    delta = jnp.maximum(x, 0.0) + jnp.log1p(jnp.exp(-jnp.abs(x)))
    da = delta * (-jnp.exp(alog_ref[...]))
    q = x.shape[0]
    i = lax.broadcasted_iota(jnp.int32, (q, q), 0)
    j = lax.broadcasted_iota(jnp.int32, (q, q), 1)
    hi = lax.Precision.HIGHEST
    fwd = jnp.dot((i >= j).astype(F32), da, preferred_element_type=F32, precision=hi)
    bwd = jnp.dot((i <= j).astype(F32), da, preferred_element_type=F32, precision=hi)
    lane = lax.broadcasted_iota(jnp.int32, da.shape, 1)
    delta_ref[...] = delta
    acs_ref[...] = jnp.where(lane < SSM_HEADS, fwd, bwd)


def ssd_prep(dt, dt_bias, a_log):
    n = dt.shape[0]
    q = SSM_CHUNK
    blk = pl.BlockSpec((q, 2 * SSM_HEADS), lambda c: (c, 0))
    row = pl.BlockSpec((1, 2 * SSM_HEADS), lambda c: (0, 0))
    return pl.pallas_call(
        _ssd_prep_kernel,
        grid=(n // q,),
        in_specs=[blk, row, row],
        out_specs=[blk, blk],
        out_shape=[jax.ShapeDtypeStruct(dt.shape, F32)] * 2,
        compiler_params=_params("parallel"),
        name="ssd_prep",
    )(dt, dt_bias.reshape(1, -1), a_log.reshape(1, -1))


def _ssd_kernel(x_ref, b_ref, c_ref, ac_ref, ar_ref, dr_ref, init_ref, y_ref, fin_ref, state_sc, *, n_sub):
    d = pl.program_id(0)
    q = SSM_CHUNK

    @pl.when(pl.program_id(2) == 0)
    def _():
        state_sc[...] = init_ref[0]

    i = lax.broadcasted_iota(jnp.int32, (q, q), 0)
    j = lax.broadcasted_iota(jnp.int32, (q, q), 1)
    causal = jnp.where(d == 0, i - j, j - i) >= 0
    lo = lax.broadcasted_iota(jnp.int32, (q, LANES), 1) < SSM_HEAD_DIM

    def chunk(k, carry):
        ci = k + d * (n_sub - 1 - 2 * k)
        r = pl.multiple_of(ci * q, q)
        bm = b_ref[pl.ds(r, q), :]
        cm = c_ref[pl.ds(r, q), :]
        ac = ac_ref[0, 0, pl.ds(r, q), :]
        ar = ar_ref[0, 0, ci]
        dr = dr_ref[0, 0, ci]
        tot = jnp.where(d == 0, ar[:, q - 1:q], ar[:, 0:1])
        w_out = jnp.exp(tot - ar) * dr
        e_tot = jnp.exp(tot)
        cb = _dot_nt(cm, bm)
        bt = bm.astype(F32).T
        cf = cm.astype(F32)
        state = state_sc[...]
        for p in range(SSM_HEADS_PER_GROUP // 2):
            cols = slice(p * LANES, (p + 1) * LANES)
            x = x_ref[pl.ds(r, q), cols]
            lhs_y, lhs_s = [], []
            for hd in (2 * p, 2 * p + 1):
                a_i = jnp.broadcast_to(ac[:, hd:hd + 1], (q, q))
                seg = jnp.where(causal, a_i - ar[hd:hd + 1, :], -jnp.inf)
                intra = cb * jnp.exp(seg) * dr[hd:hd + 1, :]
                lhs_y.append(jnp.concatenate([intra, cf * jnp.exp(a_i)], axis=1).astype(BF16))
                lhs_s.append((bt * w_out[hd:hd + 1, :]).astype(BF16))
            rhs = jnp.concatenate([x, state[:, cols].astype(BF16)], axis=0)
            res = _dot(jnp.concatenate(lhs_y, axis=0), rhs)
            y_ref[0, pl.ds(r, q), cols] = jnp.where(lo, res[:q], res[q:]).astype(y_ref.dtype)
            s_res = _dot(jnp.concatenate(lhs_s, axis=0), x)
            s_new = jnp.where(lo, s_res[:SSM_STATE], s_res[SSM_STATE:])
            keep = jnp.where(lo[0:1], e_tot[2 * p:2 * p + 1, :], e_tot[2 * p + 1:2 * p + 2, :])
            state_sc[:, cols] = state[:, cols] * keep + s_new
        return carry

    lax.fori_loop(0, n_sub, chunk, 0, unroll=True)

    @pl.when(pl.program_id(2) == pl.num_programs(2) - 1)
    def _():
        fin_ref[0] = state_sc[...]


def ssd_scan(xbc, delta, acs, init_state):
    assert SSM_STATE == SSM_CHUNK == LANES
    n = xbc.shape[0]
    q = SSM_CHUNK
    nc = n // q
    n_sub = min(nc, SSD_CHUNKS_PER_STEP)
    ns = nc // n_sub
    rows = n_sub * q
    hg = SSM_HEADS_PER_GROUP
    gw = hg * SSM_HEAD_DIM
    ac_col = acs.reshape(n, 2, SSM_GROUPS, hg).transpose(1, 2, 0, 3)
    row = lambda t: t.reshape(nc, q, 2, SSM_GROUPS, hg).transpose(2, 3, 0, 4, 1)
    ac_row, dl_row = row(acs), row(delta)
    b0 = SSM_D_INNER // SSM_STATE
    c0 = b0 + SSM_GROUPS
    step = lambda d, s: s + d * (ns - 1 - 2 * s)
    row_spec = pl.BlockSpec((1, 1, n_sub, hg, q), lambda d, g, s: (d, g, step(d, s), 0, 0))
    state_spec = pl.BlockSpec((1, SSM_STATE, gw), lambda d, g, s: (d, 0, g))
    return pl.pallas_call(
        functools.partial(_ssd_kernel, n_sub=n_sub),
        grid=(2, SSM_GROUPS, ns),
        in_specs=[
            pl.BlockSpec((rows, gw), lambda d, g, s: (step(d, s), g)),
            pl.BlockSpec((rows, SSM_STATE), lambda d, g, s: (step(d, s), b0 + g)),
            pl.BlockSpec((rows, SSM_STATE), lambda d, g, s: (step(d, s), c0 + g)),
            pl.BlockSpec((1, 1, rows, hg), lambda d, g, s: (d, g, step(d, s), 0)),
            row_spec, row_spec,
            state_spec,
        ],
        out_specs=[pl.BlockSpec((1, rows, gw), lambda d, g, s: (d, step(d, s), g)), state_spec],
        out_shape=[jax.ShapeDtypeStruct((2, n, SSM_D_INNER), BF16),
                   jax.ShapeDtypeStruct((2, SSM_STATE, SSM_D_INNER), F32)],
        scratch_shapes=[pltpu.VMEM((SSM_STATE, gw), F32)],
        compiler_params=_params("parallel", "parallel", "arbitrary"),
        name="ssd_scan",
    )(xbc, xbc, xbc, ac_col, ac_row, dl_row, init_state)


def _ssm_out_kernel(yf_ref, yb_ref, x_ref, z_ref, dsum_ref, g_ref, w_ref, h_ref, gate_ref, o_ref):
    y = yf_ref[0].astype(F32) + yb_ref[0].astype(F32) + dsum_ref[...] * x_ref[...].astype(F32)
    v = y * _silu(z_ref[...].astype(F32))
    var = jnp.mean(v * v, axis=-1, keepdims=True)
    yn = ((v * lax.rsqrt(var + EPS)) * g_ref[...]).astype(BF16)
    o_ref[...] = h_ref[...] + gate_ref[...] * _dot(yn, w_ref[...])


def ssm_out_projection(y, xbc, zxbc, dsum, norm_g, w, layer, h, gate):
    n, d = h.shape
    tl = min(n, 256)
    di = SSM_D_INNER
    row = pl.BlockSpec((1, di), lambda i: (0, 0))
    return pl.pallas_call(
        _ssm_out_kernel,
        grid=(n // tl,),
        in_specs=[
            pl.BlockSpec((1, tl, di), lambda i: (0, i, 0)),
            pl.BlockSpec((1, tl, di), lambda i: (1, i, 0)),
            pl.BlockSpec((tl, di), lambda i: (i, 0)),
            pl.BlockSpec((tl, di), lambda i: (i, 0)),
            row, row,
            pl.BlockSpec((None, di, d), lambda i: (layer, 0, 0), pipeline_mode=pl.Buffered(1)),
            pl.BlockSpec((tl, d), lambda i: (i, 0)),
            pl.BlockSpec((1, d), lambda i: (0, 0)),
        ],
        out_specs=pl.BlockSpec((tl, d), lambda i: (i, 0)),
        out_shape=jax.ShapeDtypeStruct((n, d), F32),
        compiler_params=_params("parallel"),
        name="ssm_out_projection",
    )(y, y, xbc, zxbc, dsum, norm_g, w, h, gate)


def mamba2_bidirectional(h, hc, g, mod_l, mod_c, w_in, w_out_bf16, layer, conv_w, conv_b, a_log, dt_bias,
                         d_skip, norm_g):
    n_main = SSM_D_INNER + SSM_CONV_DIM
    w_dt = w_in[layer, :, n_main:][None]
    dsum = jnp.repeat(d_skip[0] + d_skip[1], SSM_HEAD_DIM)[None]
    state = jnp.zeros((2, SSM_STATE, SSM_D_INNER), F32)
    outs = []
    for t, mod in ((hc, mod_c), (h, mod_l)):
        shift, scale, gate = mod[3:4], mod[4:5], mod[5:6]
        zxbc = norm_mod_matmul(t, g, shift, scale, w_in, layer, BF16, n=n_main)
        dt = norm_mod_matmul(t, g, shift, scale, w_dt, 0, F32, precise=True)
        xbc = conv_silu(zxbc, conv_w, conv_b[None])
        delta, acs = ssd_prep(dt, dt_bias, a_log)
        y, state = ssd_scan(xbc, delta, acs, state)
        outs.append(ssm_out_projection(y, xbc, zxbc, dsum, norm_g[None], w_out_bf16, layer, t, gate))
    return outs[1], outs[0]


def kernel(x, c, ctx, c_ctx, ada_w, ada_b, norm_g, ffn_w_in, ffn_w_out, na_w_qkv, na_rpb, na_w_o, ssm_w_in, ssm_conv_w, ssm_conv_b, ssm_a_log, ssm_dt_bias, ssm_d, ssm_norm_g, ssm_w_out, gqa_w_qkv, gqa_q_norm, gqa_k_norm, gqa_w_o, final_norm_g):
    bsz, n_tok, d = x.shape
    assert bsz == 1, "the kernels take one sequence"
    depth = ada_w.shape[0]
    h, hc = x[0], ctx[0]
    mod = adaln_mod(c, c_ctx, ada_w, ada_b)
    cos_full, sin_signed = axial_rope_tables(n_tok)
    rope_off = jnp.zeros((ctx.shape[1], LANES), F32)
    ffn_w = {(0, 0): (ffn_w_in[0, 0].astype(BF16), ffn_w_out[0, 0].astype(BF16))}
    na_w_o_b, ssm_w_out_b, gqa_w_o_b = na_w_o.astype(BF16), ssm_w_out.astype(BF16), gqa_w_o.astype(BF16)
    gqa_q_scale = LANES ** -0.5 * LOG2_E

    def ffn(t, m, i, k, latent, final_g=None):
        w_in_b, w_out_b = ffn_w[(i, k)]
        nxt = (i + (k + 1) // 2, (k + 1) % 2)
        cast_next = latent and nxt[0] < depth
        out = ffn_half_step(t, norm_g[i, 2 * k][None], m[6 * k:6 * k + 1], m[6 * k + 1:6 * k + 2],
                            m[6 * k + 2:6 * k + 3], w_in_b, w_out_b,
                            (ffn_w_in, ffn_w_out) + nxt if cast_next else None, final_g)
        if cast_next:
            out, *ffn_w[nxt] = out
        return out

    for i in range(depth):
        last = i == depth - 1
        ml, mc = mod[i, 0], mod[i, 1]
        g_mix = norm_g[i, 1][None]
        h = ffn(h, ml, i, 0, True)
        hc = ffn(hc, mc, i, 0, False)
        kind, j = i % 3, i // 3
        if kind == 0:
            qkv = norm_mod_matmul(h, g_mix, ml[3:4], ml[4:5], na_w_qkv, j, BF16)
            qkv_c = norm_mod_matmul(hc, g_mix, mc[3:4], mc[4:5], na_w_qkv, j, BF16)
            o = neighbourhood_attention(qkv, qkv_c, na_rpb[j])
            h = matmul_gated_residual(o, na_w_o_b, j, h, ml[5:6])
            if not last:
                oc = ctx_attention(qkv_c, NA_HEADS, lambda hd: hd, lambda hd: NA_HEADS + hd,
                                   lambda hd: 2 * NA_HEADS + hd)
                hc = matmul_gated_residual(oc, na_w_o_b, j, hc, mc[5:6])
        elif kind == 1:
            h, hc = mamba2_bidirectional(h, hc, g_mix, ml, mc, ssm_w_in, ssm_w_out_b, j, ssm_conv_w[j],
                                         ssm_conv_b[j], ssm_a_log[j], ssm_dt_bias[j], ssm_d[j], ssm_norm_g[j])
        else:
            qkv = gqa_projection(h, g_mix, ml[3:4], ml[4:5], gqa_w_qkv, j, gqa_q_norm[j], gqa_k_norm[j],
                                 cos_full, sin_signed, True, gqa_q_scale)
            qkv_c = gqa_projection(hc, g_mix, mc[3:4], mc[4:5], gqa_w_qkv, j, gqa_q_norm[j], gqa_k_norm[j],
                                   rope_off, rope_off, False, 1.0)
            o = gqa_attention(qkv, qkv_c)
            h = matmul_gated_residual(o, gqa_w_o_b, j, h, ml[5:6])
            if not last:
                oc = ctx_attention(qkv_c, GQA_HEADS, lambda hd: hd, lambda hd: GQA_HEADS + hd // GQA_GROUP,
                                   lambda hd: GQA_HEADS + GQA_KV_HEADS + hd // GQA_GROUP)
                hc = matmul_gated_residual(oc, gqa_w_o_b, j, hc, mc[5:6])
        h = ffn(h, ml, i, 1, True, final_norm_g[None] if last else None)
        if not last:
            hc = ffn(hc, mc, i, 1, False)
    return h[None]
```

```python
import functools

import jax
import jax.numpy as jnp
import numpy as np
from jax import lax
from jax.experimental import pallas as pl
from jax.experimental.pallas import tpu as pltpu

F32 = jnp.float32
BF16 = jnp.bfloat16

EPS = 1e-6
HALF_STEP = 0.5
N_MOD = 9
GRID_W = 64
LANES = 128
NORM_ROWS = 16

NA_HEADS = 16
NA_WIN_ROWS = 8
NA_WIN_COLS = 16
NA_Q_ROWS = 4
NA_BAND_ROWS = 12
NA_BLOCKS_PER_STEP = 4

SSM_HEADS = 64
SSM_HEAD_DIM = 64
SSM_GROUPS = 8
SSM_STATE = 128
SSM_CHUNK = 128
SSM_D_INNER = SSM_HEADS * SSM_HEAD_DIM
SSM_GN = SSM_GROUPS * SSM_STATE
SSM_CONV_W = 5
SSM_CONV_DIM = SSM_D_INNER + 2 * SSM_GN
SSM_HEADS_PER_GROUP = SSM_HEADS // SSM_GROUPS
SSD_CHUNKS_PER_STEP = 16
CONV_HALO = 16

GQA_HEADS = 16
GQA_KV_HEADS = 4
GQA_GROUP = GQA_HEADS // GQA_KV_HEADS
ROPE_THETA = 10000.0
LOG2_E = 1.4426950408889634

MASK_VALUE = -1e30
VMEM_LIMIT = 56 * 1024 * 1024


def _params(*sem):
    return pltpu.CompilerParams(dimension_semantics=sem, vmem_limit_bytes=VMEM_LIMIT)


def _silu(x):
    half = 0.5 * x
    return half + half * jnp.tanh(half)


def _dot(a, b):
    return jnp.dot(a, b, preferred_element_type=F32)


def _dot_nt(a, b):
    return lax.dot_general(a, b, (((1,), (1,)), ((), ())), preferred_element_type=F32)


def _dot_tn(a, b):
    return lax.dot_general(a, b, (((0,), (0,)), ((), ())), preferred_element_type=F32)


def _rowwise(n_rows, chunk, body, unroll=2):
    def step(i, carry):
        body(pl.ds(pl.multiple_of(i * chunk, chunk), chunk))
        return carry

    lax.fori_loop(0, n_rows // chunk, step, 0, unroll=unroll)


def _norm_modulate_into(dst_ref, src_ref, g_ref, sh_ref, sc_ref, rs_ref):
    gain = g_ref[...] * (1.0 + sc_ref[...])
    shift = sh_ref[...]

    n_rows, d = src_ref.shape

    def stats(rows):
        x = src_ref[rows, :]
        sq = x * x
        rs_ref[rows, :] = functools.reduce(jnp.add, [sq[:, c * LANES:(c + 1) * LANES] for c in range(d // LANES)])

    def scale(rows):
        x = src_ref[rows, :]
        rs = jnp.concatenate([rs_ref[rows, :]] * (d // LANES), axis=1)
        dst_ref[rows, :] = ((x * rs) * gain + shift).astype(dst_ref.dtype)

    _rowwise(n_rows, NORM_ROWS, stats, unroll=4)
    var = jnp.sum(rs_ref[...], axis=-1, keepdims=True) * (1.0 / d)
    rs_ref[...] = jnp.broadcast_to(lax.rsqrt(var + EPS), rs_ref.shape)
    _rowwise(n_rows, NORM_ROWS, scale, unroll=2)


def _adaln_kernel(c_ref, w_ref, b_ref, o_ref, s_sc, *, rows, group):
    tk, tn = w_ref.shape
    s_sc[...] = _silu(c_ref[...])

    @pl.when(pl.program_id(2) == 0)
    def _():
        o_ref[...] = jnp.broadcast_to(b_ref[...], o_ref.shape)

    for lg in range(tn // group):
        cols = slice(lg * group, (lg + 1) * group)

        def body(i, carry):
            a0, a1 = carry
            r = pl.multiple_of(i * rows, rows)
            w = w_ref[pl.ds(r, rows), cols]
            s0 = jnp.concatenate([s_sc[0, pl.ds(r, rows), :]] * (group // LANES), axis=1)
            s1 = jnp.concatenate([s_sc[1, pl.ds(r, rows), :]] * (group // LANES), axis=1)
            return a0 + w * s0, a1 + w * s1

        zero = jnp.zeros((rows, group), F32)
        a0, a1 = lax.fori_loop(0, tk // rows, body, (zero, zero), unroll=2)
        o_ref[0:1, cols] += jnp.sum(a0, axis=0, keepdims=True)
        o_ref[1:2, cols] += jnp.sum(a1, axis=0, keepdims=True)


def adaln_mod(c, c_ctx, ada_w, ada_b):
    depth, d, n = ada_w.shape
    tk = 256
    tn = n // 4 if n % 2048 == 0 else 512
    cb = jnp.broadcast_to(jnp.stack([c[0], c_ctx])[:, :, None], (2, d, LANES))
    out = pl.pallas_call(
        functools.partial(_adaln_kernel, rows=16, group=512),
        grid=(depth, n // tn, d // tk),
        in_specs=[
            pl.BlockSpec((2, tk, LANES), lambda l, j, k: (0, k, 0)),
            pl.BlockSpec((None, tk, tn), lambda l, j, k: (l, k, j)),
            pl.BlockSpec((None, 1, tn), lambda l, j, k: (l, 0, j)),
        ],
        out_specs=pl.BlockSpec((None, 2, tn), lambda l, j, k: (l, 0, j)),
        out_shape=jax.ShapeDtypeStruct((depth, 2, n), F32),
        scratch_shapes=[pltpu.VMEM((2, tk, LANES), F32)],
        compiler_params=_params("parallel", "parallel", "arbitrary"),
        name="adaln_mod",
    )(cb, ada_w, ada_b.reshape(depth, 1, n))
    return out.reshape(depth, 2, N_MOD, d)


def _ffn_kernel(h_ref, g_ref, sh_ref, sc_ref, gate_ref, wa_ref, wb_ref, wo_ref, fg_ref, *rest, final_norm, cast_next):
    if cast_next:
        nwi_ref, nwo_ref, o_ref, nwi_out, nwo_out, xn_sc, acc_sc, rs_sc = rest
    else:
        o_ref, xn_sc, acc_sc, rs_sc = rest
    j = pl.program_id(1)

    @pl.when(j == 0)
    def _():
        _norm_modulate_into(xn_sc, h_ref, g_ref, sh_ref, sc_ref, rs_sc)
        acc_sc[...] = jnp.zeros_like(acc_sc)

    if cast_next:
        nwi_out[...] = nwi_ref[...].astype(BF16)
        nwo_out[...] = nwo_ref[...].astype(BF16)
    xn = xn_sc[...]
    a = _dot(xn, wa_ref[...])
    b = _dot(xn, wb_ref[...])
    hm = (_silu(a) * b).astype(BF16)
    acc_sc[...] += _dot(hm, wo_ref[...])

    @pl.when(j == pl.num_programs(1) - 1)
    def _():
        out = h_ref[...] + (HALF_STEP * gate_ref[...]) * acc_sc[...]
        if final_norm:
            var = jnp.mean(out * out, axis=-1, keepdims=True)
            out = (out * lax.rsqrt(var + EPS)) * fg_ref[...]
        o_ref[...] = out


def ffn_half_step(h, g, shift, scale, gate, w_in, w_out, next_w=None, final_g=None):
    m, d = h.shape
    f = w_out.shape[0]
    tm = min(m, 512)
    tf = 512
    nf = f // tf
    n_i = m // tm
    row = lambda i, j: (0, 0)
    fg = final_g if final_g is not None else g
    in_specs = [
        pl.BlockSpec((tm, d), lambda i, j: (i, 0)),
        pl.BlockSpec((1, d), row), pl.BlockSpec((1, d), row),
        pl.BlockSpec((1, d), row), pl.BlockSpec((1, d), row),
        pl.BlockSpec((d, tf), lambda i, j: (0, j)),
        pl.BlockSpec((d, tf), lambda i, j: (0, j + nf)),
        pl.BlockSpec((tf, d), lambda i, j: (j, 0)),
        pl.BlockSpec((1, d), row),
    ]
    out_specs = [pl.BlockSpec((tm, d), lambda i, j: (i, 0))]
    out_shape = [jax.ShapeDtypeStruct((m, d), F32)]
    args = [h, g, shift, scale, gate, w_in, w_in, w_out, fg]
    if next_w is not None:
        nw_in, nw_out, layer, k = next_w
        ri, ci, ro = d // n_i, 2 * f // nf, f // (n_i * nf)
        assert ri % 16 == 0 and ci % LANES == 0 and ro % 16 == 0
        in_specs += [pl.BlockSpec((None, None, ri, ci), lambda i, j: (layer, k, i, j)),
                     pl.BlockSpec((None, None, ro, d), lambda i, j: (layer, k, i * nf + j, 0))]
        out_specs += [pl.BlockSpec((ri, ci), lambda i, j: (i, j)),
                      pl.BlockSpec((ro, d), lambda i, j: (i * nf + j, 0))]
        out_shape += [jax.ShapeDtypeStruct((d, 2 * f), BF16), jax.ShapeDtypeStruct((f, d), BF16)]
        args += [nw_in, nw_out]
    outs = pl.pallas_call(
        functools.partial(_ffn_kernel, final_norm=final_g is not None, cast_next=next_w is not None),
        grid=(n_i, nf),
        in_specs=in_specs,
        out_specs=out_specs,
        out_shape=out_shape,
        scratch_shapes=[pltpu.VMEM((tm, d), BF16), pltpu.VMEM((tm, d), F32), pltpu.VMEM((tm, LANES), F32)],
        compiler_params=_params("parallel", "arbitrary"),
        name="ffn_half_step",
    )(*args)
    return outs if next_w is not None else outs[0]


def _nmm_kernel(h_ref, g_ref, sh_ref, sc_ref, w_ref, o_ref, xn_sc, rs_sc, *, precise):
    @pl.when(pl.program_id(1) == 0)
    def _():
        _norm_modulate_into(xn_sc, h_ref, g_ref, sh_ref, sc_ref, rs_sc)

    if precise:
        xf, wf = xn_sc[...], w_ref[...]
        xh, wh = xf.astype(BF16), wf.astype(BF16)
        xl, wl = (xf - xh.astype(F32)).astype(BF16), (wf - wh.astype(F32)).astype(BF16)
        y = _dot(xh, wh) + (_dot(xh, wl) + _dot(xl, wh))
    else:
        y = _dot(xn_sc[...], w_ref[...].astype(BF16))
    o_ref[...] = y.astype(o_ref.dtype)


def norm_mod_matmul(h, g, shift, scale, w, layer, out_dtype, n=None, precise=False):
    m, d = h.shape
    n = w.shape[2] if n is None else n
    tm = min(m, 1024)
    tn = n if n <= 1024 else (1024 if n % 1024 == 0 else 512)
    row = lambda i, j: (0, 0)
    return pl.pallas_call(
        functools.partial(_nmm_kernel, precise=precise),
        grid=(m // tm, n // tn),
        in_specs=[
            pl.BlockSpec((tm, d), lambda i, j: (i, 0)),
            pl.BlockSpec((1, d), row), pl.BlockSpec((1, d), row), pl.BlockSpec((1, d), row),
            pl.BlockSpec((None, d, tn), lambda i, j: (layer, 0, j)),
        ],
        out_specs=pl.BlockSpec((tm, tn), lambda i, j: (i, j)),
        out_shape=jax.ShapeDtypeStruct((m, n), out_dtype),
        scratch_shapes=[pltpu.VMEM((tm, d), F32 if precise else BF16), pltpu.VMEM((tm, LANES), F32)],
        compiler_params=_params("parallel", "arbitrary"),
        name="norm_mod_matmul",
    )(h, g, shift, scale, w)


def _mgr_kernel(a_ref, w_ref, h_ref, gate_ref, o_ref):
    o_ref[...] = h_ref[...] + gate_ref[...] * _dot(a_ref[...], w_ref[...])


def matmul_gated_residual(a, w, layer, h, gate):
    m, k = a.shape
    n = w.shape[2]
    tm = min(m, 512)
    tn = n if k * n * 2 <= 8 * 1024 * 1024 else min(n, 1024)
    return pl.pallas_call(
        _mgr_kernel,
        grid=(m // tm, n // tn),
        in_specs=[
            pl.BlockSpec((tm, k), lambda i, j: (i, 0)),
            pl.BlockSpec((None, k, tn), lambda i, j: (layer, 0, j)),
            pl.BlockSpec((tm, tn), lambda i, j: (i, j)),
            pl.BlockSpec((1, tn), lambda i, j: (0, j)),
        ],
        out_specs=pl.BlockSpec((tm, tn), lambda i, j: (i, j)),
        out_shape=jax.ShapeDtypeStruct((m, n), F32),
        compiler_params=_params("parallel", "parallel"),
        name="matmul_gated_residual",
    )(a, w, h, gate)


def _ctx_attn_kernel(q_ref, k_ref, v_ref, o_ref):
    q = q_ref[...]
    s = _dot_nt(q, k_ref[...]) * (q.shape[-1] ** -0.5)
    m = jnp.max(s, axis=-1, keepdims=True)
    p = jnp.exp(s - m)
    l = jnp.sum(p, axis=-1, keepdims=True)
    o_ref[...] = (_dot(p.astype(BF16), v_ref[...]) / l).astype(o_ref.dtype)


def ctx_attention(qkv, n_heads, q_blk, k_blk, v_blk):
    n_ctx = qkv.shape[0]
    spec = lambda f: pl.BlockSpec((n_ctx, LANES), lambda h: (0, f(h)))
    return pl.pallas_call(
        _ctx_attn_kernel,
        grid=(n_heads,),
        in_specs=[spec(q_blk), spec(k_blk), spec(v_blk)],
        out_specs=pl.BlockSpec((n_ctx, LANES), lambda h: (0, h)),
        out_shape=jax.ShapeDtypeStruct((n_ctx, n_heads * LANES), BF16),
        compiler_params=_params("parallel"),
        name="ctx_attention",
    )(qkv, qkv, qkv)


def _na_kernel(q_ref, k_ref, v_ref, kc_ref, vc_ref, pair_ref, *rest, n_rows):
    mask_refs, o_ref = rest[:-1], rest[-1]
    nq = NA_Q_ROWS * GRID_W
    n_band = NA_BAND_ROWS * GRID_W
    scale = q_ref.shape[-1] ** -0.5 * LOG2_E
    for blk, mask_ref in enumerate(mask_refs):
        r0 = (pl.program_id(1) * len(mask_refs) + blk) * NA_Q_ROWS
        b0 = jnp.clip(r0 - NA_WIN_ROWS // 2, 0, n_rows - NA_BAND_ROWS)
        start = pl.multiple_of(b0 * GRID_W, GRID_W)
        q = q_ref[blk * nq:(blk + 1) * nq, :]
        bias = jnp.concatenate([
            jnp.concatenate([
                pair_ref[0, jnp.clip(b0 + 2 * t - r0 - qi + NA_WIN_ROWS, 0, 2 * NA_WIN_ROWS - 1)]
                for t in range(NA_BAND_ROWS // 2)], axis=1)
            for qi in range(NA_Q_ROWS)], axis=0)
        keys = jnp.concatenate([k_ref[pl.ds(start, n_band), :], kc_ref[...]], axis=0)
        vals = jnp.concatenate([v_ref[pl.ds(start, n_band), :], vc_ref[...]], axis=0)
        s = _dot_nt(q, keys) * scale
        s_w = s[:, :n_band] + (bias + mask_ref[0])
        s_c = s[:, n_band:]
        m = jnp.maximum(jnp.max(s_w, axis=-1, keepdims=True), jnp.max(s_c, axis=-1, keepdims=True))
        p = jnp.concatenate([jnp.exp2(s_w - m), jnp.exp2(s_c - m)], axis=1)
        l = jnp.sum(p, axis=-1, keepdims=True)
        o_ref[blk * nq:(blk + 1) * nq, :] = (_dot(p.astype(BF16), vals) / l).astype(o_ref.dtype)


def _na_window_masks(n_rows):
    nq, nk = NA_Q_ROWS * GRID_W, NA_BAND_ROWS * GRID_W
    valid = []
    for r0 in (0, NA_Q_ROWS, n_rows - NA_Q_ROWS):
        b0 = int(np.clip(r0 - NA_WIN_ROWS // 2, 0, n_rows - NA_BAND_ROWS))
        qr = r0 + np.arange(nq) // GRID_W
        qc = np.arange(nq) % GRID_W
        kr = b0 + np.arange(nk) // GRID_W
        kc = np.arange(nk) % GRID_W
        rs = np.clip(qr - NA_WIN_ROWS // 2, 0, n_rows - NA_WIN_ROWS)
        cs = np.clip(qc - NA_WIN_COLS // 2, 0, GRID_W - NA_WIN_COLS)
        ok = ((kr[None] >= rs[:, None]) & (kr[None] < rs[:, None] + NA_WIN_ROWS)
              & (kc[None] >= cs[:, None]) & (kc[None] < cs[:, None] + NA_WIN_COLS))
        valid.append(ok)
    return np.where(np.stack(valid), 0.0, MASK_VALUE).astype(np.float32)


def _na_pair_tiles(rpb):
    n_heads = rpb.shape[0]
    dcol = np.arange(GRID_W)[None, :] - np.arange(GRID_W)[:, None] + NA_WIN_COLS - 1
    in_table = (dcol >= 0) & (dcol < 2 * NA_WIN_COLS - 1)
    onehot = (np.clip(dcol, 0, 2 * NA_WIN_COLS - 2)[None] == np.arange(2 * NA_WIN_COLS - 1)[:, None, None]) & in_table
    toeplitz = jnp.einsum("hrd,dqk->hrqk", rpb, jnp.asarray(onehot, F32), precision=lax.Precision.HIGHEST)
    zero = jnp.zeros((n_heads, 1, GRID_W, GRID_W), F32)
    padded = jnp.concatenate([zero, toeplitz, zero], axis=1)
    return jnp.concatenate([padded[:, :-1], padded[:, 1:]], axis=-1)


def neighbourhood_attention(qkv, qkv_c, rpb):
    n_tok = qkv.shape[0]
    n_ctx = qkv_c.shape[0]
    n_rows = n_tok // GRID_W
    n_blocks = n_rows // NA_Q_ROWS
    nb = NA_BLOCKS_PER_STEP
    nq, nk = NA_Q_ROWS * GRID_W, NA_BAND_ROWS * GRID_W
    assert NA_BAND_ROWS % 2 == 0
    pairs = _na_pair_tiles(rpb) * LOG2_E
    masks = jnp.asarray(_na_window_masks(n_rows))
    h_ = NA_HEADS

    def mask_spec(blk):
        def index(h, s):
            rb = s * nb + blk
            return (jnp.where(rb == 0, 0, jnp.where(rb == n_blocks - 1, 2, 1)), 0, 0)
        return pl.BlockSpec((1, nq, nk), index)

    return pl.pallas_call(
        functools.partial(_na_kernel, n_rows=n_rows),
        grid=(h_, n_blocks // nb),
        in_specs=[
            pl.BlockSpec((nb * nq, LANES), lambda h, s: (s, h)),
            pl.BlockSpec((n_tok, LANES), lambda h, s: (0, h_ + h)),
            pl.BlockSpec((n_tok, LANES), lambda h, s: (0, 2 * h_ + h)),
            pl.BlockSpec((n_ctx, LANES), lambda h, s: (0, h_ + h)),
            pl.BlockSpec((n_ctx, LANES), lambda h, s: (0, 2 * h_ + h)),
            pl.BlockSpec((1,) + pairs.shape[1:], lambda h, s: (h, 0, 0, 0)),
        ] + [mask_spec(blk) for blk in range(nb)],
        out_specs=pl.BlockSpec((nb * nq, LANES), lambda h, s: (s, h)),
        out_shape=jax.ShapeDtypeStruct((n_tok, h_ * LANES), BF16),
        compiler_params=_params("parallel", "arbitrary"),
        name="neighbourhood_attention",
    )(qkv, qkv, qkv, qkv_c, qkv_c, pairs, *([masks] * nb))


def _gqa_proj_kernel(h_ref, g_ref, sh_ref, sc_ref, w_ref, qn_ref, kn_ref, cos_ref, sin_ref, o_ref, xn_sc,
                     rs_sc, *, rope, q_scale):
    j = pl.program_id(1)
    tn = o_ref.shape[1]
    n_q = GQA_HEADS * LANES // tn

    @pl.when(j == 0)
    def _():
        _norm_modulate_into(xn_sc, h_ref, g_ref, sh_ref, sc_ref, rs_sc)

    y = _dot(xn_sc[...], w_ref[...].astype(BF16))

    def normed(w, mult):
        for s in range(tn // LANES):
            cols = slice(s * LANES, (s + 1) * LANES)
            x = y[:, cols]
            var = jnp.mean(x * x, axis=-1, keepdims=True)
            t = (x * lax.rsqrt(var + EPS)) * w
            if rope:
                t = t * cos_ref[...] + pltpu.roll(t, LANES // 2, 1) * sin_ref[...]
            if mult != 1.0:
                t = t * mult
            o_ref[:, cols] = t.astype(o_ref.dtype)

    @pl.when(j < n_q)
    def _():
        normed(qn_ref[...], q_scale)

    @pl.when(j == n_q)
    def _():
        normed(kn_ref[...], 1.0)

    @pl.when(j > n_q)
    def _():
        o_ref[...] = y.astype(o_ref.dtype)


def gqa_projection(h, g, shift, scale, w, layer, q_norm, k_norm, cos_full, sin_signed, rope, q_scale):
    m, d = h.shape
    n = w.shape[2]
    tm = min(m, 1024)
    tn = GQA_KV_HEADS * LANES
    row = lambda i, j: (0, 0)
    head = pl.BlockSpec((1, LANES), row)
    pos = pl.BlockSpec((tm, LANES), lambda i, j: (i, 0))
    return pl.pallas_call(
        functools.partial(_gqa_proj_kernel, rope=rope, q_scale=q_scale),
        grid=(m // tm, n // tn),
        in_specs=[
            pl.BlockSpec((tm, d), lambda i, j: (i, 0)),
            pl.BlockSpec((1, d), row), pl.BlockSpec((1, d), row), pl.BlockSpec((1, d), row),
            pl.BlockSpec((None, d, tn), lambda i, j: (layer, 0, j)),
            head, head, pos, pos,
        ],
        out_specs=pl.BlockSpec((tm, tn), lambda i, j: (i, j)),
        out_shape=jax.ShapeDtypeStruct((m, n), BF16),
        scratch_shapes=[pltpu.VMEM((tm, d), BF16), pltpu.VMEM((tm, LANES), F32)],
        compiler_params=_params("parallel", "arbitrary"),
        name="gqa_projection",
    )(h, g, shift, scale, w, q_norm[None], k_norm[None], cos_full, sin_signed)


def _gqa_kernel(q_ref, k_ref, v_ref, kc_ref, vc_ref, o_ref, m_sc, l_sc, acc_sc, *, tk):
    tq = q_ref.shape[0]
    n_tok = k_ref.shape[0]
    q = jnp.concatenate([q_ref[:, g * LANES:(g + 1) * LANES] for g in range(GQA_GROUP)], axis=0)
    m_sc[...] = jnp.full_like(m_sc, -jnp.inf)
    l_sc[...] = jnp.zeros_like(l_sc)
    acc_sc[...] = jnp.zeros_like(acc_sc)

    def update(kb, vb):
        s = _dot_nt(q, kb)
        blocks = [s[:, c * LANES:(c + 1) * LANES] for c in range(kb.shape[0] // LANES)]
        m_old = m_sc[...]
        block_max = jnp.max(functools.reduce(jnp.maximum, blocks), axis=-1, keepdims=True)
        m_new = jnp.maximum(m_old, block_max)
        alpha = jnp.exp2(m_old - m_new)
        p = [jnp.exp2(b - m_new) for b in blocks]
        l_sc[...] = alpha * l_sc[...] + functools.reduce(jnp.add, p)
        acc_sc[...] = alpha * acc_sc[...] + _dot(jnp.concatenate(p, axis=-1).astype(BF16), vb)
        m_sc[...] = m_new

    def body(j, carry):
        r = pl.multiple_of(j * tk, tk)
        update(k_ref[pl.ds(r, tk), :], v_ref[pl.ds(r, tk), :])
        return carry

    lax.fori_loop(0, n_tok // tk, body, 0, unroll=True)
    update(kc_ref[...], vc_ref[...])
    o = acc_sc[...] / jnp.sum(l_sc[...], axis=-1, keepdims=True)
    for g in range(GQA_GROUP):
        o_ref[:, g * LANES:(g + 1) * LANES] = o[g * tq:(g + 1) * tq].astype(o_ref.dtype)


def gqa_attention(qkv, qkv_c):
    n_tok = qkv.shape[0]
    n_ctx = qkv_c.shape[0]
    tq = min(n_tok, 512)
    tk = min(n_tok, 1024)
    kv0 = GQA_HEADS
    v0 = GQA_HEADS + GQA_KV_HEADS
    rows = GQA_GROUP * tq
    return pl.pallas_call(
        functools.partial(_gqa_kernel, tk=tk),
        grid=(GQA_KV_HEADS, n_tok // tq),
        in_specs=[
            pl.BlockSpec((tq, GQA_GROUP * LANES), lambda kv, i: (i, kv)),
            pl.BlockSpec((n_tok, LANES), lambda kv, i: (0, kv0 + kv)),
            pl.BlockSpec((n_tok, LANES), lambda kv, i: (0, v0 + kv)),
            pl.BlockSpec((n_ctx, LANES), lambda kv, i: (0, kv0 + kv)),
            pl.BlockSpec((n_ctx, LANES), lambda kv, i: (0, v0 + kv)),
        ],
        out_specs=pl.BlockSpec((tq, GQA_GROUP * LANES), lambda kv, i: (i, kv)),
        out_shape=jax.ShapeDtypeStruct((n_tok, GQA_HEADS * LANES), BF16),
        scratch_shapes=[pltpu.VMEM((rows, LANES), F32)] * 3,
        compiler_params=_params("parallel", "arbitrary"),
        name="gqa_attention",
    )(qkv, qkv, qkv, qkv_c, qkv_c)


def axial_rope_tables(n_tok):
    t = jnp.arange(n_tok, dtype=jnp.int32)
    row = (t // GRID_W).astype(F32)
    col = (t % GRID_W).astype(F32)
    n_freq = LANES // 4
    inv_freq = ROPE_THETA ** (-jnp.arange(n_freq, dtype=F32) / n_freq)
    ang = jnp.concatenate([row[:, None] * inv_freq, col[:, None] * inv_freq], axis=-1)
    cos, sin = jnp.cos(ang), jnp.sin(ang)
    return jnp.concatenate([cos, cos], axis=-1), jnp.concatenate([-sin, sin], axis=-1)


def _conv_silu_kernel(prev_ref, cur_ref, next_ref, w_ref, b_ref, o_ref):
    i = pl.program_id(0)
    tl = cur_ref.shape[0]
    prev = jnp.where(i == 0, 0.0, prev_ref[...].astype(F32))
    nxt = jnp.where(i == pl.num_programs(0) - 1, 0.0, next_ref[...].astype(F32))
    ext = jnp.concatenate([prev, cur_ref[...].astype(F32), nxt], axis=0)
    n_ext = ext.shape[0]
    pad = (SSM_CONV_W - 1) // 2
    acc = jnp.zeros((tl, ext.shape[1]), F32) + b_ref[...]
    for k in range(SSM_CONV_W):
        shifted = ext if k == pad else pltpu.roll(ext, (pad - k) % n_ext, 0)
        acc = acc + shifted[CONV_HALO:CONV_HALO + tl] * w_ref[k:k + 1, :]
    o_ref[...] = _silu(acc).astype(o_ref.dtype)


def conv_silu(zxbc, conv_w, conv_b):
    n = zxbc.shape[0]
    tl = min(n, 512)
    tc = 512
    c0 = SSM_D_INNER // tc
    hb = tl // CONV_HALO
    n_halo = n // CONV_HALO
    return pl.pallas_call(
        _conv_silu_kernel,
        grid=(n // tl, SSM_CONV_DIM // tc),
        in_specs=[
            pl.BlockSpec((CONV_HALO, tc), lambda i, j: (jnp.maximum(i * hb - 1, 0), c0 + j)),
            pl.BlockSpec((tl, tc), lambda i, j: (i, c0 + j)),
            pl.BlockSpec((CONV_HALO, tc), lambda i, j: (jnp.minimum((i + 1) * hb, n_halo - 1), c0 + j)),
            pl.BlockSpec((SSM_CONV_W, tc), lambda i, j: (0, j)),
            pl.BlockSpec((1, tc), lambda i, j: (0, j)),
        ],
        out_specs=pl.BlockSpec((tl, tc), lambda i, j: (i, j)),
        out_shape=jax.ShapeDtypeStruct((n, SSM_CONV_DIM), BF16),
        compiler_params=_params("parallel", "parallel"),
        name="conv_silu",
    )(zxbc, zxbc, zxbc, conv_w, conv_b)


def _ssd_prep_kernel(dt_ref, bias_ref, alog_ref, delta_ref, acs_ref):
    x = dt_ref[...] + bias_ref[...]
    delta = jnp.maximum(x, 0.0) + jnp.log1p(jnp.exp(-jnp.abs(x)))
    da = delta * (-jnp.exp(alog_ref[...]))
    q = x.shape[0]
    i = lax.broadcasted_iota(jnp.int32, (q, q), 0)
    j = lax.broadcasted_iota(jnp.int32, (q, q), 1)
    hi = lax.Precision.HIGHEST
    fwd = jnp.dot((i >= j).astype(F32), da, preferred_element_type=F32, precision=hi)
    bwd = jnp.dot((i <= j).astype(F32), da, preferred_element_type=F32, precision=hi)
    lane = lax.broadcasted_iota(jnp.int32, da.shape, 1)
    delta_ref[...] = delta
    acs_ref[...] = jnp.where(lane < SSM_HEADS, fwd, bwd)


def ssd_prep(dt, dt_bias, a_log):
    n = dt.shape[0]
    q = SSM_CHUNK
    blk = pl.BlockSpec((q, 2 * SSM_HEADS), lambda c: (c, 0))
    row = pl.BlockSpec((1, 2 * SSM_HEADS), lambda c: (0, 0))
    return pl.pallas_call(
        _ssd_prep_kernel,
        grid=(n // q,),
        in_specs=[blk, row, row],
        out_specs=[blk, blk],
        out_shape=[jax.ShapeDtypeStruct(dt.shape, F32)] * 2,
        compiler_params=_params("parallel"),
        name="ssd_prep",
    )(dt, dt_bias.reshape(1, -1), a_log.reshape(1, -1))


def _ssd_kernel(x_ref, b_ref, c_ref, ac_ref, ar_ref, dr_ref, init_ref, y_ref, fin_ref, state_sc, *, n_sub):
    d = pl.program_id(0)
    q = SSM_CHUNK

    @pl.when(pl.program_id(2) == 0)
    def _():
        state_sc[...] = init_ref[0]

    i = lax.broadcasted_iota(jnp.int32, (q, q), 0)
    j = lax.broadcasted_iota(jnp.int32, (q, q), 1)
    causal = jnp.where(d == 0, i - j, j - i) >= 0
    lo = lax.broadcasted_iota(jnp.int32, (q, LANES), 1) < SSM_HEAD_DIM

    def chunk(k, carry):
        ci = k + d * (n_sub - 1 - 2 * k)
        r = pl.multiple_of(ci * q, q)
        bm = b_ref[pl.ds(r, q), :]
        cm = c_ref[pl.ds(r, q), :]
        ac = ac_ref[0, 0, pl.ds(r, q), :]
        ar = ar_ref[0, 0, ci]
        dr = dr_ref[0, 0, ci]
        tot = jnp.where(d == 0, ar[:, q - 1:q], ar[:, 0:1])
        w_out = jnp.exp(tot - ar) * dr
        e_tot = jnp.exp(tot)
        cb = _dot_nt(cm, bm)
        bt = bm.astype(F32).T
        cf = cm.astype(F32)
        state = state_sc[...]
        for p in range(SSM_HEADS_PER_GROUP // 2):
            cols = slice(p * LANES, (p + 1) * LANES)
            x = x_ref[pl.ds(r, q), cols]
            lhs_y, lhs_s = [], []
            for hd in (2 * p, 2 * p + 1):
                a_i = jnp.broadcast_to(ac[:, hd:hd + 1], (q, q))
                seg = jnp.where(causal, a_i - ar[hd:hd + 1, :], -jnp.inf)
                intra = cb * jnp.exp(seg) * dr[hd:hd + 1, :]
                lhs_y.append(jnp.concatenate([intra, cf * jnp.exp(a_i)], axis=1).astype(BF16))
                lhs_s.append((bt * w_out[hd:hd + 1, :]).astype(BF16))
            rhs = jnp.concatenate([x, state[:, cols].astype(BF16)], axis=0)
            res = _dot(jnp.concatenate(lhs_y, axis=0), rhs)
            y_ref[0, pl.ds(r, q), cols] = jnp.where(lo, res[:q], res[q:]).astype(y_ref.dtype)
            s_res = _dot(jnp.concatenate(lhs_s, axis=0), x)
            s_new = jnp.where(lo, s_res[:SSM_STATE], s_res[SSM_STATE:])
            keep = jnp.where(lo[0:1], e_tot[2 * p:2 * p + 1, :], e_tot[2 * p + 1:2 * p + 2, :])
            state_sc[:, cols] = state[:, cols] * keep + s_new
        return carry

    lax.fori_loop(0, n_sub, chunk, 0, unroll=True)

    @pl.when(pl.program_id(2) == pl.num_programs(2) - 1)
    def _():
        fin_ref[0] = state_sc[...]


def ssd_scan(xbc, delta, acs, init_state):
    assert SSM_STATE == SSM_CHUNK == LANES
    n = xbc.shape[0]
    q = SSM_CHUNK
    nc = n // q
    n_sub = min(nc, SSD_CHUNKS_PER_STEP)
    ns = nc // n_sub
    rows = n_sub * q
    hg = SSM_HEADS_PER_GROUP
    gw = hg * SSM_HEAD_DIM
    ac_col = acs.reshape(n, 2, SSM_GROUPS, hg).transpose(1, 2, 0, 3)
    row = lambda t: t.reshape(nc, q, 2, SSM_GROUPS, hg).transpose(2, 3, 0, 4, 1)
    ac_row, dl_row = row(acs), row(delta)
    b0 = SSM_D_INNER // SSM_STATE
    c0 = b0 + SSM_GROUPS
    step = lambda d, s: s + d * (ns - 1 - 2 * s)
    row_spec = pl.BlockSpec((1, 1, n_sub, hg, q), lambda d, g, s: (d, g, step(d, s), 0, 0))
    state_spec = pl.BlockSpec((1, SSM_STATE, gw), lambda d, g, s: (d, 0, g))
    return pl.pallas_call(
        functools.partial(_ssd_kernel, n_sub=n_sub),
        grid=(2, SSM_GROUPS, ns),
        in_specs=[
            pl.BlockSpec((rows, gw), lambda d, g, s: (step(d, s), g)),
            pl.BlockSpec((rows, SSM_STATE), lambda d, g, s: (step(d, s), b0 + g)),
            pl.BlockSpec((rows, SSM_STATE), lambda d, g, s: (step(d, s), c0 + g)),
            pl.BlockSpec((1, 1, rows, hg), lambda d, g, s: (d, g, step(d, s), 0)),
            row_spec, row_spec,
            state_spec,
        ],
        out_specs=[pl.BlockSpec((1, rows, gw), lambda d, g, s: (d, step(d, s), g)), state_spec],
        out_shape=[jax.ShapeDtypeStruct((2, n, SSM_D_INNER), BF16),
                   jax.ShapeDtypeStruct((2, SSM_STATE, SSM_D_INNER), F32)],
        scratch_shapes=[pltpu.VMEM((SSM_STATE, gw), F32)],
        compiler_params=_params("parallel", "parallel", "arbitrary"),
        name="ssd_scan",
    )(xbc, xbc, xbc, ac_col, ac_row, dl_row, init_state)


def _ssm_out_kernel(yf_ref, yb_ref, x_ref, z_ref, dsum_ref, g_ref, w_ref, h_ref, gate_ref, o_ref):
    y = yf_ref[0].astype(F32) + yb_ref[0].astype(F32) + dsum_ref[...] * x_ref[...].astype(F32)
    v = y * _silu(z_ref[...].astype(F32))
    var = jnp.mean(v * v, axis=-1, keepdims=True)
    yn = ((v * lax.rsqrt(var + EPS)) * g_ref[...]).astype(BF16)
    o_ref[...] = h_ref[...] + gate_ref[...] * _dot(yn, w_ref[...])


def ssm_out_projection(y, xbc, zxbc, dsum, norm_g, w, layer, h, gate):
    n, d = h.shape
    tl = min(n, 256)
    di = SSM_D_INNER
    row = pl.BlockSpec((1, di), lambda i: (0, 0))
    return pl.pallas_call(
        _ssm_out_kernel,
        grid=(n // tl,),
        in_specs=[
            pl.BlockSpec((1, tl, di), lambda i: (0, i, 0)),
            pl.BlockSpec((1, tl, di), lambda i: (1, i, 0)),
            pl.BlockSpec((tl, di), lambda i: (i, 0)),
            pl.BlockSpec((tl, di), lambda i: (i, 0)),
            row, row,
            pl.BlockSpec((None, di, d), lambda i: (layer, 0, 0), pipeline_mode=pl.Buffered(1)),
            pl.BlockSpec((tl, d), lambda i: (i, 0)),
            pl.BlockSpec((1, d), lambda i: (0, 0)),
        ],
        out_specs=pl.BlockSpec((tl, d), lambda i: (i, 0)),
        out_shape=jax.ShapeDtypeStruct((n, d), F32),
        compiler_params=_params("parallel"),
        name="ssm_out_projection",
    )(y, y, xbc, zxbc, dsum, norm_g, w, h, gate)


def mamba2_bidirectional(h, hc, g, mod_l, mod_c, w_in, w_out_bf16, layer, conv_w, conv_b, a_log, dt_bias,
                         d_skip, norm_g):
    n_main = SSM_D_INNER + SSM_CONV_DIM
    w_dt = w_in[layer, :, n_main:][None]
    dsum = jnp.repeat(d_skip[0] + d_skip[1], SSM_HEAD_DIM)[None]
    state = jnp.zeros((2, SSM_STATE, SSM_D_INNER), F32)
    outs = []
    for t, mod in ((hc, mod_c), (h, mod_l)):
        shift, scale, gate = mod[3:4], mod[4:5], mod[5:6]
        zxbc = norm_mod_matmul(t, g, shift, scale, w_in, layer, BF16, n=n_main)
        dt = norm_mod_matmul(t, g, shift, scale, w_dt, 0, F32, precise=True)
        xbc = conv_silu(zxbc, conv_w, conv_b[None])
        delta, acs = ssd_prep(dt, dt_bias, a_log)
        y, state = ssd_scan(xbc, delta, acs, state)
        outs.append(ssm_out_projection(y, xbc, zxbc, dsum, norm_g[None], w_out_bf16, layer, t, gate))
    return outs[1], outs[0]


def kernel(x, c, ctx, c_ctx, ada_w, ada_b, norm_g, ffn_w_in, ffn_w_out, na_w_qkv, na_rpb, na_w_o, ssm_w_in, ssm_conv_w, ssm_conv_b, ssm_a_log, ssm_dt_bias, ssm_d, ssm_norm_g, ssm_w_out, gqa_w_qkv, gqa_q_norm, gqa_k_norm, gqa_w_o, final_norm_g):
    bsz, n_tok, d = x.shape
    assert bsz == 1, "the kernels take one sequence"
    depth = ada_w.shape[0]
    h, hc = x[0], ctx[0]
    mod = adaln_mod(c, c_ctx, ada_w, ada_b)
    cos_full, sin_signed = axial_rope_tables(n_tok)
    rope_off = jnp.zeros((ctx.shape[1], LANES), F32)
    ffn_w = {(0, 0): (ffn_w_in[0, 0].astype(BF16), ffn_w_out[0, 0].astype(BF16))}
    na_w_o_b, ssm_w_out_b, gqa_w_o_b = na_w_o.astype(BF16), ssm_w_out.astype(BF16), gqa_w_o.astype(BF16)
    gqa_q_scale = LANES ** -0.5 * LOG2_E

    def ffn(t, m, i, k, latent, final_g=None):
        w_in_b, w_out_b = ffn_w[(i, k)]
        nxt = (i + (k + 1) // 2, (k + 1) % 2)
        cast_next = latent and nxt[0] < depth
        out = ffn_half_step(t, norm_g[i, 2 * k][None], m[6 * k:6 * k + 1], m[6 * k + 1:6 * k + 2],
                            m[6 * k + 2:6 * k + 3], w_in_b, w_out_b,
                            (ffn_w_in, ffn_w_out) + nxt if cast_next else None, final_g)
        if cast_next:
            out, *ffn_w[nxt] = out
        return out

    for i in range(depth):
        last = i == depth - 1
        ml, mc = mod[i, 0], mod[i, 1]
        g_mix = norm_g[i, 1][None]
        h = ffn(h, ml, i, 0, True)
        hc = ffn(hc, mc, i, 0, False)
        kind, j = i % 3, i // 3
        if kind == 0:
            qkv = norm_mod_matmul(h, g_mix, ml[3:4], ml[4:5], na_w_qkv, j, BF16)
            qkv_c = norm_mod_matmul(hc, g_mix, mc[3:4], mc[4:5], na_w_qkv, j, BF16)
            o = neighbourhood_attention(qkv, qkv_c, na_rpb[j])
            h = matmul_gated_residual(o, na_w_o_b, j, h, ml[5:6])
            if not last:
                oc = ctx_attention(qkv_c, NA_HEADS, lambda hd: hd, lambda hd: NA_HEADS + hd,
                                   lambda hd: 2 * NA_HEADS + hd)
                hc = matmul_gated_residual(oc, na_w_o_b, j, hc, mc[5:6])
        elif kind == 1:
            h, hc = mamba2_bidirectional(h, hc, g_mix, ml, mc, ssm_w_in, ssm_w_out_b, j, ssm_conv_w[j],
                                         ssm_conv_b[j], ssm_a_log[j], ssm_dt_bias[j], ssm_d[j], ssm_norm_g[j])
        else:
            qkv = gqa_projection(h, g_mix, ml[3:4], ml[4:5], gqa_w_qkv, j, gqa_q_norm[j], gqa_k_norm[j],
                                 cos_full, sin_signed, True, gqa_q_scale)
            qkv_c = gqa_projection(hc, g_mix, mc[3:4], mc[4:5], gqa_w_qkv, j, gqa_q_norm[j], gqa_k_norm[j],
                                   rope_off, rope_off, False, 1.0)
            o = gqa_attention(qkv, qkv_c)
            h = matmul_gated_residual(o, gqa_w_o_b, j, h, ml[5:6])
            if not last:
                oc = ctx_attention(qkv_c, GQA_HEADS, lambda hd: hd, lambda hd: GQA_HEADS + hd // GQA_GROUP,
                                   lambda hd: GQA_HEADS + GQA_KV_HEADS + hd // GQA_GROUP)
                hc = matmul_gated_residual(oc, gqa_w_o_b, j, hc, mc[5:6])
        h = ffn(h, ml, i, 1, True, final_norm_g[None] if last else None)
        if not last:
            hc = ffn(hc, mc, i, 1, False)
    return h[None]
```

```python
import functools

import jax
import jax.numpy as jnp
import numpy as np
from jax import lax
from jax.experimental import pallas as pl
from jax.experimental.pallas import tpu as pltpu

F32 = jnp.float32
BF16 = jnp.bfloat16

EPS = 1e-6
HALF_STEP = 0.5
N_MOD = 9
GRID_W = 64
LANES = 128
NORM_ROWS = 16

NA_HEADS = 16
NA_WIN_ROWS = 8
NA_WIN_COLS = 16
NA_Q_ROWS = 4
NA_BAND_ROWS = 12
NA_BLOCKS_PER_STEP = 8

SSM_HEADS = 64
SSM_HEAD_DIM = 64
SSM_GROUPS = 8
SSM_STATE = 128
SSM_CHUNK = 128
SSM_D_INNER = SSM_HEADS * SSM_HEAD_DIM
SSM_GN = SSM_GROUPS * SSM_STATE
SSM_CONV_W = 5
SSM_CONV_DIM = SSM_D_INNER + 2 * SSM_GN
SSM_HEADS_PER_GROUP = SSM_HEADS // SSM_GROUPS
SSD_CHUNKS_PER_STEP = 16
CONV_HALO = 16

GQA_HEADS = 16
GQA_KV_HEADS = 4
GQA_GROUP = GQA_HEADS // GQA_KV_HEADS
ROPE_THETA = 10000.0
LOG2_E = 1.4426950408889634

MASK_VALUE = -1e30
VMEM_LIMIT = 56 * 1024 * 1024


def _params(*sem):
    return pltpu.CompilerParams(dimension_semantics=sem, vmem_limit_bytes=VMEM_LIMIT)


def _silu(x):
    half = 0.5 * x
    return half + half * jnp.tanh(half)


def _dot(a, b):
    return jnp.dot(a, b, preferred_element_type=F32)


def _dot_nt(a, b):
    return lax.dot_general(a, b, (((1,), (1,)), ((), ())), preferred_element_type=F32)


def _rowwise(n_rows, chunk, body, unroll=2):
    def step(i, carry):
        body(pl.ds(pl.multiple_of(i * chunk, chunk), chunk))
        return carry

    lax.fori_loop(0, n_rows // chunk, step, 0, unroll=unroll)


def _norm_modulate_into(dst_ref, src_ref, g_ref, sh_ref, sc_ref, rs_ref):
    gain = g_ref[...] * (1.0 + sc_ref[...])
    shift = sh_ref[...]

    n_rows, d = src_ref.shape

    def stats(rows):
        x = src_ref[rows, :]
        sq = x * x
        rs_ref[rows, :] = functools.reduce(jnp.add, [sq[:, c * LANES:(c + 1) * LANES] for c in range(d // LANES)])

    def scale(rows):
        x = src_ref[rows, :]
        rs = jnp.concatenate([rs_ref[rows, :]] * (d // LANES), axis=1)
        dst_ref[rows, :] = ((x * rs) * gain + shift).astype(dst_ref.dtype)

    _rowwise(n_rows, NORM_ROWS, stats, unroll=4)
    var = jnp.sum(rs_ref[...], axis=-1, keepdims=True) * (1.0 / d)
    rs_ref[...] = jnp.broadcast_to(lax.rsqrt(var + EPS), rs_ref.shape)
    _rowwise(n_rows, NORM_ROWS, scale, unroll=2)


def _adaln_kernel(c_ref, w_ref, b_ref, o_ref, s_sc, *, rows, group):
    tk, tn = w_ref.shape
    s_sc[...] = _silu(c_ref[...])

    @pl.when(pl.program_id(2) == 0)
    def _():
        o_ref[...] = jnp.broadcast_to(b_ref[...], o_ref.shape)

    for lg in range(tn // group):
        cols = slice(lg * group, (lg + 1) * group)

        def body(i, carry):
            a0, a1 = carry
            r = pl.multiple_of(i * rows, rows)
            w = w_ref[pl.ds(r, rows), cols]
            s0 = jnp.concatenate([s_sc[0, pl.ds(r, rows), :]] * (group // LANES), axis=1)
            s1 = jnp.concatenate([s_sc[1, pl.ds(r, rows), :]] * (group // LANES), axis=1)
            return a0 + w * s0, a1 + w * s1

        zero = jnp.zeros((rows, group), F32)
        a0, a1 = lax.fori_loop(0, tk // rows, body, (zero, zero), unroll=2)
        o_ref[0:1, cols] += jnp.sum(a0, axis=0, keepdims=True)
        o_ref[1:2, cols] += jnp.sum(a1, axis=0, keepdims=True)


def adaln_mod(c, c_ctx, ada_w, ada_b):
    depth, d, n = ada_w.shape
    tk = 256
    tn = n // 4 if n % 2048 == 0 else 512
    cb = jnp.broadcast_to(jnp.stack([c[0], c_ctx])[:, :, None], (2, d, LANES))
    out = pl.pallas_call(
        functools.partial(_adaln_kernel, rows=16, group=512),
        grid=(depth, n // tn, d // tk),
        in_specs=[
            pl.BlockSpec((2, tk, LANES), lambda l, j, k: (0, k, 0)),
            pl.BlockSpec((None, tk, tn), lambda l, j, k: (l, k, j)),
            pl.BlockSpec((None, 1, tn), lambda l, j, k: (l, 0, j)),
        ],
        out_specs=pl.BlockSpec((None, 2, tn), lambda l, j, k: (l, 0, j)),
        out_shape=jax.ShapeDtypeStruct((depth, 2, n), F32),
        scratch_shapes=[pltpu.VMEM((2, tk, LANES), F32)],
        compiler_params=_params("parallel", "parallel", "arbitrary"),
        name="adaln_mod",
    )(cb, ada_w, ada_b.reshape(depth, 1, n))
    return out.reshape(depth, 2, N_MOD, d)


def _ffn_kernel(h_ref, g_ref, sh_ref, sc_ref, gate_ref, wa_ref, wb_ref, wo_ref, fg_ref, *rest, final_norm, cast_next):
    if cast_next:
        nwi_ref, nwo_ref, o_ref, nwi_out, nwo_out, xn_sc, acc_sc, rs_sc = rest
    else:
        o_ref, xn_sc, acc_sc, rs_sc = rest
    j = pl.program_id(1)

    @pl.when(j == 0)
    def _():
        _norm_modulate_into(xn_sc, h_ref, g_ref, sh_ref, sc_ref, rs_sc)
        acc_sc[...] = jnp.zeros_like(acc_sc)

    if cast_next:
        nwi_out[...] = nwi_ref[...].astype(BF16)
        nwo_out[...] = nwo_ref[...].astype(BF16)
    xn = xn_sc[...]
    a = _dot(xn, wa_ref[...])
    b = _dot(xn, wb_ref[...])
    hm = (_silu(a) * b).astype(BF16)
    acc_sc[...] += _dot(hm, wo_ref[...])

    @pl.when(j == pl.num_programs(1) - 1)
    def _():
        out = h_ref[...] + (HALF_STEP * gate_ref[...]) * acc_sc[...]
        if final_norm:
            var = jnp.mean(out * out, axis=-1, keepdims=True)
            out = (out * lax.rsqrt(var + EPS)) * fg_ref[...]
        o_ref[...] = out


def ffn_half_step(h, g, shift, scale, gate, w_in, w_out, next_w=None, final_g=None):
    m, d = h.shape
    f = w_out.shape[0]
    tm = min(m, 512)
    tf = 512
    nf = f // tf
    n_i = m // tm
    row = lambda i, j: (0, 0)
    fg = final_g if final_g is not None else g
    in_specs = [
        pl.BlockSpec((tm, d), lambda i, j: (i, 0)),
        pl.BlockSpec((1, d), row), pl.BlockSpec((1, d), row),
        pl.BlockSpec((1, d), row), pl.BlockSpec((1, d), row),
        pl.BlockSpec((d, tf), lambda i, j: (0, j)),
        pl.BlockSpec((d, tf), lambda i, j: (0, j + nf)),
        pl.BlockSpec((tf, d), lambda i, j: (j, 0)),
        pl.BlockSpec((1, d), row),
    ]
    out_specs = [pl.BlockSpec((tm, d), lambda i, j: (i, 0))]
    out_shape = [jax.ShapeDtypeStruct((m, d), F32)]
    args = [h, g, shift, scale, gate, w_in, w_in, w_out, fg]
    if next_w is not None:
        nw_in, nw_out, layer, k = next_w
        ri, ci, ro = d // n_i, 2 * f // nf, f // (n_i * nf)
        assert ri % 16 == 0 and ci % LANES == 0 and ro % 16 == 0
        in_specs += [pl.BlockSpec((None, None, ri, ci), lambda i, j: (layer, k, i, j)),
                     pl.BlockSpec((None, None, ro, d), lambda i, j: (layer, k, i * nf + j, 0))]
        out_specs += [pl.BlockSpec((ri, ci), lambda i, j: (i, j)),
                      pl.BlockSpec((ro, d), lambda i, j: (i * nf + j, 0))]
        out_shape += [jax.ShapeDtypeStruct((d, 2 * f), BF16), jax.ShapeDtypeStruct((f, d), BF16)]
        args += [nw_in, nw_out]
    outs = pl.pallas_call(
        functools.partial(_ffn_kernel, final_norm=final_g is not None, cast_next=next_w is not None),
        grid=(n_i, nf),
        in_specs=in_specs,
        out_specs=out_specs,
        out_shape=out_shape,
        scratch_shapes=[pltpu.VMEM((tm, d), BF16), pltpu.VMEM((tm, d), F32), pltpu.VMEM((tm, LANES), F32)],
        compiler_params=_params("parallel", "arbitrary"),
        name="ffn_half_step",
    )(*args)
    return outs if next_w is not None else outs[0]


def _nmm_kernel(h_ref, g_ref, sh_ref, sc_ref, w_ref, o_ref, xn_sc, rs_sc, *, precise):
    @pl.when(pl.program_id(1) == 0)
    def _():
        _norm_modulate_into(xn_sc, h_ref, g_ref, sh_ref, sc_ref, rs_sc)

    if precise:
        xf, wf = xn_sc[...], w_ref[...]
        xh, wh = xf.astype(BF16), wf.astype(BF16)
        xl, wl = (xf - xh.astype(F32)).astype(BF16), (wf - wh.astype(F32)).astype(BF16)
        y = _dot(xh, wh) + (_dot(xh, wl) + _dot(xl, wh))
    else:
        y = _dot(xn_sc[...], w_ref[...].astype(BF16))
    o_ref[...] = y.astype(o_ref.dtype)


def norm_mod_matmul(h, g, shift, scale, w, layer, out_dtype, n=None, precise=False):
    m, d = h.shape
    n = w.shape[2] if n is None else n
    tm = min(m, 1024)
    tn = n if n <= 1024 else (1024 if n % 1024 == 0 else 512)
    row = lambda i, j: (0, 0)
    return pl.pallas_call(
        functools.partial(_nmm_kernel, precise=precise),
        grid=(m // tm, n // tn),
        in_specs=[
            pl.BlockSpec((tm, d), lambda i, j: (i, 0)),
            pl.BlockSpec((1, d), row), pl.BlockSpec((1, d), row), pl.BlockSpec((1, d), row),
            pl.BlockSpec((None, d, tn), lambda i, j: (layer, 0, j)),
        ],
        out_specs=pl.BlockSpec((tm, tn), lambda i, j: (i, j)),
        out_shape=jax.ShapeDtypeStruct((m, n), out_dtype),
        scratch_shapes=[pltpu.VMEM((tm, d), F32 if precise else BF16), pltpu.VMEM((tm, LANES), F32)],
        compiler_params=_params("parallel", "arbitrary"),
        name="norm_mod_matmul",
    )(h, g, shift, scale, w)


def _mgr_kernel(a_ref, w_ref, h_ref, gate_ref, o_ref):
    o_ref[...] = h_ref[...] + gate_ref[...] * _dot(a_ref[...], w_ref[...])


def matmul_gated_residual(a, w, layer, h, gate):
    m, k = a.shape
    n = w.shape[2]
    tm = min(m, 512)
    tn = n if k * n * 2 <= 8 * 1024 * 1024 else min(n, 1024)
    return pl.pallas_call(
        _mgr_kernel,
        grid=(m // tm, n // tn),
        in_specs=[
            pl.BlockSpec((tm, k), lambda i, j: (i, 0)),
            pl.BlockSpec((None, k, tn), lambda i, j: (layer, 0, j)),
            pl.BlockSpec((tm, tn), lambda i, j: (i, j)),
            pl.BlockSpec((1, tn), lambda i, j: (0, j)),
        ],
        out_specs=pl.BlockSpec((tm, tn), lambda i, j: (i, j)),
        out_shape=jax.ShapeDtypeStruct((m, n), F32),
        compiler_params=_params("parallel", "parallel"),
        name="matmul_gated_residual",
    )(a, w, h, gate)


def _ctx_attn_kernel(q_ref, k_ref, v_ref, o_ref):
    q = q_ref[...]
    s = _dot_nt(q, k_ref[...]) * (q.shape[-1] ** -0.5)
    m = jnp.max(s, axis=-1, keepdims=True)
    p = jnp.exp(s - m)
    l = jnp.sum(p, axis=-1, keepdims=True)
    o_ref[...] = (_dot(p.astype(BF16), v_ref[...]) / l).astype(o_ref.dtype)


def ctx_attention(qkv, n_heads, q_blk, k_blk, v_blk):
    n_ctx = qkv.shape[0]
    spec = lambda f: pl.BlockSpec((n_ctx, LANES), lambda h: (0, f(h)))
    return pl.pallas_call(
        _ctx_attn_kernel,
        grid=(n_heads,),
        in_specs=[spec(q_blk), spec(k_blk), spec(v_blk)],
        out_specs=pl.BlockSpec((n_ctx, LANES), lambda h: (0, h)),
        out_shape=jax.ShapeDtypeStruct((n_ctx, n_heads * LANES), BF16),
        compiler_params=_params("parallel"),
        name="ctx_attention",
    )(qkv, qkv, qkv)


def _na_kernel(q_ref, k_ref, v_ref, kc_ref, vc_ref, pair_ref, *rest, n_rows):
    mask_refs, o_ref = rest[:-1], rest[-1]
    nq = NA_Q_ROWS * GRID_W
    n_band = NA_BAND_ROWS * GRID_W
    scale = q_ref.shape[-1] ** -0.5 * LOG2_E
    for blk, mask_ref in enumerate(mask_refs):
        r0 = (pl.program_id(1) * len(mask_refs) + blk) * NA_Q_ROWS
        b0 = jnp.clip(r0 - NA_WIN_ROWS // 2, 0, n_rows - NA_BAND_ROWS)
        start = pl.multiple_of(b0 * GRID_W, GRID_W)
        q = q_ref[blk * nq:(blk + 1) * nq, :]
        bias = jnp.concatenate([
            jnp.concatenate([
                pair_ref[0, jnp.clip(b0 + 2 * t - r0 - qi + NA_WIN_ROWS, 0, 2 * NA_WIN_ROWS - 1)]
                for t in range(NA_BAND_ROWS // 2)], axis=1)
            for qi in range(NA_Q_ROWS)], axis=0)
        keys = jnp.concatenate([k_ref[pl.ds(start, n_band), :], kc_ref[...]], axis=0)
        vals = jnp.concatenate([v_ref[pl.ds(start, n_band), :], vc_ref[...]], axis=0)
        s = _dot_nt(q, keys) * scale
        s_w = s[:, :n_band] + (bias + mask_ref[0])
        s_c = s[:, n_band:]
        m = jnp.maximum(jnp.max(s_w, axis=-1, keepdims=True), jnp.max(s_c, axis=-1, keepdims=True))
        p = jnp.concatenate([jnp.exp2(s_w - m), jnp.exp2(s_c - m)], axis=1)
        l = jnp.sum(p, axis=-1, keepdims=True)
        o_ref[blk * nq:(blk + 1) * nq, :] = (_dot(p.astype(BF16), vals) / l).astype(o_ref.dtype)


def _na_window_masks(n_rows):
    nq, nk = NA_Q_ROWS * GRID_W, NA_BAND_ROWS * GRID_W
    valid = []
    for r0 in (0, NA_Q_ROWS, n_rows - NA_Q_ROWS):
        b0 = int(np.clip(r0 - NA_WIN_ROWS // 2, 0, n_rows - NA_BAND_ROWS))
        qr = r0 + np.arange(nq) // GRID_W
        qc = np.arange(nq) % GRID_W
        kr = b0 + np.arange(nk) // GRID_W
        kc = np.arange(nk) % GRID_W
        rs = np.clip(qr - NA_WIN_ROWS // 2, 0, n_rows - NA_WIN_ROWS)
        cs = np.clip(qc - NA_WIN_COLS // 2, 0, GRID_W - NA_WIN_COLS)
        ok = ((kr[None] >= rs[:, None]) & (kr[None] < rs[:, None] + NA_WIN_ROWS)
              & (kc[None] >= cs[:, None]) & (kc[None] < cs[:, None] + NA_WIN_COLS))
        valid.append(ok)
    return np.where(np.stack(valid), 0.0, MASK_VALUE).astype(np.float32)


def _na_pair_tiles(rpb):
    n_heads = rpb.shape[0]
    dcol = np.arange(GRID_W)[None, :] - np.arange(GRID_W)[:, None] + NA_WIN_COLS - 1
    in_table = (dcol >= 0) & (dcol < 2 * NA_WIN_COLS - 1)
    onehot = (np.clip(dcol, 0, 2 * NA_WIN_COLS - 2)[None] == np.arange(2 * NA_WIN_COLS - 1)[:, None, None]) & in_table
    toeplitz = jnp.einsum("hrd,dqk->hrqk", rpb, jnp.asarray(onehot, F32), precision=lax.Precision.HIGHEST)
    zero = jnp.zeros((n_heads, 1, GRID_W, GRID_W), F32)
    padded = jnp.concatenate([zero, toeplitz, zero], axis=1)
    return jnp.concatenate([padded[:, :-1], padded[:, 1:]], axis=-1)


def neighbourhood_attention(qkv, qkv_c, rpb):
    n_tok = qkv.shape[0]
    n_ctx = qkv_c.shape[0]
    n_rows = n_tok // GRID_W
    n_blocks = n_rows // NA_Q_ROWS
    nb = NA_BLOCKS_PER_STEP
    nq, nk = NA_Q_ROWS * GRID_W, NA_BAND_ROWS * GRID_W
    assert NA_BAND_ROWS % 2 == 0
    pairs = _na_pair_tiles(rpb) * LOG2_E
    masks = jnp.asarray(_na_window_masks(n_rows))
    h_ = NA_HEADS

    def mask_spec(blk):
        def index(h, s):
            rb = s * nb + blk
            return (jnp.where(rb == 0, 0, jnp.where(rb == n_blocks - 1, 2, 1)), 0, 0)
        return pl.BlockSpec((1, nq, nk), index)

    return pl.pallas_call(
        functools.partial(_na_kernel, n_rows=n_rows),
        grid=(h_, n_blocks // nb),
        in_specs=[
            pl.BlockSpec((nb * nq, LANES), lambda h, s: (s, h)),
            pl.BlockSpec((n_tok, LANES), lambda h, s: (0, h_ + h)),
            pl.BlockSpec((n_tok, LANES), lambda h, s: (0, 2 * h_ + h)),
            pl.BlockSpec((n_ctx, LANES), lambda h, s: (0, h_ + h)),
            pl.BlockSpec((n_ctx, LANES), lambda h, s: (0, 2 * h_ + h)),
            pl.BlockSpec((1,) + pairs.shape[1:], lambda h, s: (h, 0, 0, 0)),
        ] + [mask_spec(blk) for blk in range(nb)],
        out_specs=pl.BlockSpec((nb * nq, LANES), lambda h, s: (s, h)),
        out_shape=jax.ShapeDtypeStruct((n_tok, h_ * LANES), BF16),
        compiler_params=_params("parallel", "arbitrary"),
        name="neighbourhood_attention",
    )(qkv, qkv, qkv, qkv_c, qkv_c, pairs, *([masks] * nb))


def _gqa_proj_kernel(h_ref, g_ref, sh_ref, sc_ref, w_ref, qn_ref, kn_ref, cos_ref, sin_ref, o_ref, xn_sc,
                     rs_sc, *, rope, q_scale):
    j = pl.program_id(1)
    tn = o_ref.shape[1]
    n_q = GQA_HEADS * LANES // tn

    @pl.when(j == 0)
    def _():
        _norm_modulate_into(xn_sc, h_ref, g_ref, sh_ref, sc_ref, rs_sc)

    y = _dot(xn_sc[...], w_ref[...].astype(BF16))

    def normed(w, mult):
        for s in range(tn // LANES):
            cols = slice(s * LANES, (s + 1) * LANES)
            x = y[:, cols]
            var = jnp.mean(x * x, axis=-1, keepdims=True)
            t = (x * lax.rsqrt(var + EPS)) * w
            if rope:
                t = t * cos_ref[...] + pltpu.roll(t, LANES // 2, 1) * sin_ref[...]
            if mult != 1.0:
                t = t * mult
            o_ref[:, cols] = t.astype(o_ref.dtype)

    @pl.when(j < n_q)
    def _():
        normed(qn_ref[...], q_scale)

    @pl.when(j == n_q)
    def _():
        normed(kn_ref[...], 1.0)

    @pl.when(j > n_q)
    def _():
        o_ref[...] = y.astype(o_ref.dtype)


def gqa_projection(h, g, shift, scale, w, layer, q_norm, k_norm, cos_full, sin_signed, rope, q_scale):
    m, d = h.shape
    n = w.shape[2]
    tm = min(m, 1024)
    tn = GQA_KV_HEADS * LANES
    row = lambda i, j: (0, 0)
    head = pl.BlockSpec((1, LANES), row)
    pos = pl.BlockSpec((tm, LANES), lambda i, j: (i, 0))
    return pl.pallas_call(
        functools.partial(_gqa_proj_kernel, rope=rope, q_scale=q_scale),
        grid=(m // tm, n // tn),
        in_specs=[
            pl.BlockSpec((tm, d), lambda i, j: (i, 0)),
            pl.BlockSpec((1, d), row), pl.BlockSpec((1, d), row), pl.BlockSpec((1, d), row),
            pl.BlockSpec((None, d, tn), lambda i, j: (layer, 0, j)),
            head, head, pos, pos,
        ],
        out_specs=pl.BlockSpec((tm, tn), lambda i, j: (i, j)),
        out_shape=jax.ShapeDtypeStruct((m, n), BF16),
        scratch_shapes=[pltpu.VMEM((tm, d), BF16), pltpu.VMEM((tm, LANES), F32)],
        compiler_params=_params("parallel", "arbitrary"),
        name="gqa_projection",
    )(h, g, shift, scale, w, q_norm[None], k_norm[None], cos_full, sin_signed)


def _gqa_kernel(q_ref, k_ref, v_ref, kc_ref, vc_ref, o_ref, m_sc, l_sc, acc_sc, *, tk):
    tq = q_ref.shape[0]
    n_tok = k_ref.shape[0]
    q = jnp.concatenate([q_ref[:, g * LANES:(g + 1) * LANES] for g in range(GQA_GROUP)], axis=0)
    m_sc[...] = jnp.full_like(m_sc, -jnp.inf)
    l_sc[...] = jnp.zeros_like(l_sc)
    acc_sc[...] = jnp.zeros_like(acc_sc)

    def update(kb, vb):
        s = _dot_nt(q, kb)
        blocks = [s[:, c * LANES:(c + 1) * LANES] for c in range(kb.shape[0] // LANES)]
        m_old = m_sc[...]
        block_max = jnp.max(functools.reduce(jnp.maximum, blocks), axis=-1, keepdims=True)
        m_new = jnp.maximum(m_old, block_max)
        alpha = jnp.exp2(m_old - m_new)
        p = [jnp.exp2(b - m_new) for b in blocks]
        l_sc[...] = alpha * l_sc[...] + functools.reduce(jnp.add, p)
        acc_sc[...] = alpha * acc_sc[...] + _dot(jnp.concatenate(p, axis=-1).astype(BF16), vb)
        m_sc[...] = m_new

    def body(j, carry):
        r = pl.multiple_of(j * tk, tk)
        update(k_ref[pl.ds(r, tk), :], v_ref[pl.ds(r, tk), :])
        return carry

    lax.fori_loop(0, n_tok // tk, body, 0, unroll=True)
    update(kc_ref[...], vc_ref[...])
    o = acc_sc[...] / jnp.sum(l_sc[...], axis=-1, keepdims=True)
    for g in range(GQA_GROUP):
        o_ref[:, g * LANES:(g + 1) * LANES] = o[g * tq:(g + 1) * tq].astype(o_ref.dtype)


def gqa_attention(qkv, qkv_c):
    n_tok = qkv.shape[0]
    n_ctx = qkv_c.shape[0]
    tq = min(n_tok, 512)
    tk = min(n_tok, 1024)
    kv0 = GQA_HEADS
    v0 = GQA_HEADS + GQA_KV_HEADS
    rows = GQA_GROUP * tq
    return pl.pallas_call(
        functools.partial(_gqa_kernel, tk=tk),
        grid=(GQA_KV_HEADS, n_tok // tq),
        in_specs=[
            pl.BlockSpec((tq, GQA_GROUP * LANES), lambda kv, i: (i, kv)),
            pl.BlockSpec((n_tok, LANES), lambda kv, i: (0, kv0 + kv)),
            pl.BlockSpec((n_tok, LANES), lambda kv, i: (0, v0 + kv)),
            pl.BlockSpec((n_ctx, LANES), lambda kv, i: (0, kv0 + kv)),
            pl.BlockSpec((n_ctx, LANES), lambda kv, i: (0, v0 + kv)),
        ],
        out_specs=pl.BlockSpec((tq, GQA_GROUP * LANES), lambda kv, i: (i, kv)),
        out_shape=jax.ShapeDtypeStruct((n_tok, GQA_HEADS * LANES), BF16),
        scratch_shapes=[pltpu.VMEM((rows, LANES), F32)] * 3,
        compiler_params=_params("parallel", "arbitrary"),
        name="gqa_attention",
    )(qkv, qkv, qkv, qkv_c, qkv_c)


def axial_rope_tables(n_tok):
    t = jnp.arange(n_tok, dtype=jnp.int32)
    row = (t // GRID_W).astype(F32)
    col = (t % GRID_W).astype(F32)
    n_freq = LANES // 4
    inv_freq = ROPE_THETA ** (-jnp.arange(n_freq, dtype=F32) / n_freq)
    ang = jnp.concatenate([row[:, None] * inv_freq, col[:, None] * inv_freq], axis=-1)
    cos, sin = jnp.cos(ang), jnp.sin(ang)
    return jnp.concatenate([cos, cos], axis=-1), jnp.concatenate([-sin, sin], axis=-1)


def _conv_silu_kernel(prev_ref, cur_ref, next_ref, w_ref, b_ref, o_ref):
    i = pl.program_id(0)
    tl = cur_ref.shape[0]
    prev = jnp.where(i == 0, 0.0, prev_ref[...].astype(F32))
    nxt = jnp.where(i == pl.num_programs(0) - 1, 0.0, next_ref[...].astype(F32))
    ext = jnp.concatenate([prev, cur_ref[...].astype(F32), nxt], axis=0)
    n_ext = ext.shape[0]
    pad = (SSM_CONV_W - 1) // 2
    acc = jnp.zeros((tl, ext.shape[1]), F32) + b_ref[...]
    for k in range(SSM_CONV_W):
        shifted = ext if k == pad else pltpu.roll(ext, (pad - k) % n_ext, 0)
        acc = acc + shifted[CONV_HALO:CONV_HALO + tl] * w_ref[k:k + 1, :]
    o_ref[...] = _silu(acc).astype(o_ref.dtype)


def conv_silu(zxbc, conv_w, conv_b):
    n = zxbc.shape[0]
    tl = min(n, 512)
    tc = 512
    c0 = SSM_D_INNER // tc
    hb = tl // CONV_HALO
    n_halo = n // CONV_HALO
    return pl.pallas_call(
        _conv_silu_kernel,
        grid=(n // tl, SSM_CONV_DIM // tc),
        in_specs=[
            pl.BlockSpec((CONV_HALO, tc), lambda i, j: (jnp.maximum(i * hb - 1, 0), c0 + j)),
            pl.BlockSpec((tl, tc), lambda i, j: (i, c0 + j)),
            pl.BlockSpec((CONV_HALO, tc), lambda i, j: (jnp.minimum((i + 1) * hb, n_halo - 1), c0 + j)),
            pl.BlockSpec((SSM_CONV_W, tc), lambda i, j: (0, j)),
            pl.BlockSpec((1, tc), lambda i, j: (0, j)),
        ],
        out_specs=pl.BlockSpec((tl, tc), lambda i, j: (i, j)),
        out_shape=jax.ShapeDtypeStruct((n, SSM_CONV_DIM), BF16),
        compiler_params=_params("parallel", "parallel"),
        name="conv_silu",
    )(zxbc, zxbc, zxbc, conv_w, conv_b)


def _ssd_prep_kernel(dt_ref, bias_ref, alog_ref, delta_ref, acs_ref):
    x = dt_ref[...] + bias_ref[...]
    delta = jnp.maximum(x, 0.0) + jnp.log1p(jnp.exp(-jnp.abs(x)))
    da = delta * (-jnp.exp(alog_ref[...]))
    q = x.shape[0]
    i = lax.broadcasted_iota(jnp.int32, (q, q), 0)
    j = lax.broadcasted_iota(jnp.int32, (q, q), 1)
    hi = lax.Precision.HIGHEST
    fwd = jnp.dot((i >= j).astype(F32), da, preferred_element_type=F32, precision=hi)
    bwd = jnp.dot((i <= j).astype(F32), da, preferred_element_type=F32, precision=hi)
    lane = lax.broadcasted_iota(jnp.int32, da.shape, 1)
    delta_ref[...] = delta
    acs_ref[...] = jnp.where(lane < SSM_HEADS, fwd, bwd)


def ssd_prep(dt, dt_bias, a_log):
    n = dt.shape[0]
    q = SSM_CHUNK
    blk = pl.BlockSpec((q, 2 * SSM_HEADS), lambda c: (c, 0))
    row = pl.BlockSpec((1, 2 * SSM_HEADS), lambda c: (0, 0))
    return pl.pallas_call(
        _ssd_prep_kernel,
        grid=(n // q,),
        in_specs=[blk, row, row],
        out_specs=[blk, blk],
        out_shape=[jax.ShapeDtypeStruct(dt.shape, F32)] * 2,
        compiler_params=_params("parallel"),
        name="ssd_prep",
    )(dt, dt_bias.reshape(1, -1), a_log.reshape(1, -1))


def _ssd_kernel(x_ref, b_ref, c_ref, ac_ref, ar_ref, dr_ref, init_ref, y_ref, fin_ref, state_sc, *, n_sub):
    d = pl.program_id(0)
    q = SSM_CHUNK

    @pl.when(pl.program_id(2) == 0)
    def _():
        state_sc[...] = init_ref[0]

    i = lax.broadcasted_iota(jnp.int32, (q, q), 0)
    j = lax.broadcasted_iota(jnp.int32, (q, q), 1)
    causal = jnp.where(d == 0, i - j, j - i) >= 0
    lo = lax.broadcasted_iota(jnp.int32, (q, LANES), 1) < SSM_HEAD_DIM

    def chunk(k, carry):
        ci = k + d * (n_sub - 1 - 2 * k)
        r = pl.multiple_of(ci * q, q)
        bm = b_ref[pl.ds(r, q), :]
        cm = c_ref[pl.ds(r, q), :]
        ac = ac_ref[0, 0, pl.ds(r, q), :]
        ar = ar_ref[0, 0, ci]
        dr = dr_ref[0, 0, ci]
        tot = jnp.where(d == 0, ar[:, q - 1:q], ar[:, 0:1])
        w_out = jnp.exp(tot - ar) * dr
        e_tot = jnp.exp(tot)
        cb = _dot_nt(cm, bm)
        bt = bm.astype(F32).T
        cf = cm.astype(F32)
        state = state_sc[...]
        for p in range(SSM_HEADS_PER_GROUP // 2):
            cols = slice(p * LANES, (p + 1) * LANES)
            x = x_ref[pl.ds(r, q), cols]
            lhs_y, lhs_s = [], []
            for hd in (2 * p, 2 * p + 1):
                a_i = jnp.broadcast_to(ac[:, hd:hd + 1], (q, q))
                seg = jnp.where(causal, a_i - ar[hd:hd + 1, :], -jnp.inf)
                intra = cb * jnp.exp(seg) * dr[hd:hd + 1, :]
                lhs_y.append(jnp.concatenate([intra, cf * jnp.exp(a_i)], axis=1).astype(BF16))
                lhs_s.append((bt * w_out[hd:hd + 1, :]).astype(BF16))
            rhs = jnp.concatenate([x, state[:, cols].astype(BF16)], axis=0)
            res = _dot(jnp.concatenate(lhs_y, axis=0), rhs)
            y_ref[0, pl.ds(r, q), cols] = jnp.where(lo, res[:q], res[q:]).astype(y_ref.dtype)
            s_res = _dot(jnp.concatenate(lhs_s, axis=0), x)
            s_new = jnp.where(lo, s_res[:SSM_STATE], s_res[SSM_STATE:])
            keep = jnp.where(lo[0:1], e_tot[2 * p:2 * p + 1, :], e_tot[2 * p + 1:2 * p + 2, :])
            state_sc[:, cols] = state[:, cols] * keep + s_new
        return carry

    lax.fori_loop(0, n_sub, chunk, 0, unroll=True)

    @pl.when(pl.program_id(2) == pl.num_programs(2) - 1)
    def _():
        fin_ref[0] = state_sc[...]


def ssd_scan(xbc, delta, acs, init_state):
    assert SSM_STATE == SSM_CHUNK == LANES
    n = xbc.shape[0]
    q = SSM_CHUNK
    nc = n // q
    n_sub = min(nc, SSD_CHUNKS_PER_STEP)
    ns = nc // n_sub
    rows = n_sub * q
    hg = SSM_HEADS_PER_GROUP
    gw = hg * SSM_HEAD_DIM
    ac_col = acs.reshape(n, 2, SSM_GROUPS, hg).transpose(1, 2, 0, 3)
    row = lambda t: t.reshape(nc, q, 2, SSM_GROUPS, hg).transpose(2, 3, 0, 4, 1)
    ac_row, dl_row = row(acs), row(delta)
    b0 = SSM_D_INNER // SSM_STATE
    c0 = b0 + SSM_GROUPS
    step = lambda d, s: s + d * (ns - 1 - 2 * s)
    row_spec = pl.BlockSpec((1, 1, n_sub, hg, q), lambda d, g, s: (d, g, step(d, s), 0, 0))
    state_spec = pl.BlockSpec((1, SSM_STATE, gw), lambda d, g, s: (d, 0, g))
    return pl.pallas_call(
        functools.partial(_ssd_kernel, n_sub=n_sub),
        grid=(2, SSM_GROUPS, ns),
        in_specs=[
            pl.BlockSpec((rows, gw), lambda d, g, s: (step(d, s), g)),
            pl.BlockSpec((rows, SSM_STATE), lambda d, g, s: (step(d, s), b0 + g)),
            pl.BlockSpec((rows, SSM_STATE), lambda d, g, s: (step(d, s), c0 + g)),
            pl.BlockSpec((1, 1, rows, hg), lambda d, g, s: (d, g, step(d, s), 0)),
            row_spec, row_spec,
            state_spec,
        ],
        out_specs=[pl.BlockSpec((1, rows, gw), lambda d, g, s: (d, step(d, s), g)), state_spec],
        out_shape=[jax.ShapeDtypeStruct((2, n, SSM_D_INNER), BF16),
                   jax.ShapeDtypeStruct((2, SSM_STATE, SSM_D_INNER), F32)],
        scratch_shapes=[pltpu.VMEM((SSM_STATE, gw), F32)],
        compiler_params=_params("parallel", "parallel", "arbitrary"),
        name="ssd_scan",
    )(xbc, xbc, xbc, ac_col, ac_row, dl_row, init_state)


def _ssm_out_kernel(yf_ref, yb_ref, x_ref, z_ref, dsum_ref, g_ref, w_ref, h_ref, gate_ref, o_ref):
    y = yf_ref[0].astype(F32) + yb_ref[0].astype(F32) + dsum_ref[...] * x_ref[...].astype(F32)
    v = y * _silu(z_ref[...].astype(F32))
    var = jnp.mean(v * v, axis=-1, keepdims=True)
    yn = ((v * lax.rsqrt(var + EPS)) * g_ref[...]).astype(BF16)
    o_ref[...] = h_ref[...] + gate_ref[...] * _dot(yn, w_ref[...])


def ssm_out_projection(y, xbc, zxbc, dsum, norm_g, w, layer, h, gate):
    n, d = h.shape
    tl = min(n, 256)
    di = SSM_D_INNER
    row = pl.BlockSpec((1, di), lambda i: (0, 0))
    return pl.pallas_call(
        _ssm_out_kernel,
        grid=(n // tl,),
        in_specs=[
            pl.BlockSpec((1, tl, di), lambda i: (0, i, 0)),
            pl.BlockSpec((1, tl, di), lambda i: (1, i, 0)),
            pl.BlockSpec((tl, di), lambda i: (i, 0)),
            pl.BlockSpec((tl, di), lambda i: (i, 0)),
            row, row,
            pl.BlockSpec((None, di, d), lambda i: (layer, 0, 0), pipeline_mode=pl.Buffered(1)),
            pl.BlockSpec((tl, d), lambda i: (i, 0)),
            pl.BlockSpec((1, d), lambda i: (0, 0)),
        ],
        out_specs=pl.BlockSpec((tl, d), lambda i: (i, 0)),
        out_shape=jax.ShapeDtypeStruct((n, d), F32),
        compiler_params=_params("parallel"),
        name="ssm_out_projection",
    )(y, y, xbc, zxbc, dsum, norm_g, w, h, gate)


def mamba2_bidirectional(h, hc, g, mod_l, mod_c, w_in, w_out_bf16, layer, conv_w, conv_b, a_log, dt_bias,
                         d_skip, norm_g):
    n_main = SSM_D_INNER + SSM_CONV_DIM
    w_dt = w_in[layer, :, n_main:][None]
    dsum = jnp.repeat(d_skip[0] + d_skip[1], SSM_HEAD_DIM)[None]
    state = jnp.zeros((2, SSM_STATE, SSM_D_INNER), F32)
    outs = []
    for t, mod in ((hc, mod_c), (h, mod_l)):
        shift, scale, gate = mod[3:4], mod[4:5], mod[5:6]
        zxbc = norm_mod_matmul(t, g, shift, scale, w_in, layer, BF16, n=n_main)
        dt = norm_mod_matmul(t, g, shift, scale, w_dt, 0, F32, precise=True)
        xbc = conv_silu(zxbc, conv_w, conv_b[None])
        delta, acs = ssd_prep(dt, dt_bias, a_log)
        y, state = ssd_scan(xbc, delta, acs, state)
        outs.append(ssm_out_projection(y, xbc, zxbc, dsum, norm_g[None], w_out_bf16, layer, t, gate))
    return outs[1], outs[0]


def kernel(x, c, ctx, c_ctx, ada_w, ada_b, norm_g, ffn_w_in, ffn_w_out, na_w_qkv, na_rpb, na_w_o, ssm_w_in, ssm_conv_w, ssm_conv_b, ssm_a_log, ssm_dt_bias, ssm_d, ssm_norm_g, ssm_w_out, gqa_w_qkv, gqa_q_norm, gqa_k_norm, gqa_w_o, final_norm_g):
    bsz, n_tok, d = x.shape
    assert bsz == 1, "the kernels take one sequence"
    depth = ada_w.shape[0]
    h, hc = x[0], ctx[0]
    mod = adaln_mod(c, c_ctx, ada_w, ada_b)
    cos_full, sin_signed = axial_rope_tables(n_tok)
    rope_off = jnp.zeros((ctx.shape[1], LANES), F32)
    ffn_w = {(0, 0): (ffn_w_in[0, 0].astype(BF16), ffn_w_out[0, 0].astype(BF16))}
    na_w_o_b, ssm_w_out_b, gqa_w_o_b = na_w_o.astype(BF16), ssm_w_out.astype(BF16), gqa_w_o.astype(BF16)
    gqa_q_scale = LANES ** -0.5 * LOG2_E

    def ffn(t, m, i, k, latent, final_g=None):
        w_in_b, w_out_b = ffn_w[(i, k)]
        nxt = (i + (k + 1) // 2, (k + 1) % 2)
        cast_next = latent and nxt[0] < depth
        out = ffn_half_step(t, norm_g[i, 2 * k][None], m[6 * k:6 * k + 1], m[6 * k + 1:6 * k + 2],
                            m[6 * k + 2:6 * k + 3], w_in_b, w_out_b,
                            (ffn_w_in, ffn_w_out) + nxt if cast_next else None, final_g)
        if cast_next:
            out, *ffn_w[nxt] = out
        return out

    for i in range(depth):
        last = i == depth - 1
        ml, mc = mod[i, 0], mod[i, 1]
        g_mix = norm_g[i, 1][None]
        h = ffn(h, ml, i, 0, True)
        hc = ffn(hc, mc, i, 0, False)
        kind, j = i % 3, i // 3
        if kind == 0:
            qkv = norm_mod_matmul(h, g_mix, ml[3:4], ml[4:5], na_w_qkv, j, BF16)
            qkv_c = norm_mod_matmul(hc, g_mix, mc[3:4], mc[4:5], na_w_qkv, j, BF16)
            o = neighbourhood_attention(qkv, qkv_c, na_rpb[j])
            h = matmul_gated_residual(o, na_w_o_b, j, h, ml[5:6])
            if not last:
                oc = ctx_attention(qkv_c, NA_HEADS, lambda hd: hd, lambda hd: NA_HEADS + hd,
                                   lambda hd: 2 * NA_HEADS + hd)
                hc = matmul_gated_residual(oc, na_w_o_b, j, hc, mc[5:6])
        elif kind == 1:
            h, hc = mamba2_bidirectional(h, hc, g_mix, ml, mc, ssm_w_in, ssm_w_out_b, j, ssm_conv_w[j],
                                         ssm_conv_b[j], ssm_a_log[j], ssm_dt_bias[j], ssm_d[j], ssm_norm_g[j])
        else:
            qkv = gqa_projection(h, g_mix, ml[3:4], ml[4:5], gqa_w_qkv, j, gqa_q_norm[j], gqa_k_norm[j],
                                 cos_full, sin_signed, True, gqa_q_scale)
            qkv_c = gqa_projection(hc, g_mix, mc[3:4], mc[4:5], gqa_w_qkv, j, gqa_q_norm[j], gqa_k_norm[j],
                                   rope_off, rope_off, False, 1.0)
            o = gqa_attention(qkv, qkv_c)
            h = matmul_gated_residual(o, gqa_w_o_b, j, h, ml[5:6])
            if not last:
                oc = ctx_attention(qkv_c, GQA_HEADS, lambda hd: hd, lambda hd: GQA_HEADS + hd // GQA_GROUP,
                                   lambda hd: GQA_HEADS + GQA_KV_HEADS + hd // GQA_GROUP)
                hc = matmul_gated_residual(oc, gqa_w_o_b, j, hc, mc[5:6])
        h = ffn(h, ml, i, 1, True, final_norm_g[None] if last else None)
        if not last:
            hc = ffn(hc, mc, i, 1, False)
    return h[None]
```

```python
import functools

import jax
import jax.numpy as jnp
import numpy as np
from jax import lax
from jax.experimental import pallas as pl
from jax.experimental.pallas import tpu as pltpu

F32 = jnp.float32
BF16 = jnp.bfloat16

EPS = 1e-6
HALF_STEP = 0.5
N_MOD = 9
GRID_W = 64
LANES = 128
NORM_ROWS = 16

NA_HEADS = 16
NA_WIN_ROWS = 8
NA_WIN_COLS = 16
NA_Q_ROWS = 4
NA_BAND_ROWS = 12
NA_BLOCKS_PER_STEP = 8

SSM_HEADS = 64
SSM_HEAD_DIM = 64
SSM_GROUPS = 8
SSM_STATE = 128
SSM_CHUNK = 128
SSM_D_INNER = SSM_HEADS * SSM_HEAD_DIM
SSM_GN = SSM_GROUPS * SSM_STATE
SSM_CONV_W = 5
SSM_CONV_DIM = SSM_D_INNER + 2 * SSM_GN
SSM_HEADS_PER_GROUP = SSM_HEADS // SSM_GROUPS
SSD_CHUNKS_PER_STEP = 16
CONV_HALO = 16

GQA_HEADS = 16
GQA_KV_HEADS = 4
GQA_GROUP = GQA_HEADS // GQA_KV_HEADS
ROPE_THETA = 10000.0
LOG2_E = 1.4426950408889634

MASK_VALUE = -1e30
VMEM_LIMIT = 56 * 1024 * 1024


def _params(*sem):
    return pltpu.CompilerParams(dimension_semantics=sem, vmem_limit_bytes=VMEM_LIMIT)


def _silu(x):
    half = 0.5 * x
    return half + half * jnp.tanh(half)


def _dot(a, b):
    return jnp.dot(a, b, preferred_element_type=F32)


def _dot_nt(a, b):
    return lax.dot_general(a, b, (((1,), (1,)), ((), ())), preferred_element_type=F32)


def _rowwise(n_rows, chunk, body, unroll=2):
    def step(i, carry):
        body(pl.ds(pl.multiple_of(i * chunk, chunk), chunk))
        return carry

    lax.fori_loop(0, n_rows // chunk, step, 0, unroll=unroll)


def _norm_modulate_into(dst_ref, src_ref, g_ref, sh_ref, sc_ref, rs_ref, lo_ref=None):
    gain = g_ref[...] * (1.0 + sc_ref[...])
    shift = sh_ref[...]

    n_rows, d = src_ref.shape

    def stats(rows):
        x = src_ref[rows, :]
        sq = x * x
        rs_ref[rows, :] = functools.reduce(jnp.add, [sq[:, c * LANES:(c + 1) * LANES] for c in range(d // LANES)])

    def scale(rows):
        x = src_ref[rows, :]
        rs = jnp.concatenate([rs_ref[rows, :]] * (d // LANES), axis=1)
        val = (x * rs) * gain + shift
        hi = val.astype(dst_ref.dtype)
        dst_ref[rows, :] = hi
        if lo_ref is not None:
            lo_ref[rows, :] = (val - hi.astype(F32)).astype(lo_ref.dtype)

    _rowwise(n_rows, NORM_ROWS, stats, unroll=4)
    var = jnp.sum(rs_ref[...], axis=-1, keepdims=True) * (1.0 / d)
    rs_ref[...] = jnp.broadcast_to(lax.rsqrt(var + EPS), rs_ref.shape)
    _rowwise(n_rows, NORM_ROWS, scale, unroll=2)


def _adaln_kernel(c_ref, w_ref, b_ref, o_ref, s_sc, *, rows, group):
    tk, tn = w_ref.shape
    s_sc[...] = _silu(c_ref[...])

    @pl.when(pl.program_id(2) == 0)
    def _():
        o_ref[...] = jnp.broadcast_to(b_ref[...], o_ref.shape)

    for lg in range(tn // group):
        cols = slice(lg * group, (lg + 1) * group)

        def body(i, carry):
            a0, a1 = carry
            r = pl.multiple_of(i * rows, rows)
            w = w_ref[pl.ds(r, rows), cols]
            s0 = jnp.concatenate([s_sc[0, pl.ds(r, rows), :]] * (group // LANES), axis=1)
            s1 = jnp.concatenate([s_sc[1, pl.ds(r, rows), :]] * (group // LANES), axis=1)
            return a0 + w * s0, a1 + w * s1

        zero = jnp.zeros((rows, group), F32)
        a0, a1 = lax.fori_loop(0, tk // rows, body, (zero, zero), unroll=2)
        o_ref[0:1, cols] += jnp.sum(a0, axis=0, keepdims=True)
        o_ref[1:2, cols] += jnp.sum(a1, axis=0, keepdims=True)


def adaln_mod(c, c_ctx, ada_w, ada_b):
    depth, d, n = ada_w.shape
    tk = 256
    tn = n // 4 if n % 2048 == 0 else 512
    cb = jnp.broadcast_to(jnp.stack([c[0], c_ctx])[:, :, None], (2, d, LANES))
    out = pl.pallas_call(
        functools.partial(_adaln_kernel, rows=16, group=512),
        grid=(depth, n // tn, d // tk),
        in_specs=[
            pl.BlockSpec((2, tk, LANES), lambda l, j, k: (0, k, 0)),
            pl.BlockSpec((None, tk, tn), lambda l, j, k: (l, k, j)),
            pl.BlockSpec((None, 1, tn), lambda l, j, k: (l, 0, j)),
        ],
        out_specs=pl.BlockSpec((None, 2, tn), lambda l, j, k: (l, 0, j)),
        out_shape=jax.ShapeDtypeStruct((depth, 2, n), F32),
        scratch_shapes=[pltpu.VMEM((2, tk, LANES), F32)],
        compiler_params=_params("parallel", "parallel", "arbitrary"),
        name="adaln_mod",
    )(cb, ada_w, ada_b.reshape(depth, 1, n))
    return out.reshape(depth, 2, N_MOD, d)


def _ffn_kernel(h_ref, g_ref, sh_ref, sc_ref, gate_ref, wa_ref, wb_ref, wo_ref, fg_ref, *rest, final_norm, cast_next):
    if cast_next:
        nwi_ref, nwo_ref, o_ref, nwi_out, nwo_out, xn_sc, acc_sc, rs_sc = rest
    else:
        o_ref, xn_sc, acc_sc, rs_sc = rest
    j = pl.program_id(1)

    @pl.when(j == 0)
    def _():
        _norm_modulate_into(xn_sc, h_ref, g_ref, sh_ref, sc_ref, rs_sc)
        acc_sc[...] = jnp.zeros_like(acc_sc)

    if cast_next:
        nwi_out[...] = nwi_ref[...].astype(BF16)
        nwo_out[...] = nwo_ref[...].astype(BF16)
    xn = xn_sc[...]
    a = _dot(xn, wa_ref[...])
    b = _dot(xn, wb_ref[...])
    hm = (_silu(a) * b).astype(BF16)
    acc_sc[...] += _dot(hm, wo_ref[...])

    @pl.when(j == pl.num_programs(1) - 1)
    def _():
        out = h_ref[...] + (HALF_STEP * gate_ref[...]) * acc_sc[...]
        if final_norm:
            var = jnp.mean(out * out, axis=-1, keepdims=True)
            out = (out * lax.rsqrt(var + EPS)) * fg_ref[...]
        o_ref[...] = out


def ffn_half_step(h, g, shift, scale, gate, w_in, w_out, next_w=None, final_g=None):
    m, d = h.shape
    f = w_out.shape[0]
    tm = min(m, 512)
    tf = 512
    nf = f // tf
    n_i = m // tm
    row = lambda i, j: (0, 0)
    fg = final_g if final_g is not None else g
    in_specs = [
        pl.BlockSpec((tm, d), lambda i, j: (i, 0)),
        pl.BlockSpec((1, d), row), pl.BlockSpec((1, d), row),
        pl.BlockSpec((1, d), row), pl.BlockSpec((1, d), row),
        pl.BlockSpec((d, tf), lambda i, j: (0, j)),
        pl.BlockSpec((d, tf), lambda i, j: (0, j + nf)),
        pl.BlockSpec((tf, d), lambda i, j: (j, 0)),
        pl.BlockSpec((1, d), row),
    ]
    out_specs = [pl.BlockSpec((tm, d), lambda i, j: (i, 0))]
    out_shape = [jax.ShapeDtypeStruct((m, d), F32)]
    args = [h, g, shift, scale, gate, w_in, w_in, w_out, fg]
    if next_w is not None:
        nw_in, nw_out, layer, k = next_w
        ri, ci, ro = d // n_i, 2 * f // nf, f // (n_i * nf)
        assert ri % 16 == 0 and ci % LANES == 0 and ro % 16 == 0
        in_specs += [pl.BlockSpec((None, None, ri, ci), lambda i, j: (layer, k, i, j)),
                     pl.BlockSpec((None, None, ro, d), lambda i, j: (layer, k, i * nf + j, 0))]
        out_specs += [pl.BlockSpec((ri, ci), lambda i, j: (i, j)),
                      pl.BlockSpec((ro, d), lambda i, j: (i * nf + j, 0))]
        out_shape += [jax.ShapeDtypeStruct((d, 2 * f), BF16), jax.ShapeDtypeStruct((f, d), BF16)]
        args += [nw_in, nw_out]
    outs = pl.pallas_call(
        functools.partial(_ffn_kernel, final_norm=final_g is not None, cast_next=next_w is not None),
        grid=(n_i, nf),
        in_specs=in_specs,
        out_specs=out_specs,
        out_shape=out_shape,
        scratch_shapes=[pltpu.VMEM((tm, d), BF16), pltpu.VMEM((tm, d), F32), pltpu.VMEM((tm, LANES), F32)],
        compiler_params=_params("parallel", "arbitrary"),
        name="ffn_half_step",
    )(*args)
    return outs if next_w is not None else outs[0]


def _nmm_kernel(h_ref, g_ref, sh_ref, sc_ref, w_ref, o_ref, xn_sc, rs_sc, *, precise):
    @pl.when(pl.program_id(1) == 0)
    def _():
        _norm_modulate_into(xn_sc, h_ref, g_ref, sh_ref, sc_ref, rs_sc)

    if precise:
        xf, wf = xn_sc[...], w_ref[...]
        xh, wh = xf.astype(BF16), wf.astype(BF16)
        xl, wl = (xf - xh.astype(F32)).astype(BF16), (wf - wh.astype(F32)).astype(BF16)
        y = _dot(xh, wh) + (_dot(xh, wl) + _dot(xl, wh))
    else:
        y = _dot(xn_sc[...], w_ref[...].astype(BF16))
    o_ref[...] = y.astype(o_ref.dtype)


def norm_mod_matmul(h, g, shift, scale, w, layer, out_dtype, n=None, precise=False):
    m, d = h.shape
    n = w.shape[2] if n is None else n
    tm = min(m, 1024)
    tn = n if n <= 1024 else (1024 if n % 1024 == 0 else 512)
    row = lambda i, j: (0, 0)
    return pl.pallas_call(
        functools.partial(_nmm_kernel, precise=precise),
        grid=(m // tm, n // tn),
        in_specs=[
            pl.BlockSpec((tm, d), lambda i, j: (i, 0)),
            pl.BlockSpec((1, d), row), pl.BlockSpec((1, d), row), pl.BlockSpec((1, d), row),
            pl.BlockSpec((None, d, tn), lambda i, j: (layer, 0, j)),
        ],
        out_specs=pl.BlockSpec((tm, tn), lambda i, j: (i, j)),
        out_shape=jax.ShapeDtypeStruct((m, n), out_dtype),
        scratch_shapes=[pltpu.VMEM((tm, d), F32 if precise else BF16), pltpu.VMEM((tm, LANES), F32)],
        compiler_params=_params("parallel", "arbitrary"),
        name="norm_mod_matmul",
    )(h, g, shift, scale, w)


def _ssm_in_kernel(h_ref, g_ref, sh_ref, sc_ref, w_ref, wdt_ref, o_ref, dt_ref, xn_sc, xl_sc, rs_sc):
    @pl.when(pl.program_id(1) == 0)
    def _():
        _norm_modulate_into(xn_sc, h_ref, g_ref, sh_ref, sc_ref, rs_sc, lo_ref=xl_sc)
        wf = wdt_ref[...]
        wh = wf.astype(BF16)
        wl = (wf - wh.astype(F32)).astype(BF16)
        xh = xn_sc[...]
        dt_ref[...] = _dot(xh, wh) + (_dot(xh, wl) + _dot(xl_sc[...], wh))

    o_ref[...] = _dot(xn_sc[...], w_ref[...].astype(BF16)).astype(o_ref.dtype)


def ssm_in_projection(h, g, shift, scale, w, layer, n_main):
    m, d = h.shape
    n_dt = w.shape[2] - n_main
    w_dt = w[layer, :, n_main:]
    tm = min(m, 1024)
    tn = 1024
    row = lambda i, j: (0, 0)
    return pl.pallas_call(
        _ssm_in_kernel,
        grid=(m // tm, n_main // tn),
        in_specs=[
            pl.BlockSpec((tm, d), lambda i, j: (i, 0)),
            pl.BlockSpec((1, d), row), pl.BlockSpec((1, d), row), pl.BlockSpec((1, d), row),
            pl.BlockSpec((None, d, tn), lambda i, j: (layer, 0, j)),
            pl.BlockSpec((d, n_dt), row),
        ],
        out_specs=[pl.BlockSpec((tm, tn), lambda i, j: (i, j)), pl.BlockSpec((tm, n_dt), lambda i, j: (i, 0))],
        out_shape=[jax.ShapeDtypeStruct((m, n_main), BF16), jax.ShapeDtypeStruct((m, n_dt), F32)],
        scratch_shapes=[pltpu.VMEM((tm, d), BF16), pltpu.VMEM((tm, d), BF16), pltpu.VMEM((tm, LANES), F32)],
        compiler_params=_params("parallel", "arbitrary"),
        name="ssm_in_projection",
    )(h, g, shift, scale, w, w_dt)


def _mgr_kernel(a_ref, w_ref, h_ref, gate_ref, o_ref):
    o_ref[...] = h_ref[...] + gate_ref[...] * _dot(a_ref[...], w_ref[...])


def matmul_gated_residual(a, w, layer, h, gate):
    m, k = a.shape
    n = w.shape[2]
    tm = min(m, 512)
    tn = n if k * n * 2 <= 8 * 1024 * 1024 else min(n, 1024)
    return pl.pallas_call(
        _mgr_kernel,
        grid=(m // tm, n // tn),
        in_specs=[
            pl.BlockSpec((tm, k), lambda i, j: (i, 0)),
            pl.BlockSpec((None, k, tn), lambda i, j: (layer, 0, j)),
            pl.BlockSpec((tm, tn), lambda i, j: (i, j)),
            pl.BlockSpec((1, tn), lambda i, j: (0, j)),
        ],
        out_specs=pl.BlockSpec((tm, tn), lambda i, j: (i, j)),
        out_shape=jax.ShapeDtypeStruct((m, n), F32),
        compiler_params=_params("parallel", "parallel"),
        name="matmul_gated_residual",
    )(a, w, h, gate)


def _ctx_attn_kernel(q_ref, k_ref, v_ref, o_ref):
    q = q_ref[...]
    s = _dot_nt(q, k_ref[...]) * (q.shape[-1] ** -0.5)
    m = jnp.max(s, axis=-1, keepdims=True)
    p = jnp.exp(s - m)
    l = jnp.sum(p, axis=-1, keepdims=True)
    o_ref[...] = (_dot(p.astype(BF16), v_ref[...]) / l).astype(o_ref.dtype)


def ctx_attention(qkv, n_heads, q_blk, k_blk, v_blk):
    n_ctx = qkv.shape[0]
    spec = lambda f: pl.BlockSpec((n_ctx, LANES), lambda h: (0, f(h)))
    return pl.pallas_call(
        _ctx_attn_kernel,
        grid=(n_heads,),
        in_specs=[spec(q_blk), spec(k_blk), spec(v_blk)],
        out_specs=pl.BlockSpec((n_ctx, LANES), lambda h: (0, h)),
        out_shape=jax.ShapeDtypeStruct((n_ctx, n_heads * LANES), BF16),
        compiler_params=_params("parallel"),
        name="ctx_attention",
    )(qkv, qkv, qkv)


def _na_kernel(q_ref, k_ref, v_ref, kc_ref, vc_ref, pair_ref, *rest, n_rows):
    mask_refs, o_ref = rest[:-1], rest[-1]
    nq = NA_Q_ROWS * GRID_W
    n_band = NA_BAND_ROWS * GRID_W
    scale = q_ref.shape[-1] ** -0.5 * LOG2_E
    for blk, mask_ref in enumerate(mask_refs):
        r0 = (pl.program_id(1) * len(mask_refs) + blk) * NA_Q_ROWS
        b0 = jnp.clip(r0 - NA_WIN_ROWS // 2, 0, n_rows - NA_BAND_ROWS)
        start = pl.multiple_of(b0 * GRID_W, GRID_W)
        q = q_ref[blk * nq:(blk + 1) * nq, :]
        bias = jnp.concatenate([
            jnp.concatenate([
                pair_ref[0, jnp.clip(b0 + 2 * t - r0 - qi + NA_WIN_ROWS, 0, 2 * NA_WIN_ROWS - 1)]
                for t in range(NA_BAND_ROWS // 2)], axis=1)
            for qi in range(NA_Q_ROWS)], axis=0)
        keys = jnp.concatenate([k_ref[pl.ds(start, n_band), :], kc_ref[...]], axis=0)
        vals = jnp.concatenate([v_ref[pl.ds(start, n_band), :], vc_ref[...]], axis=0)
        s = _dot_nt(q, keys) * scale
        s_w = s[:, :n_band] + (bias + mask_ref[0])
        s_c = s[:, n_band:]
        m = jnp.maximum(jnp.max(s_w, axis=-1, keepdims=True), jnp.max(s_c, axis=-1, keepdims=True))
        p = jnp.concatenate([jnp.exp2(s_w - m), jnp.exp2(s_c - m)], axis=1)
        l = jnp.sum(p, axis=-1, keepdims=True)
        o_ref[blk * nq:(blk + 1) * nq, :] = (_dot(p.astype(BF16), vals) / l).astype(o_ref.dtype)


def _na_window_masks(n_rows):
    nq, nk = NA_Q_ROWS * GRID_W, NA_BAND_ROWS * GRID_W
    valid = []
    for r0 in (0, NA_Q_ROWS, n_rows - NA_Q_ROWS):
        b0 = int(np.clip(r0 - NA_WIN_ROWS // 2, 0, n_rows - NA_BAND_ROWS))
        qr = r0 + np.arange(nq) // GRID_W
        qc = np.arange(nq) % GRID_W
        kr = b0 + np.arange(nk) // GRID_W
        kc = np.arange(nk) % GRID_W
        rs = np.clip(qr - NA_WIN_ROWS // 2, 0, n_rows - NA_WIN_ROWS)
        cs = np.clip(qc - NA_WIN_COLS // 2, 0, GRID_W - NA_WIN_COLS)
        ok = ((kr[None] >= rs[:, None]) & (kr[None] < rs[:, None] + NA_WIN_ROWS)
              & (kc[None] >= cs[:, None]) & (kc[None] < cs[:, None] + NA_WIN_COLS))
        valid.append(ok)
    return np.where(np.stack(valid), 0.0, MASK_VALUE).astype(np.float32)


def _na_pair_tiles(rpb):
    n_heads = rpb.shape[0]
    dcol = np.arange(GRID_W)[None, :] - np.arange(GRID_W)[:, None] + NA_WIN_COLS - 1
    in_table = (dcol >= 0) & (dcol < 2 * NA_WIN_COLS - 1)
    onehot = (np.clip(dcol, 0, 2 * NA_WIN_COLS - 2)[None] == np.arange(2 * NA_WIN_COLS - 1)[:, None, None]) & in_table
    toeplitz = jnp.einsum("hrd,dqk->hrqk", rpb, jnp.asarray(onehot, F32), precision=lax.Precision.HIGHEST)
    zero = jnp.zeros((n_heads, 1, GRID_W, GRID_W), F32)
    padded = jnp.concatenate([zero, toeplitz, zero], axis=1)
    return jnp.concatenate([padded[:, :-1], padded[:, 1:]], axis=-1)


def neighbourhood_attention(qkv, qkv_c, rpb):
    n_tok = qkv.shape[0]
    n_ctx = qkv_c.shape[0]
    n_rows = n_tok // GRID_W
    n_blocks = n_rows // NA_Q_ROWS
    nb = NA_BLOCKS_PER_STEP
    nq, nk = NA_Q_ROWS * GRID_W, NA_BAND_ROWS * GRID_W
    assert NA_BAND_ROWS % 2 == 0
    pairs = _na_pair_tiles(rpb) * LOG2_E
    masks = jnp.asarray(_na_window_masks(n_rows))
    h_ = NA_HEADS

    def mask_spec(blk):
        def index(h, s):
            rb = s * nb + blk
            return (jnp.where(rb == 0, 0, jnp.where(rb == n_blocks - 1, 2, 1)), 0, 0)
        return pl.BlockSpec((1, nq, nk), index)

    return pl.pallas_call(
        functools.partial(_na_kernel, n_rows=n_rows),
        grid=(h_, n_blocks // nb),
        in_specs=[
            pl.BlockSpec((nb * nq, LANES), lambda h, s: (s, h)),
            pl.BlockSpec((n_tok, LANES), lambda h, s: (0, h_ + h)),
            pl.BlockSpec((n_tok, LANES), lambda h, s: (0, 2 * h_ + h)),
            pl.BlockSpec((n_ctx, LANES), lambda h, s: (0, h_ + h)),
            pl.BlockSpec((n_ctx, LANES), lambda h, s: (0, 2 * h_ + h)),
            pl.BlockSpec((1,) + pairs.shape[1:], lambda h, s: (h, 0, 0, 0)),
        ] + [mask_spec(blk) for blk in range(nb)],
        out_specs=pl.BlockSpec((nb * nq, LANES), lambda h, s: (s, h)),
        out_shape=jax.ShapeDtypeStruct((n_tok, h_ * LANES), BF16),
        compiler_params=_params("parallel", "arbitrary"),
        name="neighbourhood_attention",
    )(qkv, qkv, qkv, qkv_c, qkv_c, pairs, *([masks] * nb))


def _gqa_proj_kernel(h_ref, g_ref, sh_ref, sc_ref, w_ref, qn_ref, kn_ref, cos_ref, sin_ref, o_ref, xn_sc,
                     rs_sc, *, rope, q_scale):
    j = pl.program_id(1)
    tn = o_ref.shape[1]
    n_q = GQA_HEADS * LANES // tn

    @pl.when(j == 0)
    def _():
        _norm_modulate_into(xn_sc, h_ref, g_ref, sh_ref, sc_ref, rs_sc)

    y = _dot(xn_sc[...], w_ref[...].astype(BF16))

    def normed(w, mult):
        for s in range(tn // LANES):
            cols = slice(s * LANES, (s + 1) * LANES)
            x = y[:, cols]
            var = jnp.mean(x * x, axis=-1, keepdims=True)
            t = (x * lax.rsqrt(var + EPS)) * w
            if rope:
                t = t * cos_ref[...] + pltpu.roll(t, LANES // 2, 1) * sin_ref[...]
            if mult != 1.0:
                t = t * mult
            o_ref[:, cols] = t.astype(o_ref.dtype)

    @pl.when(j < n_q)
    def _():
        normed(qn_ref[...], q_scale)

    @pl.when(j == n_q)
    def _():
        normed(kn_ref[...], 1.0)

    @pl.when(j > n_q)
    def _():
        o_ref[...] = y.astype(o_ref.dtype)


def gqa_projection(h, g, shift, scale, w, layer, q_norm, k_norm, cos_full, sin_signed, rope, q_scale):
    m, d = h.shape
    n = w.shape[2]
    tm = min(m, 1024)
    tn = GQA_KV_HEADS * LANES
    row = lambda i, j: (0, 0)
    head = pl.BlockSpec((1, LANES), row)
    pos = pl.BlockSpec((tm, LANES), lambda i, j: (i, 0))
    return pl.pallas_call(
        functools.partial(_gqa_proj_kernel, rope=rope, q_scale=q_scale),
        grid=(m // tm, n // tn),
        in_specs=[
            pl.BlockSpec((tm, d), lambda i, j: (i, 0)),
            pl.BlockSpec((1, d), row), pl.BlockSpec((1, d), row), pl.BlockSpec((1, d), row),
            pl.BlockSpec((None, d, tn), lambda i, j: (layer, 0, j)),
            head, head, pos, pos,
        ],
        out_specs=pl.BlockSpec((tm, tn), lambda i, j: (i, j)),
        out_shape=jax.ShapeDtypeStruct((m, n), BF16),
        scratch_shapes=[pltpu.VMEM((tm, d), BF16), pltpu.VMEM((tm, LANES), F32)],
        compiler_params=_params("parallel", "arbitrary"),
        name="gqa_projection",
    )(h, g, shift, scale, w, q_norm[None], k_norm[None], cos_full, sin_signed)


def _gqa_kernel(q_ref, k_ref, v_ref, kc_ref, vc_ref, o_ref, m_sc, l_sc, acc_sc, *, tk):
    tq = q_ref.shape[0]
    n_tok = k_ref.shape[0]
    q = jnp.concatenate([q_ref[:, g * LANES:(g + 1) * LANES] for g in range(GQA_GROUP)], axis=0)
    m_sc[...] = jnp.full_like(m_sc, -jnp.inf)
    l_sc[...] = jnp.zeros_like(l_sc)
    acc_sc[...] = jnp.zeros_like(acc_sc)

    def update(kb, vb):
        s = _dot_nt(q, kb)
        blocks = [s[:, c * LANES:(c + 1) * LANES] for c in range(kb.shape[0] // LANES)]
        m_old = m_sc[...]
        block_max = jnp.max(functools.reduce(jnp.maximum, blocks), axis=-1, keepdims=True)
        m_new = jnp.maximum(m_old, block_max)
        alpha = jnp.exp2(m_old - m_new)
        p = [jnp.exp2(b - m_new) for b in blocks]
        l_sc[...] = alpha * l_sc[...] + functools.reduce(jnp.add, p)
        acc_sc[...] = alpha * acc_sc[...] + _dot(jnp.concatenate(p, axis=-1).astype(BF16), vb)
        m_sc[...] = m_new

    def body(j, carry):
        r = pl.multiple_of(j * tk, tk)
        update(k_ref[pl.ds(r, tk), :], v_ref[pl.ds(r, tk), :])
        return carry

    lax.fori_loop(0, n_tok // tk, body, 0, unroll=True)
    update(kc_ref[...], vc_ref[...])
    o = acc_sc[...] / jnp.sum(l_sc[...], axis=-1, keepdims=True)
    for g in range(GQA_GROUP):
        o_ref[:, g * LANES:(g + 1) * LANES] = o[g * tq:(g + 1) * tq].astype(o_ref.dtype)


def gqa_attention(qkv, qkv_c):
    n_tok = qkv.shape[0]
    n_ctx = qkv_c.shape[0]
    tq = min(n_tok, 512)
    tk = min(n_tok, 1024)
    kv0 = GQA_HEADS
    v0 = GQA_HEADS + GQA_KV_HEADS
    rows = GQA_GROUP * tq
    return pl.pallas_call(
        functools.partial(_gqa_kernel, tk=tk),
        grid=(GQA_KV_HEADS, n_tok // tq),
        in_specs=[
            pl.BlockSpec((tq, GQA_GROUP * LANES), lambda kv, i: (i, kv)),
            pl.BlockSpec((n_tok, LANES), lambda kv, i: (0, kv0 + kv)),
            pl.BlockSpec((n_tok, LANES), lambda kv, i: (0, v0 + kv)),
            pl.BlockSpec((n_ctx, LANES), lambda kv, i: (0, kv0 + kv)),
            pl.BlockSpec((n_ctx, LANES), lambda kv, i: (0, v0 + kv)),
        ],
        out_specs=pl.BlockSpec((tq, GQA_GROUP * LANES), lambda kv, i: (i, kv)),
        out_shape=jax.ShapeDtypeStruct((n_tok, GQA_HEADS * LANES), BF16),
        scratch_shapes=[pltpu.VMEM((rows, LANES), F32)] * 3,
        compiler_params=_params("parallel", "arbitrary"),
        name="gqa_attention",
    )(qkv, qkv, qkv, qkv_c, qkv_c)


def axial_rope_tables(n_tok):
    t = jnp.arange(n_tok, dtype=jnp.int32)
    row = (t // GRID_W).astype(F32)
    col = (t % GRID_W).astype(F32)
    n_freq = LANES // 4
    inv_freq = ROPE_THETA ** (-jnp.arange(n_freq, dtype=F32) / n_freq)
    ang = jnp.concatenate([row[:, None] * inv_freq, col[:, None] * inv_freq], axis=-1)
    cos, sin = jnp.cos(ang), jnp.sin(ang)
    return jnp.concatenate([cos, cos], axis=-1), jnp.concatenate([-sin, sin], axis=-1)


def _conv_silu_kernel(prev_ref, cur_ref, next_ref, w_ref, b_ref, o_ref):
    i = pl.program_id(0)
    tl = cur_ref.shape[0]
    prev = jnp.where(i == 0, 0.0, prev_ref[...].astype(F32))
    nxt = jnp.where(i == pl.num_programs(0) - 1, 0.0, next_ref[...].astype(F32))
    ext = jnp.concatenate([prev, cur_ref[...].astype(F32), nxt], axis=0)
    n_ext = ext.shape[0]
    pad = (SSM_CONV_W - 1) // 2
    acc = jnp.zeros((tl, ext.shape[1]), F32) + b_ref[...]
    for k in range(SSM_CONV_W):
        shifted = ext if k == pad else pltpu.roll(ext, (pad - k) % n_ext, 0)
        acc = acc + shifted[CONV_HALO:CONV_HALO + tl] * w_ref[k:k + 1, :]
    o_ref[...] = _silu(acc).astype(o_ref.dtype)


def conv_silu(zxbc, conv_w, conv_b):
    n = zxbc.shape[0]
    tl = min(n, 512)
    tc = 512
    c0 = SSM_D_INNER // tc
    hb = tl // CONV_HALO
    n_halo = n // CONV_HALO
    return pl.pallas_call(
        _conv_silu_kernel,
        grid=(n // tl, SSM_CONV_DIM // tc),
        in_specs=[
            pl.BlockSpec((CONV_HALO, tc), lambda i, j: (jnp.maximum(i * hb - 1, 0), c0 + j)),
            pl.BlockSpec((tl, tc), lambda i, j: (i, c0 + j)),
            pl.BlockSpec((CONV_HALO, tc), lambda i, j: (jnp.minimum((i + 1) * hb, n_halo - 1), c0 + j)),
            pl.BlockSpec((SSM_CONV_W, tc), lambda i, j: (0, j)),
            pl.BlockSpec((1, tc), lambda i, j: (0, j)),
        ],
        out_specs=pl.BlockSpec((tl, tc), lambda i, j: (i, j)),
        out_shape=jax.ShapeDtypeStruct((n, SSM_CONV_DIM), BF16),
        compiler_params=_params("parallel", "parallel"),
        name="conv_silu",
    )(zxbc, zxbc, zxbc, conv_w, conv_b)


def _ssd_prep_kernel(dt_ref, bias_ref, alog_ref, delta_ref, acs_ref):
    x = dt_ref[...] + bias_ref[...]
    delta = jnp.maximum(x, 0.0) + jnp.log1p(jnp.exp(-jnp.abs(x)))
    da = delta * (-jnp.exp(alog_ref[...]))
    q = x.shape[0]
    i = lax.broadcasted_iota(jnp.int32, (q, q), 0)
    j = lax.broadcasted_iota(jnp.int32, (q, q), 1)
    hi = lax.Precision.HIGHEST
    fwd = jnp.dot((i >= j).astype(F32), da, preferred_element_type=F32, precision=hi)
    bwd = jnp.dot((i <= j).astype(F32), da, preferred_element_type=F32, precision=hi)
    lane = lax.broadcasted_iota(jnp.int32, da.shape, 1)
    delta_ref[...] = delta
    acs_ref[...] = jnp.where(lane < SSM_HEADS, fwd, bwd)


def ssd_prep(dt, dt_bias, a_log):
    n = dt.shape[0]
    q = SSM_CHUNK
    blk = pl.BlockSpec((q, 2 * SSM_HEADS), lambda c: (c, 0))
    row = pl.BlockSpec((1, 2 * SSM_HEADS), lambda c: (0, 0))
    return pl.pallas_call(
        _ssd_prep_kernel,
        grid=(n // q,),
        in_specs=[blk, row, row],
        out_specs=[blk, blk],
        out_shape=[jax.ShapeDtypeStruct(dt.shape, F32)] * 2,
        compiler_params=_params("parallel"),
        name="ssd_prep",
    )(dt, dt_bias.reshape(1, -1), a_log.reshape(1, -1))


def _ssd_kernel(x_ref, b_ref, c_ref, ac_ref, ar_ref, dr_ref, init_ref, y_ref, fin_ref, state_sc, *, n_sub):
    d = pl.program_id(0)
    q = SSM_CHUNK

    @pl.when(pl.program_id(2) == 0)
    def _():
        state_sc[...] = init_ref[0]

    i = lax.broadcasted_iota(jnp.int32, (q, q), 0)
    j = lax.broadcasted_iota(jnp.int32, (q, q), 1)
    causal = jnp.where(d == 0, i - j, j - i) >= 0
    lo = lax.broadcasted_iota(jnp.int32, (q, LANES), 1) < SSM_HEAD_DIM

    def chunk(k, carry):
        ci = k + d * (n_sub - 1 - 2 * k)
        r = pl.multiple_of(ci * q, q)
        bm = b_ref[pl.ds(r, q), :]
        cm = c_ref[pl.ds(r, q), :]
        ac = ac_ref[0, 0, pl.ds(r, q), :]
        ar = ar_ref[0, 0, ci]
        dr = dr_ref[0, 0, ci]
        tot = jnp.where(d == 0, ar[:, q - 1:q], ar[:, 0:1])
        w_out = jnp.exp(tot - ar) * dr
        e_tot = jnp.exp(tot)
        cb = _dot_nt(cm, bm)
        bt = bm.astype(F32).T
        cf = cm.astype(F32)
        state = state_sc[...]
        for p in range(SSM_HEADS_PER_GROUP // 2):
            cols = slice(p * LANES, (p + 1) * LANES)
            x = x_ref[pl.ds(r, q), cols]
            lhs_y, lhs_s = [], []
            for hd in (2 * p, 2 * p + 1):
                a_i = jnp.broadcast_to(ac[:, hd:hd + 1], (q, q))
                seg = jnp.where(causal, a_i - ar[hd:hd + 1, :], -jnp.inf)
                intra = cb * jnp.exp(seg) * dr[hd:hd + 1, :]
                lhs_y.append(jnp.concatenate([intra, cf * jnp.exp(a_i)], axis=1).astype(BF16))
                lhs_s.append((bt * w_out[hd:hd + 1, :]).astype(BF16))
            rhs = jnp.concatenate([x, state[:, cols].astype(BF16)], axis=0)
            res = _dot(jnp.concatenate(lhs_y, axis=0), rhs)
            y_ref[0, pl.ds(r, q), cols] = jnp.where(lo, res[:q], res[q:]).astype(y_ref.dtype)
            s_res = _dot(jnp.concatenate(lhs_s, axis=0), x)
            s_new = jnp.where(lo, s_res[:SSM_STATE], s_res[SSM_STATE:])
            keep = jnp.where(lo[0:1], e_tot[2 * p:2 * p + 1, :], e_tot[2 * p + 1:2 * p + 2, :])
            state_sc[:, cols] = state[:, cols] * keep + s_new
        return carry

    lax.fori_loop(0, n_sub, chunk, 0, unroll=True)

    @pl.when(pl.program_id(2) == pl.num_programs(2) - 1)
    def _():
        fin_ref[0] = state_sc[...]


def ssd_scan(xbc, delta, acs, init_state):
    assert SSM_STATE == SSM_CHUNK == LANES
    n = xbc.shape[0]
    q = SSM_CHUNK
    nc = n // q
    n_sub = min(nc, SSD_CHUNKS_PER_STEP)
    ns = nc // n_sub
    rows = n_sub * q
    hg = SSM_HEADS_PER_GROUP
    gw = hg * SSM_HEAD_DIM
    ac_col = acs.reshape(n, 2, SSM_GROUPS, hg).transpose(1, 2, 0, 3)
    row = lambda t: t.reshape(nc, q, 2, SSM_GROUPS, hg).transpose(2, 3, 0, 4, 1)
    ac_row, dl_row = row(acs), row(delta)
    b0 = SSM_D_INNER // SSM_STATE
    c0 = b0 + SSM_GROUPS
    step = lambda d, s: s + d * (ns - 1 - 2 * s)
    row_spec = pl.BlockSpec((1, 1, n_sub, hg, q), lambda d, g, s: (d, g, step(d, s), 0, 0))
    state_spec = pl.BlockSpec((1, SSM_STATE, gw), lambda d, g, s: (d, 0, g))
    return pl.pallas_call(
        functools.partial(_ssd_kernel, n_sub=n_sub),
        grid=(2, SSM_GROUPS, ns),
        in_specs=[
            pl.BlockSpec((rows, gw), lambda d, g, s: (step(d, s), g)),
            pl.BlockSpec((rows, SSM_STATE), lambda d, g, s: (step(d, s), b0 + g)),
            pl.BlockSpec((rows, SSM_STATE), lambda d, g, s: (step(d, s), c0 + g)),
            pl.BlockSpec((1, 1, rows, hg), lambda d, g, s: (d, g, step(d, s), 0)),
            row_spec, row_spec,
            state_spec,
        ],
        out_specs=[pl.BlockSpec((1, rows, gw), lambda d, g, s: (d, step(d, s), g)), state_spec],
        out_shape=[jax.ShapeDtypeStruct((2, n, SSM_D_INNER), BF16),
                   jax.ShapeDtypeStruct((2, SSM_STATE, SSM_D_INNER), F32)],
        scratch_shapes=[pltpu.VMEM((SSM_STATE, gw), F32)],
        compiler_params=_params("parallel", "parallel", "arbitrary"),
        name="ssd_scan",
    )(xbc, xbc, xbc, ac_col, ac_row, dl_row, init_state)


def _ssm_out_kernel(yf_ref, yb_ref, x_ref, z_ref, dsum_ref, g_ref, w_ref, h_ref, gate_ref, o_ref):
    y = yf_ref[0].astype(F32) + yb_ref[0].astype(F32) + dsum_ref[...] * x_ref[...].astype(F32)
    v = y * _silu(z_ref[...].astype(F32))
    var = jnp.mean(v * v, axis=-1, keepdims=True)
    yn = ((v * lax.rsqrt(var + EPS)) * g_ref[...]).astype(BF16)
    o_ref[...] = h_ref[...] + gate_ref[...] * _dot(yn, w_ref[...])


def ssm_out_projection(y, xbc, zxbc, dsum, norm_g, w, layer, h, gate):
    n, d = h.shape
    tl = min(n, 256)
    di = SSM_D_INNER
    row = pl.BlockSpec((1, di), lambda i: (0, 0))
    return pl.pallas_call(
        _ssm_out_kernel,
        grid=(n // tl,),
        in_specs=[
            pl.BlockSpec((1, tl, di), lambda i: (0, i, 0)),
            pl.BlockSpec((1, tl, di), lambda i: (1, i, 0)),
            pl.BlockSpec((tl, di), lambda i: (i, 0)),
            pl.BlockSpec((tl, di), lambda i: (i, 0)),
            row, row,
            pl.BlockSpec((None, di, d), lambda i: (layer, 0, 0), pipeline_mode=pl.Buffered(1)),
            pl.BlockSpec((tl, d), lambda i: (i, 0)),
            pl.BlockSpec((1, d), lambda i: (0, 0)),
        ],
        out_specs=pl.BlockSpec((tl, d), lambda i: (i, 0)),
        out_shape=jax.ShapeDtypeStruct((n, d), F32),
        compiler_params=_params("parallel"),
        name="ssm_out_projection",
    )(y, y, xbc, zxbc, dsum, norm_g, w, h, gate)


def mamba2_bidirectional(h, hc, g, mod_l, mod_c, w_in, w_out_bf16, layer, conv_w, conv_b, a_log, dt_bias,
                         d_skip, norm_g):
    n_main = SSM_D_INNER + SSM_CONV_DIM
    dsum = jnp.repeat(d_skip[0] + d_skip[1], SSM_HEAD_DIM)[None]
    state = jnp.zeros((2, SSM_STATE, SSM_D_INNER), F32)
    outs = []
    for t, mod in ((hc, mod_c), (h, mod_l)):
        shift, scale, gate = mod[3:4], mod[4:5], mod[5:6]
        zxbc, dt = ssm_in_projection(t, g, shift, scale, w_in, layer, n_main)
        xbc = conv_silu(zxbc, conv_w, conv_b[None])
        delta, acs = ssd_prep(dt, dt_bias, a_log)
        y, state = ssd_scan(xbc, delta, acs, state)
        outs.append(ssm_out_projection(y, xbc, zxbc, dsum, norm_g[None], w_out_bf16, layer, t, gate))
    return outs[1], outs[0]


def kernel(x, c, ctx, c_ctx, ada_w, ada_b, norm_g, ffn_w_in, ffn_w_out, na_w_qkv, na_rpb, na_w_o, ssm_w_in, ssm_conv_w, ssm_conv_b, ssm_a_log, ssm_dt_bias, ssm_d, ssm_norm_g, ssm_w_out, gqa_w_qkv, gqa_q_norm, gqa_k_norm, gqa_w_o, final_norm_g):
    bsz, n_tok, d = x.shape
    assert bsz == 1, "the kernels take one sequence"
    depth = ada_w.shape[0]
    h, hc = x[0], ctx[0]
    mod = adaln_mod(c, c_ctx, ada_w, ada_b)
    cos_full, sin_signed = axial_rope_tables(n_tok)
    rope_off = jnp.zeros((ctx.shape[1], LANES), F32)
    ffn_w = {(0, 0): (ffn_w_in[0, 0].astype(BF16), ffn_w_out[0, 0].astype(BF16))}
    na_w_o_b, ssm_w_out_b, gqa_w_o_b = na_w_o.astype(BF16), ssm_w_out.astype(BF16), gqa_w_o.astype(BF16)
    gqa_q_scale = LANES ** -0.5 * LOG2_E

    def ffn(t, m, i, k, latent, final_g=None):
        w_in_b, w_out_b = ffn_w[(i, k)]
        nxt = (i + (k + 1) // 2, (k + 1) % 2)
        cast_next = latent and nxt[0] < depth
        out = ffn_half_step(t, norm_g[i, 2 * k][None], m[6 * k:6 * k + 1], m[6 * k + 1:6 * k + 2],
                            m[6 * k + 2:6 * k + 3], w_in_b, w_out_b,
                            (ffn_w_in, ffn_w_out) + nxt if cast_next else None, final_g)
        if cast_next:
            out, *ffn_w[nxt] = out
        return out

    for i in range(depth):
        last = i == depth - 1
        ml, mc = mod[i, 0], mod[i, 1]
        g_mix = norm_g[i, 1][None]
        h = ffn(h, ml, i, 0, True)
        hc = ffn(hc, mc, i, 0, False)
        kind, j = i % 3, i // 3
        if kind == 0:
            qkv = norm_mod_matmul(h, g_mix, ml[3:4], ml[4:5], na_w_qkv, j, BF16)
            qkv_c = norm_mod_matmul(hc, g_mix, mc[3:4], mc[4:5], na_w_qkv, j, BF16)
            o = neighbourhood_attention(qkv, qkv_c, na_rpb[j])
            h = matmul_gated_residual(o, na_w_o_b, j, h, ml[5:6])
            if not last:
                oc = ctx_attention(qkv_c, NA_HEADS, lambda hd: hd, lambda hd: NA_HEADS + hd,
                                   lambda hd: 2 * NA_HEADS + hd)
                hc = matmul_gated_residual(oc, na_w_o_b, j, hc, mc[5:6])
        elif kind == 1:
            h, hc = mamba2_bidirectional(h, hc, g_mix, ml, mc, ssm_w_in, ssm_w_out_b, j, ssm_conv_w[j],
                                         ssm_conv_b[j], ssm_a_log[j], ssm_dt_bias[j], ssm_d[j], ssm_norm_g[j])
        else:
            qkv = gqa_projection(h, g_mix, ml[3:4], ml[4:5], gqa_w_qkv, j, gqa_q_norm[j], gqa_k_norm[j],
                                 cos_full, sin_signed, True, gqa_q_scale)
            qkv_c = gqa_projection(hc, g_mix, mc[3:4], mc[4:5], gqa_w_qkv, j, gqa_q_norm[j], gqa_k_norm[j],
                                   rope_off, rope_off, False, 1.0)
            o = gqa_attention(qkv, qkv_c)
            h = matmul_gated_residual(o, gqa_w_o_b, j, h, ml[5:6])
            if not last:
                oc = ctx_attention(qkv_c, GQA_HEADS, lambda hd: hd, lambda hd: GQA_HEADS + hd // GQA_GROUP,
                                   lambda hd: GQA_HEADS + GQA_KV_HEADS + hd // GQA_GROUP)
                hc = matmul_gated_residual(oc, gqa_w_o_b, j, hc, mc[5:6])
        h = ffn(h, ml, i, 1, True, final_norm_g[None] if last else None)
        if not last:
            hc = ffn(hc, mc, i, 1, False)
    return h[None]
```
